```python
import math
import jax, jax.numpy as jnp
from jax import lax
import numpy as np

D_MODEL = 2048
BATCH = 1
SEQ = 8192
DEPTH = 1

D_MIX = D_MODEL
HEAD_DIM = 64
D_RWKV = D_MIX // 2
N_RWKV_HEADS = D_RWKV // HEAD_DIM
DECAY_LORA = 64
AAA_LORA = 64
D_ATT = D_MIX - D_RWKV
N_Q_HEADS = D_ATT // HEAD_DIM
N_KV_HEADS = 2
KV_GROUP = N_Q_HEADS // N_KV_HEADS
D_KV = N_KV_HEADS * HEAD_DIM
WINDOW = 128
BLOCK = 128
N_BUCKETS = 32
MAX_EXACT = N_BUCKETS // 2
MAX_DISTANCE = 128
NORM_EPS = 1e-6
LNX_EPS = 64e-5
RWKV_SPLITS = [D_RWKV, 2 * D_RWKV, 3 * D_RWKV, 4 * D_RWKV, 4 * D_RWKV + DECAY_LORA]
RWKV_COLS = 4 * D_RWKV + DECAY_LORA + AAA_LORA
ATT_SPLITS = [D_ATT, D_ATT + D_KV, D_ATT + 2 * D_KV]
ATT_COLS = 2 * D_ATT + 2 * D_KV
D_IN = RWKV_COLS + ATT_COLS

kernel_name = "hymba_rwkv7_swa_sink_hybrid"


def rms_norm(x, g, eps):
    xf = x.astype(jnp.float32)
    y = xf * lax.rsqrt(jnp.mean(xf * xf, axis=-1, keepdims=True) + eps)
    return y * g.astype(jnp.float32)


def rwkv7_mixer(z, mu, w0, w2, a0, a2, k_k, k_a, r_k, lnx_w, lnx_b):
    B_, S_ = z.shape[0], z.shape[1]
    z_prev = jnp.pad(z, ((0, 0), (1, 0), (0, 0)))[:, :-1]
    z = z + (z_prev - z) * mu
    r, k, v, g, wl, al = jnp.split(z, RWKV_SPLITS, axis=-1)
    w = -jax.nn.softplus(-(w0 + jnp.tanh(wl) @ w2)) - 0.5
    decay = jnp.exp(-jnp.exp(w))
    a = jax.nn.sigmoid(a0 + al @ a2)
    hs = lambda t: t.reshape(B_, S_, N_RWKV_HEADS, HEAD_DIM)
    kk = hs(k * k_k)
    kk = kk * lax.rsqrt(jnp.maximum(jnp.sum(kk * kk, axis=-1, keepdims=True), 1e-24))
    k = k * (1.0 + (a - 1.0) * k_a)
    r, k, v, decay, a = hs(r), hs(k), hs(v), hs(decay), hs(a)

    def step(state, inp):
        r_t, w_t, k_t, v_t, kk_t, a_t = inp
        sa = jnp.einsum('bhvk,bhk->bhv', state, -kk_t)
        state = (state * w_t[:, :, None, :]
                 + sa[..., None] * (kk_t * a_t)[:, :, None, :]
                 + v_t[..., None] * k_t[:, :, None, :])
        return state, jnp.einsum('bhvk,bhk->bhv', state, r_t)

    tm = lambda t: jnp.moveaxis(t, 1, 0)
    s0 = jnp.zeros((B_, N_RWKV_HEADS, HEAD_DIM, HEAD_DIM), jnp.float32)
    _, o = lax.scan(step, s0, (tm(r), tm(decay), tm(k), tm(v), tm(kk), tm(a)))
    o = jnp.moveaxis(o, 0, 1)
    mean = jnp.mean(o, axis=-1, keepdims=True)
    var = jnp.mean(jnp.square(o - mean), axis=-1, keepdims=True)
    o = ((o - mean) * lax.rsqrt(var + LNX_EPS)).reshape(B_, S_, D_RWKV) * lnx_w + lnx_b
    bonus = jnp.sum(r * k * r_k, axis=-1, keepdims=True) * v
    o = o + bonus.reshape(B_, S_, D_RWKV)
    return o * jax.nn.silu(g)


def t5_bucket(dist):
    n = jnp.maximum(dist, 0)
    nf = jnp.maximum(n, 1).astype(jnp.float32)
    large = MAX_EXACT + (jnp.log(nf / MAX_EXACT) / math.log(MAX_DISTANCE / MAX_EXACT)
                         * (N_BUCKETS - MAX_EXACT)).astype(jnp.int32)
    large = jnp.minimum(large, N_BUCKETS - 1)
    return jnp.where(n < MAX_EXACT, n, large)


def swa_sink_mixer(z, q_norm_w, k_norm_w, sinks, rel_bias):
    B_, S_ = z.shape[0], z.shape[1]
    nb = S_ // BLOCK
    q, k, v, g = jnp.split(z, ATT_SPLITS, axis=-1)
    q = rms_norm(q.reshape(B_, S_, N_Q_HEADS, HEAD_DIM), q_norm_w, NORM_EPS)
    k = rms_norm(k.reshape(B_, S_, N_KV_HEADS, HEAD_DIM), k_norm_w, NORM_EPS)
    v = v.reshape(B_, S_, N_KV_HEADS, HEAD_DIM)
    qb = q.reshape(B_, nb, BLOCK, N_KV_HEADS, KV_GROUP, HEAD_DIM)

    def band(t):
        tb = t.reshape(B_, nb, BLOCK, N_KV_HEADS, HEAD_DIM)
        prev = jnp.pad(tb, ((0, 0), (1, 0), (0, 0), (0, 0), (0, 0)))[:, :-1]
        return jnp.concatenate([prev, tb], axis=2)

    kb, vb = band(k), band(v)
    logits = jnp.einsum('bnqhgd,bnkhd->bnhgqk', qb, kb) * (HEAD_DIM ** -0.5)
    qi = jnp.arange(BLOCK)[:, None]
    kj = jnp.arange(2 * BLOCK)[None, :]
    dist = BLOCK + qi - kj
    bias = rel_bias.astype(jnp.float32)[t5_bucket(dist)]
    bias = jnp.transpose(bias, (2, 0, 1)).reshape(N_KV_HEADS, KV_GROUP, BLOCK, 2 * BLOCK)
    kpos = (jnp.arange(nb)[:, None, None] - 1) * BLOCK + kj[None]
    mask = (dist >= 0) & (dist < WINDOW) & (kpos >= 0)
    logits = jnp.where(mask[None, :, None, None], logits + bias, -jnp.inf)
    sink = sinks.astype(jnp.float32).reshape(1, 1, N_KV_HEADS, KV_GROUP, 1, 1)
    m = jnp.maximum(jnp.max(logits, axis=-1, keepdims=True), sink)
    e = jnp.exp(logits - m)
    p = e / (jnp.sum(e, axis=-1, keepdims=True) + jnp.exp(sink - m))
    o = jnp.einsum('bnhgqk,bnkhd->bnqhgd', p, vb).reshape(B_, S_, D_ATT)
    return o * jax.nn.silu(g)


def setup_inputs(seed: int = 0) -> dict:
    key = jax.random.key(seed)
    ks = jax.random.split(key, 20)
    f32 = jnp.float32
    nrm = lambda k, shape, s: jax.random.normal(k, shape, f32) * s
    L = DEPTH
    return {
        "x": jax.random.normal(ks[0], (BATCH, SEQ, D_MODEL), f32),
        "norm_w": 1.0 + nrm(ks[1], (L, D_MODEL), 0.02),
        "w_in": nrm(ks[2], (L, D_MODEL, D_IN), D_MODEL ** -0.5),
        "w_out": nrm(ks[3], (L, D_MIX, D_MODEL), D_MIX ** -0.5),
        "mu_rwkv": jax.random.uniform(ks[4], (L, RWKV_COLS), f32, 0.0, 1.0),
        "w0": jax.random.uniform(ks[5], (L, D_RWKV), f32, -6.0, 0.0),
        "w2": nrm(ks[6], (L, DECAY_LORA, D_RWKV), 0.1),
        "a0": nrm(ks[7], (L, D_RWKV), 0.5),
        "a2": nrm(ks[8], (L, AAA_LORA, D_RWKV), 0.5 * AAA_LORA ** -0.5),
        "k_k": 0.85 + nrm(ks[9], (L, D_RWKV), 0.05),
        "k_a": 1.0 + nrm(ks[10], (L, D_RWKV), 0.05),
        "r_k": nrm(ks[11], (L, N_RWKV_HEADS, HEAD_DIM), 0.1),
        "lnx_w": 1.0 + nrm(ks[12], (L, D_RWKV), 0.02),
        "lnx_b": nrm(ks[13], (L, D_RWKV), 0.02),
        "q_norm_w": 1.0 + nrm(ks[14], (L, HEAD_DIM), 0.02),
        "k_norm_w": 1.0 + nrm(ks[15], (L, HEAD_DIM), 0.02),
        "sinks": nrm(ks[16], (L, N_Q_HEADS), 0.5),
        "rel_bias": nrm(ks[17], (N_BUCKETS, N_Q_HEADS), 0.2),
    }


def reference(x, norm_w, w_in, w_out, mu_rwkv, w0, w2, a0, a2, k_k, k_a, r_k,
              lnx_w, lnx_b, q_norm_w, k_norm_w, sinks, rel_bias):
    for l in range(DEPTH):
        h = rms_norm(x, norm_w[l], NORM_EPS).astype(x.dtype)
        z = (h @ w_in[l]).astype(jnp.float32)
        y_rwkv = rwkv7_mixer(z[..., :RWKV_COLS], mu_rwkv[l], w0[l], w2[l], a0[l], a2[l],
                             k_k[l], k_a[l], r_k[l], lnx_w[l], lnx_b[l])
        y_att = swa_sink_mixer(z[..., RWKV_COLS:], q_norm_w[l], k_norm_w[l], sinks[l], rel_bias)
        y = jnp.concatenate([y_rwkv, y_att], axis=-1).astype(x.dtype)
        x = x + y @ w_out[l]
    return x
```

```python
import functools
import math

import jax
import jax.numpy as jnp
from jax import lax
from jax.experimental import pallas as pl
from jax.experimental.pallas import tpu as pltpu

F32 = jnp.float32
BF16 = jnp.bfloat16

D_MODEL = 2048
SEQ = 8192
HEAD_DIM = 64
D_RWKV = 1024
D_ATT = 1024
LORA = 64
N_Q_HEADS = 16
N_KV_HEADS = 2
D_KV = N_KV_HEADS * HEAD_DIM
WINDOW = 128
BLOCK = 128
N_BUCKETS = 32
MAX_EXACT = N_BUCKETS // 2
MAX_DISTANCE = 128
NORM_EPS = 1e-6
LNX_EPS = 64e-5
RWKV_COLS = 4 * D_RWKV + 2 * LORA
ATT_COLS = 2 * D_ATT + 2 * D_KV

LANES = 128
N_PAIRS = D_RWKV // LANES
CHUNK = 64
NEG_BIG = -1e30

VMEM_LIMIT = 48 * 1024 * 1024


def _dot(a, b):
    return jnp.dot(a, b, preferred_element_type=F32)


def _dot_nt(a, b):
    return lax.dot_general(a, b, (((1,), (1,)), ((), ())), preferred_element_type=F32)


def _split3(x):
    hi = x.astype(BF16)
    r1 = x - hi.astype(F32)
    mid = r1.astype(BF16)
    lo = (r1 - mid.astype(F32)).astype(BF16)
    return hi, mid, lo


def _dot_exact_rhs(x, m):
    hi, mid, lo = _split3(x)
    return _dot(hi, m) + _dot(mid, m) + _dot(lo, m)


def _dot_exact_lhs(m, x):
    hi, mid, lo = _split3(x)
    return _dot(m, hi) + _dot(m, mid) + _dot(m, lo)


def _sigmoid(x):
    return 1.0 / (1.0 + jnp.exp(-x))


def _proj_in_kernel(x_ref, nw_ref, w_ref, o_ref):
    x = x_ref[...]
    ms = jnp.mean(x * x, axis=-1, keepdims=True)
    h = x * lax.rsqrt(ms + NORM_EPS) * nw_ref[...]
    o_ref[...] = _dot(h.astype(BF16), w_ref[...])


def _proj_in(x2, norm_w, w_bf16, tm, tn, name):
    s, d = x2.shape
    n = w_bf16.shape[1]
    return pl.pallas_call(
        _proj_in_kernel,
        grid=(n // tn, s // tm),
        in_specs=[
            pl.BlockSpec((tm, d), lambda j, i: (i, 0)),
            pl.BlockSpec((1, d), lambda j, i: (0, 0)),
            pl.BlockSpec((d, tn), lambda j, i: (0, j)),
        ],
        out_specs=pl.BlockSpec((tm, tn), lambda j, i: (i, j)),
        out_shape=jax.ShapeDtypeStruct((s, n), F32),
        compiler_params=pltpu.CompilerParams(
            dimension_semantics=("arbitrary", "arbitrary"), vmem_limit_bytes=VMEM_LIMIT),
        name=name,
    )(x2, norm_w, w_bf16)


def _rwkv_kernel(z_ref, mu_ref, w0_ref, w2_ref, a0_ref, a2_ref, kk_ref, ka_ref, rk_ref,
                 lw_ref, lb_ref, o_ref, gt_ref, prev_ref):
    c = CHUNK
    i = pl.program_id(0)

    @pl.when(i == 0)
    def _():
        gt_ref[...] = jnp.zeros_like(gt_ref)
        prev_ref[...] = jnp.zeros_like(prev_ref)

    z = z_ref[...]
    row = lax.broadcasted_iota(jnp.int32, z.shape, 0)
    zprev = jnp.where(row == 0, prev_ref[...], pltpu.roll(z, 1, axis=0))
    prev_ref[...] = z[c - 1:c, :]
    zs = z + (zprev - z) * mu_ref[...]

    r_all = zs[:, 0:D_RWKV]
    k_all = zs[:, D_RWKV:2 * D_RWKV]
    v_all = zs[:, 2 * D_RWKV:3 * D_RWKV]
    g_all = zs[:, 3 * D_RWKV:4 * D_RWKV]
    lora_in = zs[:, 4 * D_RWKV:]

    wlin = w0_ref[...] + _dot(jnp.tanh(lora_in).astype(BF16), w2_ref[...])
    neg = -wlin
    softplus = jnp.maximum(neg, 0.0) + jnp.log(1.0 + jnp.exp(-jnp.abs(neg)))
    logw_all = -jnp.exp(-softplus - 0.5)
    av_all = _sigmoid(a0_ref[...] + _dot(lora_in.astype(BF16), a2_ref[...]))

    ti = lax.broadcasted_iota(jnp.int32, (c, c), 0)
    si = lax.broadcasted_iota(jnp.int32, (c, c), 1)
    tril_ones = jnp.where(si <= ti, 1.0, 0.0).astype(BF16)
    logp_all = _dot_exact_lhs(tril_ones, logw_all)

    lane = lax.broadcasted_iota(jnp.int32, (c, LANES), 1)
    head0 = lane < HEAD_DIM
    trow = lax.broadcasted_iota(jnp.int32, (c, LANES), 0)
    eye_cat = jnp.where((lane % HEAD_DIM) == trow, 1.0, 0.0)
    t2 = lax.broadcasted_iota(jnp.int32, (c, 2 * LANES), 0)
    s2 = lax.broadcasted_iota(jnp.int32, (c, 2 * LANES), 1) % HEAD_DIM
    strict2 = s2 < t2
    incl2 = s2 <= t2
    r128 = lax.broadcasted_iota(jnp.int32, (LANES, LANES), 0)
    c128 = lax.broadcasted_iota(jnp.int32, (LANES, LANES), 1)
    same_head = (r128 // HEAD_DIM) == (c128 // HEAD_DIM)
    seg_ones = jnp.where(same_head, 1.0, 0.0).astype(BF16)

    def stack2(y):
        return jnp.concatenate([jnp.where(head0, y, 0.0), jnp.where(head0, 0.0, y)], axis=0)

    def catmul(a_cat, y):
        return _dot(a_cat.astype(BF16), stack2(y).astype(BF16))

    for p in range(N_PAIRS):
        sl = slice(p * LANES, (p + 1) * LANES)
        r = r_all[:, sl]
        k = k_all[:, sl]
        v = v_all[:, sl]
        g = g_all[:, sl]
        av = av_all[:, sl]
        lw = logw_all[:, sl]
        lp = logp_all[:, sl]

        kk = k * kk_ref[:, sl]
        ss = _dot_exact_rhs(kk * kk, seg_ones)
        kk = kk * lax.rsqrt(jnp.maximum(ss, 1e-24))
        kmod = k * (1.0 + (av - 1.0) * ka_ref[:, sl])
        b = kk * av

        e_p = jnp.exp(lp)
        e_px = jnp.exp(lp - lw)
        e_n = jnp.exp(-lp)
        e_last = e_p[c - 1:c, :]
        r_t = r * e_p
        a_t = -kk * e_px
        b_t = b * e_n
        k_t = kmod * e_n

        lhs = jnp.concatenate([a_t, r_t], axis=0).astype(BF16)
        rhs = jnp.concatenate([stack2(b_t), stack2(k_t)], axis=0).astype(BF16)
        sc = _dot_nt(lhs, rhs)
        l_all = jnp.where(strict2, sc[0:c, :], 0.0)
        l_ab = l_all[:, 0:LANES]
        l_ak = l_all[:, LANES:]
        a_rbk = jnp.where(incl2, sc[c:, :], 0.0)

        x = l_ab
        tinv = eye_cat + x
        for _ in range(5):
            x = catmul(x, x)
            tinv = tinv + catmul(tinv, x)

        a_eff = catmul(tinv, a_t)
        w_loc = catmul(tinv, catmul(l_ak, v))

        gt = gt_ref[p]
        gtb = gt.astype(BF16)
        u = _dot_nt(a_eff.astype(BF16), gtb) + w_loc
        uv_stack = jnp.concatenate([stack2(u), stack2(v)], axis=0).astype(BF16)
        o = _dot_nt(r_t.astype(BF16), gtb) + _dot(a_rbk.astype(BF16), uv_stack)

        uv = jnp.concatenate([u, v], axis=0)
        bk = jnp.concatenate([b_t, k_t], axis=0) * e_last
        upd = _dot(uv.T.astype(BF16), bk.astype(BF16))
        gt_ref[p] = gt * e_last + jnp.where(same_head, upd, 0.0)

        mean = _dot_exact_rhs(o, seg_ones) * (1.0 / HEAD_DIM)
        d = o - mean
        var = _dot_exact_rhs(d * d, seg_ones) * (1.0 / HEAD_DIM)
        on = d * lax.rsqrt(var + LNX_EPS) * lw_ref[:, sl] + lb_ref[:, sl]
        bonus = _dot_exact_rhs(r * kmod * rk_ref[:, sl], seg_ones) * v
        y = (on + bonus) * (g * _sigmoid(g))
        o_ref[:, sl] = y.astype(o_ref.dtype)


def _rwkv(z_rwkv, mu, w0, w2p, a0, a2p, k_k, k_a, r_k, lnx_w, lnx_b):
    s = z_rwkv.shape[0]
    c = CHUNK
    row = lambda n: pl.BlockSpec((1, n), lambda i: (0, 0))
    return pl.pallas_call(
        _rwkv_kernel,
        grid=(s // c,),
        in_specs=[
            pl.BlockSpec((c, RWKV_COLS), lambda i: (i, 0)),
            row(RWKV_COLS), row(D_RWKV),
            pl.BlockSpec((2 * LORA, D_RWKV), lambda i: (0, 0)),
            row(D_RWKV),
            pl.BlockSpec((2 * LORA, D_RWKV), lambda i: (0, 0)),
            row(D_RWKV), row(D_RWKV), row(D_RWKV), row(D_RWKV), row(D_RWKV),
        ],
        out_specs=pl.BlockSpec((c, D_RWKV), lambda i: (i, 0)),
        out_shape=jax.ShapeDtypeStruct((s, D_RWKV), BF16),
        scratch_shapes=[
            pltpu.VMEM((N_PAIRS, LANES, LANES), F32),
            pltpu.VMEM((1, RWKV_COLS), F32),
        ],
        compiler_params=pltpu.CompilerParams(
            dimension_semantics=("arbitrary",), vmem_limit_bytes=VMEM_LIMIT),
        name="rwkv7",
    )(z_rwkv, mu, w0, w2p, a0, a2p, k_k, k_a, r_k, lnx_w, lnx_b)


def _swa_kernel(sinks_ref, relb_ref, bucket_ref, q_ref, g_ref, k_ref, v_ref, qnw_ref, knw_ref,
                o_ref, bias_ref, kprev_ref, vprev_ref):
    i = pl.program_id(0)
    bq = BLOCK

    @pl.when(i == 0)
    def _():
        kprev_ref[...] = jnp.zeros_like(kprev_ref)
        vprev_ref[...] = jnp.zeros_like(vprev_ref)
        bucket = bucket_ref[...]
        qi = lax.broadcasted_iota(jnp.int32, (bq, 2 * bq), 0)
        kj = lax.broadcasted_iota(jnp.int32, (bq, 2 * bq), 1)
        dist = bq + qi - kj
        inwin = (dist >= 0) & (dist < WINDOW)
        for h in range(N_Q_HEADS):
            acc = jnp.zeros((bq, 2 * bq), F32)
            for b in range(N_BUCKETS):
                acc = jnp.where(bucket == b, relb_ref[b, h], acc)
            acc = jnp.where(inwin, acc, NEG_BIG)
            bias_ref[0, h] = acc
            bias_ref[1, h] = jnp.where(kj >= bq, acc, NEG_BIG)

    first = jnp.where(i == 0, 1, 0)
    lane = lax.broadcasted_iota(jnp.int32, (bq, LANES), 1)
    head0 = lane < HEAD_DIM
    lane2 = lax.broadcasted_iota(jnp.int32, (2 * bq, LANES), 1)
    head0_2 = lane2 < HEAD_DIM
    r128 = lax.broadcasted_iota(jnp.int32, (LANES, LANES), 0)
    c128 = lax.broadcasted_iota(jnp.int32, (LANES, LANES), 1)
    seg_ones = jnp.where((r128 // HEAD_DIM) == (c128 // HEAD_DIM), 1.0, 0.0).astype(BF16)

    def rms_heads(t, w):
        ss = _dot_exact_rhs(t * t, seg_ones) * (1.0 / HEAD_DIM)
        return t * lax.rsqrt(ss + NORM_EPS) * w

    kn = rms_heads(k_ref[...], knw_ref[...])
    vc = v_ref[...]
    kcat = jnp.concatenate([kprev_ref[...], kn], axis=0)
    vcat = jnp.concatenate([vprev_ref[...], vc], axis=0)
    kprev_ref[...] = kn
    vprev_ref[...] = vc
    krol = pltpu.roll(kcat, HEAD_DIM, axis=1)
    vrol = pltpu.roll(vcat, HEAD_DIM, axis=1)
    kdup = [jnp.where(head0_2, kcat, krol).astype(BF16), jnp.where(head0_2, krol, kcat).astype(BF16)]
    vdup = [jnp.where(head0_2, vcat, vrol).astype(BF16), jnp.where(head0_2, vrol, vcat).astype(BF16)]

    scale = HEAD_DIM ** -0.5
    for p in range(N_PAIRS):
        sl = slice(p * LANES, (p + 1) * LANES)
        kvh = (2 * p) // (N_Q_HEADS // N_KV_HEADS)
        qn = rms_heads(q_ref[:, sl], qnw_ref[:, sl]) * scale
        q2 = jnp.concatenate([jnp.where(head0, qn, 0.0), jnp.where(head0, 0.0, qn)], axis=0).astype(BF16)
        lg2 = _dot_nt(q2, kdup[kvh])
        outs = []
        for j in range(2):
            h = 2 * p + j
            lg = lg2[j * bq:(j + 1) * bq, :] + bias_ref[first, h]
            sink = sinks_ref[h]
            m = jnp.maximum(jnp.max(lg, axis=-1, keepdims=True), sink)
            e = jnp.exp(lg - m)
            denom = jnp.sum(e, axis=-1, keepdims=True) + jnp.exp(sink - m)
            pv = _dot(e.astype(BF16), vdup[kvh])
            outs.append(pv / denom)
        o = jnp.where(head0, outs[0], outs[1])
        g = g_ref[:, sl]
        o_ref[:, sl] = (o * (g * _sigmoid(g))).astype(o_ref.dtype)


def _swa(z_att, bucket, sinks, rel_bias, qnw, knw):
    s = z_att.shape[0]
    bq = BLOCK
    smem = pl.BlockSpec(memory_space=pltpu.SMEM)
    return pl.pallas_call(
        _swa_kernel,
        grid=(s // bq,),
        in_specs=[
            smem, smem,
            pl.BlockSpec((bq, 2 * bq), lambda i: (0, 0)),
            pl.BlockSpec((bq, D_ATT), lambda i: (i, 0)),
            pl.BlockSpec((bq, D_ATT), lambda i: (i, 1)),
            pl.BlockSpec((bq, D_KV), lambda i: (i, 2 * D_ATT // D_KV)),
            pl.BlockSpec((bq, D_KV), lambda i: (i, 2 * D_ATT // D_KV + 1)),
            pl.BlockSpec((1, D_ATT), lambda i: (0, 0)),
            pl.BlockSpec((1, D_KV), lambda i: (0, 0)),
        ],
        out_specs=pl.BlockSpec((bq, D_ATT), lambda i: (i, 0)),
        out_shape=jax.ShapeDtypeStruct((s, D_ATT), BF16),
        scratch_shapes=[
            pltpu.VMEM((2, N_Q_HEADS, bq, 2 * bq), F32),
            pltpu.VMEM((bq, D_KV), F32),
            pltpu.VMEM((bq, D_KV), F32),
        ],
        compiler_params=pltpu.CompilerParams(
            dimension_semantics=("arbitrary",), vmem_limit_bytes=VMEM_LIMIT),
        name="swa_sink",
    )(sinks, rel_bias, bucket, z_att, z_att, z_att, z_att, qnw, knw)


def _proj_out_kernel(x_ref, yr_ref, ya_ref, wr_ref, wa_ref, o_ref):
    o_ref[...] = x_ref[...] + _dot(yr_ref[...], wr_ref[...]) + _dot(ya_ref[...], wa_ref[...])


def _proj_out(x2, y_rwkv, y_att, w_r, w_a, tm):
    s, d = x2.shape
    return pl.pallas_call(
        _proj_out_kernel,
        grid=(s // tm,),
        in_specs=[
            pl.BlockSpec((tm, d), lambda i: (i, 0)),
            pl.BlockSpec((tm, D_RWKV), lambda i: (i, 0)),
            pl.BlockSpec((tm, D_ATT), lambda i: (i, 0)),
            pl.BlockSpec((D_RWKV, d), lambda i: (0, 0)),
            pl.BlockSpec((D_ATT, d), lambda i: (0, 0)),
        ],
        out_specs=pl.BlockSpec((tm, d), lambda i: (i, 0)),
        out_shape=jax.ShapeDtypeStruct((s, d), F32),
        compiler_params=pltpu.CompilerParams(
            dimension_semantics=("arbitrary",), vmem_limit_bytes=VMEM_LIMIT),
        name="proj_out",
    )(x2, y_rwkv, y_att, w_r, w_a)


def _t5_bucket(dist):
    n = jnp.maximum(dist, 0)
    nf = jnp.maximum(n, 1).astype(F32)
    large = MAX_EXACT + (jnp.log(nf / MAX_EXACT) / math.log(MAX_DISTANCE / MAX_EXACT)
                         * (N_BUCKETS - MAX_EXACT)).astype(jnp.int32)
    large = jnp.minimum(large, N_BUCKETS - 1)
    return jnp.where(n < MAX_EXACT, n, large)


def kernel(x, norm_w, w_in, w_out, mu_rwkv, w0, w2, a0, a2, k_k, k_a, r_k, lnx_w, lnx_b,
           q_norm_w, k_norm_w, sinks, rel_bias):
    b, s, d = x.shape
    assert (b, s, d) == (1, SEQ, D_MODEL) and norm_w.shape[0] == 1
    x2 = x.reshape(s, d)
    l = 0
    row = lambda t: t.reshape(1, -1).astype(F32)

    w_in_l = w_in[l]
    w_rwkv = w_in_l[:, :RWKV_COLS].astype(BF16)
    wa = w_in_l[:, RWKV_COLS:]
    w_att = jnp.concatenate([wa[:, :D_ATT], wa[:, D_ATT + 2 * D_KV:], wa[:, D_ATT:D_ATT + 2 * D_KV]],
                            axis=1).astype(BF16)
    zeros_l = jnp.zeros((LORA, D_RWKV), F32)
    w2p = jnp.concatenate([w2[l], zeros_l], axis=0).astype(BF16)
    a2p = jnp.concatenate([zeros_l, a2[l]], axis=0).astype(BF16)
    w_out_r = w_out[l][:D_RWKV].astype(BF16)
    w_out_a = w_out[l][D_RWKV:].astype(BF16)
    qnw = jnp.tile(q_norm_w[l], N_Q_HEADS).reshape(1, D_ATT)
    knw = jnp.tile(k_norm_w[l], N_KV_HEADS).reshape(1, D_KV)
    qi = jnp.arange(BLOCK)[:, None]
    kj = jnp.arange(2 * BLOCK)[None, :]
    bucket = _t5_bucket(BLOCK + qi - kj).astype(jnp.int32)

    nw = row(norm_w[l])
    z_rwkv = _proj_in(x2, nw, w_rwkv, 512, 1408, "proj_in_rwkv")
    z_att = _proj_in(x2, nw, w_att, 512, 1152, "proj_in_att")

    y_rwkv = _rwkv(z_rwkv, row(mu_rwkv[l]), row(w0[l]), w2p, row(a0[l]), a2p, row(k_k[l]), row(k_a[l]),
                   row(r_k[l]), row(lnx_w[l]), row(lnx_b[l]))
    y_att = _swa(z_att, bucket, sinks[l].astype(F32), rel_bias.astype(F32), qnw, knw)

    out = _proj_out(x2, y_rwkv, y_att, w_out_r, w_out_a, 512)
    return out.reshape(b, s, d)
```

```python
import functools
import math

import jax
import jax.numpy as jnp
from jax import lax
from jax.experimental import pallas as pl
from jax.experimental.pallas import tpu as pltpu

F32 = jnp.float32
BF16 = jnp.bfloat16

D_MODEL = 2048
SEQ = 8192
HEAD_DIM = 64
D_RWKV = 1024
D_ATT = 1024
LORA = 64
N_Q_HEADS = 16
N_KV_HEADS = 2
D_KV = N_KV_HEADS * HEAD_DIM
WINDOW = 128
BLOCK = 128
N_BUCKETS = 32
MAX_EXACT = N_BUCKETS // 2
MAX_DISTANCE = 128
NORM_EPS = 1e-6
LNX_EPS = 64e-5
RWKV_COLS = 4 * D_RWKV + 2 * LORA
ATT_COLS = 2 * D_ATT + 2 * D_KV

LANES = 128
N_PAIRS = D_RWKV // LANES
CHUNK = 64
RWKV_CHUNKS_PER_STEP = 4
NEG_BIG = -1e30

VMEM_LIMIT = 48 * 1024 * 1024


def _dot(a, b):
    return jnp.dot(a, b, preferred_element_type=F32)


def _dot_nt(a, b):
    return lax.dot_general(a, b, (((1,), (1,)), ((), ())), preferred_element_type=F32)


def _split3(x):
    hi = x.astype(BF16)
    r1 = x - hi.astype(F32)
    mid = r1.astype(BF16)
    lo = (r1 - mid.astype(F32)).astype(BF16)
    return hi, mid, lo


def _dot_exact_rhs(x, m):
    hi, mid, lo = _split3(x)
    return _dot(hi, m) + _dot(mid, m) + _dot(lo, m)


def _dot_exact_lhs(m, x):
    hi, mid, lo = _split3(x)
    return _dot(m, hi) + _dot(m, mid) + _dot(m, lo)


def _sigmoid(x):
    return 1.0 / (1.0 + jnp.exp(-x))


def _proj_in_kernel(x_ref, nw_ref, w_ref, o_ref):
    x = x_ref[...]
    ms = jnp.mean(x * x, axis=-1, keepdims=True)
    h = x * lax.rsqrt(ms + NORM_EPS) * nw_ref[...]
    o_ref[...] = _dot(h.astype(BF16), w_ref[...])


def _proj_in(x2, norm_w, w_bf16, tm, tn, name):
    s, d = x2.shape
    n = w_bf16.shape[1]
    return pl.pallas_call(
        _proj_in_kernel,
        grid=(n // tn, s // tm),
        in_specs=[
            pl.BlockSpec((tm, d), lambda j, i: (i, 0)),
            pl.BlockSpec((1, d), lambda j, i: (0, 0)),
            pl.BlockSpec((d, tn), lambda j, i: (0, j)),
        ],
        out_specs=pl.BlockSpec((tm, tn), lambda j, i: (i, j)),
        out_shape=jax.ShapeDtypeStruct((s, n), F32),
        compiler_params=pltpu.CompilerParams(
            dimension_semantics=("arbitrary", "arbitrary"), vmem_limit_bytes=VMEM_LIMIT),
        name=name,
    )(x2, norm_w, w_bf16)


def _rwkv_kernel(z_ref, mu_ref, w0_ref, w2_ref, a0_ref, a2_ref, kk_ref, ka_ref, rk_ref,
                 lw_ref, lb_ref, o_ref, gt_ref, prev_ref):
    c = CHUNK
    nch = RWKV_CHUNKS_PER_STEP
    t = c * nch
    i = pl.program_id(0)

    @pl.when(i == 0)
    def _():
        gt_ref[...] = jnp.zeros_like(gt_ref)
        prev_ref[...] = jnp.zeros_like(prev_ref)

    z = z_ref[...]
    row = lax.broadcasted_iota(jnp.int32, z.shape, 0)
    zprev = jnp.where(row == 0, prev_ref[...], pltpu.roll(z, 1, axis=0))
    prev_ref[...] = z[t - 1:t, :]
    zs = z + (zprev - z) * mu_ref[...]

    r_all = zs[:, 0:D_RWKV]
    k_all = zs[:, D_RWKV:2 * D_RWKV]
    v_all = zs[:, 2 * D_RWKV:3 * D_RWKV]
    g_all = zs[:, 3 * D_RWKV:4 * D_RWKV]
    lora_in = zs[:, 4 * D_RWKV:]

    wlin = w0_ref[...] + _dot(jnp.tanh(lora_in).astype(BF16), w2_ref[...])
    neg = -wlin
    softplus = jnp.maximum(neg, 0.0) + jnp.log(1.0 + jnp.exp(-jnp.abs(neg)))
    logw_all = -jnp.exp(-softplus - 0.5)
    av_all = _sigmoid(a0_ref[...] + _dot(lora_in.astype(BF16), a2_ref[...]))

    ti = lax.broadcasted_iota(jnp.int32, (t, t), 0)
    si = lax.broadcasted_iota(jnp.int32, (t, t), 1)
    tril_blk = jnp.where((si <= ti) & ((si // c) == (ti // c)), 1.0, 0.0).astype(BF16)
    logp_all = _dot_exact_lhs(tril_blk, logw_all)

    e_p_all = jnp.exp(logp_all)
    e_px_all = jnp.exp(logp_all - logw_all)
    e_n_all = jnp.exp(-logp_all)
    kmod_all = k_all * (1.0 + (av_all - 1.0) * ka_ref[...])
    kk_all = k_all * kk_ref[...]
    rt_all = r_all * e_p_all
    kt_all = kmod_all * e_n_all

    lane = lax.broadcasted_iota(jnp.int32, (c, LANES), 1)
    head0 = lane < HEAD_DIM
    trow = lax.broadcasted_iota(jnp.int32, (c, LANES), 0)
    eye_cat = jnp.where((lane % HEAD_DIM) == trow, 1.0, 0.0)
    t2 = lax.broadcasted_iota(jnp.int32, (c, 2 * LANES), 0)
    s2 = lax.broadcasted_iota(jnp.int32, (c, 2 * LANES), 1) % HEAD_DIM
    strict2 = s2 < t2
    incl2 = s2 <= t2
    r128 = lax.broadcasted_iota(jnp.int32, (LANES, LANES), 0)
    c128 = lax.broadcasted_iota(jnp.int32, (LANES, LANES), 1)
    same_head = (r128 // HEAD_DIM) == (c128 // HEAD_DIM)
    same_head2 = jnp.concatenate([same_head, same_head], axis=1)
    seg_ones = jnp.where(same_head, 1.0, 0.0).astype(BF16)
    zeros_c = jnp.zeros((c, LANES), F32)

    def stack2(y):
        return jnp.concatenate([jnp.where(head0, y, 0.0), jnp.where(head0, 0.0, y)], axis=0)

    def catmul(a_cat, y):
        return _dot(a_cat.astype(BF16), stack2(y).astype(BF16))

    pairs = range(N_PAIRS)
    psl = [slice(p * LANES, (p + 1) * LANES) for p in pairs]
    items = [(ci, p) for ci in range(nch) for p in pairs]
    rows = lambda ci: slice(ci * c, (ci + 1) * c)

    ss = [_dot_exact_rhs(kk_all[:, s] * kk_all[:, s], seg_ones) for s in psl]
    kkn = [kk_all[:, s] * lax.rsqrt(jnp.maximum(q, 1e-24)) for s, q in zip(psl, ss)]
    at_p = [-n * e_px_all[:, s] for s, n in zip(psl, kkn)]
    bt_p = [n * av_all[:, s] * e_n_all[:, s] for s, n in zip(psl, kkn)]

    a_t = [at_p[p][rows(ci)] for ci, p in items]
    b_t = [bt_p[p][rows(ci)] for ci, p in items]
    r_t = [rt_all[rows(ci), psl[p]] for ci, p in items]
    k_t = [kt_all[rows(ci), psl[p]] for ci, p in items]
    v_i = [v_all[rows(ci), psl[p]] for ci, p in items]
    e_last = [e_p_all[ci * c + c - 1:ci * c + c, psl[p]] for ci, p in items]

    sc = [_dot_nt(jnp.concatenate([a, r], axis=0).astype(BF16),
                  jnp.concatenate([stack2(b), stack2(k)], axis=0).astype(BF16))
          for a, r, b, k in zip(a_t, r_t, b_t, k_t)]
    l_all = [jnp.where(strict2, s[0:c, :], 0.0) for s in sc]
    a_rbk = [jnp.where(incl2, s[c:, :], 0.0) for s in sc]

    x = [l[:, 0:LANES] for l in l_all]
    tinv = [eye_cat + xi for xi in x]
    for _ in range(5):
        x = [catmul(xi, xi) for xi in x]
        tinv = [ti_ + catmul(ti_, xi) for ti_, xi in zip(tinv, x)]

    a_eff = [catmul(ti_, a) for ti_, a in zip(tinv, a_t)]
    lakv = [catmul(l[:, LANES:], v) for l, v in zip(l_all, v_i)]
    w_loc = [catmul(ti_, y) for ti_, y in zip(tinv, lakv)]
    q_eff = [r + catmul(a[:, 0:LANES], ae) for r, a, ae in zip(r_t, a_rbk, a_eff)]
    o_loc = [_dot(a.astype(BF16), jnp.concatenate([stack2(w), stack2(v)], axis=0).astype(BF16))
             for a, w, v in zip(a_rbk, w_loc, v_i)]

    mn = []
    for ae, w, v, b, k, el in zip(a_eff, w_loc, v_i, b_t, k_t, e_last):
        xt = jnp.concatenate([ae, w, v, zeros_c], axis=0).T
        bh = b * el
        kh = k * el
        ymat = jnp.concatenate([
            jnp.concatenate([bh, zeros_c], axis=1),
            jnp.concatenate([zeros_c, bh], axis=1),
            jnp.concatenate([zeros_c, kh], axis=1),
            jnp.concatenate([zeros_c, zeros_c], axis=1)], axis=0)
        mn.append(jnp.where(same_head2, _dot(xt.astype(BF16), ymat.astype(BF16)), 0.0))

    gts = [gt_ref[p] for p in pairs]
    o_i = []
    for idx, (ci, p) in enumerate(items):
        gt = gts[p]
        gtb = gt.astype(BF16)
        o_i.append(_dot_nt(q_eff[idx].astype(BF16), gtb) + o_loc[idx])
        gts[p] = gt * e_last[idx] + _dot(gtb, mn[idx][:, 0:LANES].astype(BF16)) + mn[idx][:, LANES:]
    for p in pairs:
        gt_ref[p] = gts[p]

    for p in pairs:
        s = psl[p]
        o = jnp.concatenate([o_i[ci * N_PAIRS + p] for ci in range(nch)], axis=0)
        mean = _dot_exact_rhs(o, seg_ones) * (1.0 / HEAD_DIM)
        d = o - mean
        var = _dot_exact_rhs(d * d, seg_ones) * (1.0 / HEAD_DIM)
        on = d * lax.rsqrt(var + LNX_EPS) * lw_ref[:, s] + lb_ref[:, s]
        bonus = _dot_exact_rhs(r_all[:, s] * kmod_all[:, s] * rk_ref[:, s], seg_ones) * v_all[:, s]
        g = g_all[:, s]
        o_ref[:, s] = ((on + bonus) * (g * _sigmoid(g))).astype(o_ref.dtype)


def _rwkv(z_rwkv, mu, w0, w2p, a0, a2p, k_k, k_a, r_k, lnx_w, lnx_b):
    s = z_rwkv.shape[0]
    c = CHUNK * RWKV_CHUNKS_PER_STEP
    row = lambda n: pl.BlockSpec((1, n), lambda i: (0, 0))
    return pl.pallas_call(
        _rwkv_kernel,
        grid=(s // c,),
        in_specs=[
            pl.BlockSpec((c, RWKV_COLS), lambda i: (i, 0)),
            row(RWKV_COLS), row(D_RWKV),
            pl.BlockSpec((2 * LORA, D_RWKV), lambda i: (0, 0)),
            row(D_RWKV),
            pl.BlockSpec((2 * LORA, D_RWKV), lambda i: (0, 0)),
            row(D_RWKV), row(D_RWKV), row(D_RWKV), row(D_RWKV), row(D_RWKV),
        ],
        out_specs=pl.BlockSpec((c, D_RWKV), lambda i: (i, 0)),
        out_shape=jax.ShapeDtypeStruct((s, D_RWKV), BF16),
        scratch_shapes=[
            pltpu.VMEM((N_PAIRS, LANES, LANES), F32),
            pltpu.VMEM((1, RWKV_COLS), F32),
        ],
        compiler_params=pltpu.CompilerParams(
            dimension_semantics=("arbitrary",), vmem_limit_bytes=VMEM_LIMIT),
        name="rwkv7",
    )(z_rwkv, mu, w0, w2p, a0, a2p, k_k, k_a, r_k, lnx_w, lnx_b)


def _swa_kernel(sinks_ref, relb_ref, bucket_ref, q_ref, g_ref, k_ref, v_ref, qnw_ref, knw_ref,
                o_ref, bias_ref, kprev_ref, vprev_ref):
    i = pl.program_id(0)
    bq = BLOCK

    @pl.when(i == 0)
    def _():
        kprev_ref[...] = jnp.zeros_like(kprev_ref)
        vprev_ref[...] = jnp.zeros_like(vprev_ref)
        bucket = bucket_ref[...]
        qi = lax.broadcasted_iota(jnp.int32, (bq, 2 * bq), 0)
        kj = lax.broadcasted_iota(jnp.int32, (bq, 2 * bq), 1)
        dist = bq + qi - kj
        inwin = (dist >= 0) & (dist < WINDOW)
        for h in range(N_Q_HEADS):
            acc = jnp.zeros((bq, 2 * bq), F32)
            for b in range(N_BUCKETS):
                acc = jnp.where(bucket == b, relb_ref[b, h], acc)
            acc = jnp.where(inwin, acc, NEG_BIG)
            bias_ref[0, h] = acc
            bias_ref[1, h] = jnp.where(kj >= bq, acc, NEG_BIG)

    first = jnp.where(i == 0, 1, 0)
    lane = lax.broadcasted_iota(jnp.int32, (bq, LANES), 1)
    head0 = lane < HEAD_DIM
    lane2 = lax.broadcasted_iota(jnp.int32, (2 * bq, LANES), 1)
    head0_2 = lane2 < HEAD_DIM
    r128 = lax.broadcasted_iota(jnp.int32, (LANES, LANES), 0)
    c128 = lax.broadcasted_iota(jnp.int32, (LANES, LANES), 1)
    seg_ones = jnp.where((r128 // HEAD_DIM) == (c128 // HEAD_DIM), 1.0, 0.0).astype(BF16)

    def rms_heads(t, w):
        ss = _dot_exact_rhs(t * t, seg_ones) * (1.0 / HEAD_DIM)
        return t * lax.rsqrt(ss + NORM_EPS) * w

    kn = rms_heads(k_ref[...], knw_ref[...])
    vc = v_ref[...]
    kcat = jnp.concatenate([kprev_ref[...], kn], axis=0)
    vcat = jnp.concatenate([vprev_ref[...], vc], axis=0)
    kprev_ref[...] = kn
    vprev_ref[...] = vc
    krol = pltpu.roll(kcat, HEAD_DIM, axis=1)
    vrol = pltpu.roll(vcat, HEAD_DIM, axis=1)
    kdup = [jnp.where(head0_2, kcat, krol).astype(BF16), jnp.where(head0_2, krol, kcat).astype(BF16)]
    vdup = [jnp.where(head0_2, vcat, vrol).astype(BF16), jnp.where(head0_2, vrol, vcat).astype(BF16)]

    scale = HEAD_DIM ** -0.5
    for p in range(N_PAIRS):
        sl = slice(p * LANES, (p + 1) * LANES)
        kvh = (2 * p) // (N_Q_HEADS // N_KV_HEADS)
        qn = rms_heads(q_ref[:, sl], qnw_ref[:, sl]) * scale
        q2 = jnp.concatenate([jnp.where(head0, qn, 0.0), jnp.where(head0, 0.0, qn)], axis=0).astype(BF16)
        lg2 = _dot_nt(q2, kdup[kvh])
        outs = []
        for j in range(2):
            h = 2 * p + j
            lg = lg2[j * bq:(j + 1) * bq, :] + bias_ref[first, h]
            sink = sinks_ref[h]
            m = jnp.maximum(jnp.max(lg, axis=-1, keepdims=True), sink)
            e = jnp.exp(lg - m)
            denom = jnp.sum(e, axis=-1, keepdims=True) + jnp.exp(sink - m)
            pv = _dot(e.astype(BF16), vdup[kvh])
            outs.append(pv / denom)
        o = jnp.where(head0, outs[0], outs[1])
        g = g_ref[:, sl]
        o_ref[:, sl] = (o * (g * _sigmoid(g))).astype(o_ref.dtype)


def _swa(z_att, bucket, sinks, rel_bias, qnw, knw):
    s = z_att.shape[0]
    bq = BLOCK
    smem = pl.BlockSpec(memory_space=pltpu.SMEM)
    return pl.pallas_call(
        _swa_kernel,
        grid=(s // bq,),
        in_specs=[
            smem, smem,
            pl.BlockSpec((bq, 2 * bq), lambda i: (0, 0)),
            pl.BlockSpec((bq, D_ATT), lambda i: (i, 0)),
            pl.BlockSpec((bq, D_ATT), lambda i: (i, 1)),
            pl.BlockSpec((bq, D_KV), lambda i: (i, 2 * D_ATT // D_KV)),
            pl.BlockSpec((bq, D_KV), lambda i: (i, 2 * D_ATT // D_KV + 1)),
            pl.BlockSpec((1, D_ATT), lambda i: (0, 0)),
            pl.BlockSpec((1, D_KV), lambda i: (0, 0)),
        ],
        out_specs=pl.BlockSpec((bq, D_ATT), lambda i: (i, 0)),
        out_shape=jax.ShapeDtypeStruct((s, D_ATT), BF16),
        scratch_shapes=[
            pltpu.VMEM((2, N_Q_HEADS, bq, 2 * bq), F32),
            pltpu.VMEM((bq, D_KV), F32),
            pltpu.VMEM((bq, D_KV), F32),
        ],
        compiler_params=pltpu.CompilerParams(
            dimension_semantics=("arbitrary",), vmem_limit_bytes=VMEM_LIMIT),
        name="swa_sink",
    )(sinks, rel_bias, bucket, z_att, z_att, z_att, z_att, qnw, knw)


def _proj_out_kernel(x_ref, yr_ref, ya_ref, wr_ref, wa_ref, o_ref):
    o_ref[...] = x_ref[...] + _dot(yr_ref[...], wr_ref[...]) + _dot(ya_ref[...], wa_ref[...])


def _proj_out(x2, y_rwkv, y_att, w_r, w_a, tm):
    s, d = x2.shape
    return pl.pallas_call(
        _proj_out_kernel,
        grid=(s // tm,),
        in_specs=[
            pl.BlockSpec((tm, d), lambda i: (i, 0)),
            pl.BlockSpec((tm, D_RWKV), lambda i: (i, 0)),
            pl.BlockSpec((tm, D_ATT), lambda i: (i, 0)),
            pl.BlockSpec((D_RWKV, d), lambda i: (0, 0)),
            pl.BlockSpec((D_ATT, d), lambda i: (0, 0)),
        ],
        out_specs=pl.BlockSpec((tm, d), lambda i: (i, 0)),
        out_shape=jax.ShapeDtypeStruct((s, d), F32),
        compiler_params=pltpu.CompilerParams(
            dimension_semantics=("arbitrary",), vmem_limit_bytes=VMEM_LIMIT),
        name="proj_out",
    )(x2, y_rwkv, y_att, w_r, w_a)


def _t5_bucket(dist):
    n = jnp.maximum(dist, 0)
    nf = jnp.maximum(n, 1).astype(F32)
    large = MAX_EXACT + (jnp.log(nf / MAX_EXACT) / math.log(MAX_DISTANCE / MAX_EXACT)
                         * (N_BUCKETS - MAX_EXACT)).astype(jnp.int32)
    large = jnp.minimum(large, N_BUCKETS - 1)
    return jnp.where(n < MAX_EXACT, n, large)


def kernel(x, norm_w, w_in, w_out, mu_rwkv, w0, w2, a0, a2, k_k, k_a, r_k, lnx_w, lnx_b,
           q_norm_w, k_norm_w, sinks, rel_bias):
    b, s, d = x.shape
    assert (b, s, d) == (1, SEQ, D_MODEL) and norm_w.shape[0] == 1
    x2 = x.reshape(s, d)
    l = 0
    row = lambda t: t.reshape(1, -1).astype(F32)

    w_in_l = w_in[l]
    w_rwkv = w_in_l[:, :RWKV_COLS].astype(BF16)
    wa = w_in_l[:, RWKV_COLS:]
    w_att = jnp.concatenate([wa[:, :D_ATT], wa[:, D_ATT + 2 * D_KV:], wa[:, D_ATT:D_ATT + 2 * D_KV]],
                            axis=1).astype(BF16)
    zeros_l = jnp.zeros((LORA, D_RWKV), F32)
    w2p = jnp.concatenate([w2[l], zeros_l], axis=0).astype(BF16)
    a2p = jnp.concatenate([zeros_l, a2[l]], axis=0).astype(BF16)
    w_out_r = w_out[l][:D_RWKV].astype(BF16)
    w_out_a = w_out[l][D_RWKV:].astype(BF16)
    qnw = jnp.tile(q_norm_w[l], N_Q_HEADS).reshape(1, D_ATT)
    knw = jnp.tile(k_norm_w[l], N_KV_HEADS).reshape(1, D_KV)
    qi = jnp.arange(BLOCK)[:, None]
    kj = jnp.arange(2 * BLOCK)[None, :]
    bucket = _t5_bucket(BLOCK + qi - kj).astype(jnp.int32)

    nw = row(norm_w[l])
    z_rwkv = _proj_in(x2, nw, w_rwkv, 512, 1408, "proj_in_rwkv")
    z_att = _proj_in(x2, nw, w_att, 512, 1152, "proj_in_att")

    y_rwkv = _rwkv(z_rwkv, row(mu_rwkv[l]), row(w0[l]), w2p, row(a0[l]), a2p, row(k_k[l]), row(k_a[l]),
                   row(r_k[l]), row(lnx_w[l]), row(lnx_b[l]))
    y_att = _swa(z_att, bucket, sinks[l].astype(F32), rel_bias.astype(F32), qnw, knw)

    out = _proj_out(x2, y_rwkv, y_att, w_out_r, w_out_a, 512)
    return out.reshape(b, s, d)
```

```python
import functools
import math

import jax
import jax.numpy as jnp
from jax import lax
from jax.experimental import pallas as pl
from jax.experimental.pallas import tpu as pltpu

F32 = jnp.float32
BF16 = jnp.bfloat16

D_MODEL = 2048
SEQ = 8192
HEAD_DIM = 64
D_RWKV = 1024
D_ATT = 1024
LORA = 64
N_Q_HEADS = 16
N_KV_HEADS = 2
D_KV = N_KV_HEADS * HEAD_DIM
WINDOW = 128
BLOCK = 128
N_BUCKETS = 32
MAX_EXACT = N_BUCKETS // 2
MAX_DISTANCE = 128
NORM_EPS = 1e-6
LNX_EPS = 64e-5
RWKV_COLS = 4 * D_RWKV + 2 * LORA
ATT_COLS = 2 * D_ATT + 2 * D_KV

LANES = 128
MXU_WIDTH = 256
N_PAIRS = D_RWKV // LANES
CHUNK = 64
RWKV_CHUNKS_PER_STEP = 4
NEG_BIG = -1e30

VMEM_LIMIT = 48 * 1024 * 1024


def _dot(a, b):
    return jnp.dot(a, b, preferred_element_type=F32)


def _dot_nt(a, b):
    return lax.dot_general(a, b, (((1,), (1,)), ((), ())), preferred_element_type=F32)


def _split3(x):
    hi = x.astype(BF16)
    r1 = x - hi.astype(F32)
    mid = r1.astype(BF16)
    lo = (r1 - mid.astype(F32)).astype(BF16)
    return hi, mid, lo


def _seg_sum(x, seg_ones):
    hi = x.astype(BF16)
    lo = (x - hi.astype(F32)).astype(BF16)
    return _dot(hi, seg_ones) + _dot(lo, seg_ones)


def _dot_exact_lhs(m, x):
    hi, mid, lo = _split3(x)
    return _dot(m, hi) + _dot(m, mid) + _dot(m, lo)


def _sigmoid(x):
    return 1.0 / (1.0 + jnp.exp(-x))


def _proj_in_kernel(x_ref, nw_ref, w_ref, *rest, token_shift):
    if token_shift:
        mu_ref, o_ref, prev_ref = rest

        @pl.when(pl.program_id(1) == 0)
        def _():
            prev_ref[...] = jnp.zeros_like(prev_ref)
    else:
        (o_ref,) = rest
    tm, tn = o_ref.shape
    x = x_ref[...]
    h = (x * nw_ref[...]).astype(BF16)
    rs = lax.rsqrt(jnp.mean(x * x, axis=-1, keepdims=True) + NORM_EPS)
    for c0 in range(0, tn, MXU_WIDTH):
        cs = slice(c0, min(c0 + MXU_WIDTH, tn))
        z = _dot(h, w_ref[:, cs]) * rs
        if token_shift:
            row = lax.broadcasted_iota(jnp.int32, z.shape, 0)
            zprev = jnp.where(row == 0, prev_ref[:, cs], pltpu.roll(z, 1, axis=0))
            prev_ref[:, cs] = z[tm - 1:tm, :]
            z = z + (zprev - z) * mu_ref[:, cs]
        o_ref[:, cs] = z


def _proj_in(x2, norm_w, w_bf16, tm, tn, name, mu=None):
    s, d = x2.shape
    n = w_bf16.shape[1]
    shift = mu is not None
    in_specs = [
        pl.BlockSpec((tm, d), lambda j, i: (i, 0)),
        pl.BlockSpec((1, d), lambda j, i: (0, 0)),
        pl.BlockSpec((d, tn), lambda j, i: (0, j)),
    ]
    args = [x2, norm_w, w_bf16]
    scratch = []
    if shift:
        in_specs.append(pl.BlockSpec((1, tn), lambda j, i: (0, j)))
        args.append(mu)
        scratch.append(pltpu.VMEM((1, tn), F32))
    return pl.pallas_call(
        functools.partial(_proj_in_kernel, token_shift=shift),
        grid=(n // tn, s // tm),
        in_specs=in_specs,
        out_specs=pl.BlockSpec((tm, tn), lambda j, i: (i, j)),
        out_shape=jax.ShapeDtypeStruct((s, n), F32),
        scratch_shapes=scratch,
        compiler_params=pltpu.CompilerParams(
            dimension_semantics=("arbitrary", "arbitrary"), vmem_limit_bytes=VMEM_LIMIT),
        name=name,
    )(*args)


def _rwkv_kernel(z_ref, w0_ref, w2_ref, a0_ref, a2_ref, kk_ref, ka_ref, rk_ref,
                 lw_ref, lb_ref, o_ref, gt_ref):
    c = CHUNK
    nch = RWKV_CHUNKS_PER_STEP
    t = c * nch
    i = pl.program_id(0)

    @pl.when(i == 0)
    def _():
        gt_ref[...] = jnp.zeros_like(gt_ref)

    r_all = z_ref[:, 0:D_RWKV]
    k_all = z_ref[:, D_RWKV:2 * D_RWKV]
    v_all = z_ref[:, 2 * D_RWKV:3 * D_RWKV]
    g_all = z_ref[:, 3 * D_RWKV:4 * D_RWKV]
    lora_in = z_ref[:, 4 * D_RWKV:]

    wlin = w0_ref[...] + _dot(jnp.tanh(lora_in).astype(BF16), w2_ref[...])
    neg = -wlin
    softplus = jnp.maximum(neg, 0.0) + jnp.log(1.0 + jnp.exp(-jnp.abs(neg)))
    logw_all = -jnp.exp(-softplus - 0.5)
    av_all = _sigmoid(a0_ref[...] + _dot(lora_in.astype(BF16), a2_ref[...]))

    ti = lax.broadcasted_iota(jnp.int32, (t, t), 0)
    si = lax.broadcasted_iota(jnp.int32, (t, t), 1)
    tril_blk = jnp.where((si <= ti) & ((si // c) == (ti // c)), 1.0, 0.0).astype(BF16)
    logp_all = _dot_exact_lhs(tril_blk, logw_all)

    e_p_all = jnp.exp(logp_all)
    e_px_all = jnp.exp(logp_all - logw_all)
    e_n_all = jnp.exp(-logp_all)
    kmod_all = k_all * (1.0 + (av_all - 1.0) * ka_ref[...])
    kk_all = k_all * kk_ref[...]
    rt_all = r_all * e_p_all
    kt_all = kmod_all * e_n_all

    lane = lax.broadcasted_iota(jnp.int32, (c, LANES), 1)
    head0 = lane < HEAD_DIM
    trow = lax.broadcasted_iota(jnp.int32, (c, LANES), 0)
    eye_cat = jnp.where((lane % HEAD_DIM) == trow, 1.0, 0.0)
    t2 = lax.broadcasted_iota(jnp.int32, (c, 2 * LANES), 0)
    s2 = lax.broadcasted_iota(jnp.int32, (c, 2 * LANES), 1) % HEAD_DIM
    strict2 = s2 < t2
    incl2 = s2 <= t2
    r128 = lax.broadcasted_iota(jnp.int32, (LANES, LANES), 0)
    c128 = lax.broadcasted_iota(jnp.int32, (LANES, LANES), 1)
    same_head = (r128 // HEAD_DIM) == (c128 // HEAD_DIM)
    same_head2 = jnp.concatenate([same_head, same_head], axis=1)
    seg_ones = jnp.where(same_head, 1.0, 0.0).astype(BF16)
    zeros_c = jnp.zeros((c, LANES), F32)

    def stack2(y):
        return jnp.concatenate([jnp.where(head0, y, 0.0), jnp.where(head0, 0.0, y)], axis=0)

    def catmul(a_cat, y):
        return _dot(a_cat.astype(BF16), stack2(y).astype(BF16))

    pairs = range(N_PAIRS)
    psl = [slice(p * LANES, (p + 1) * LANES) for p in pairs]
    items = [(ci, p) for ci in range(nch) for p in pairs]
    rows = lambda ci: slice(ci * c, (ci + 1) * c)

    ss = [_seg_sum(kk_all[:, s] * kk_all[:, s], seg_ones) for s in psl]
    kkn = [kk_all[:, s] * lax.rsqrt(jnp.maximum(q, 1e-24)) for s, q in zip(psl, ss)]
    at_p = [-n * e_px_all[:, s] for s, n in zip(psl, kkn)]
    bt_p = [n * av_all[:, s] * e_n_all[:, s] for s, n in zip(psl, kkn)]

    a_t = [at_p[p][rows(ci)] for ci, p in items]
    b_t = [bt_p[p][rows(ci)] for ci, p in items]
    r_t = [rt_all[rows(ci), psl[p]] for ci, p in items]
    k_t = [kt_all[rows(ci), psl[p]] for ci, p in items]
    v_i = [v_all[rows(ci), psl[p]] for ci, p in items]
    e_last = [e_p_all[ci * c + c - 1:ci * c + c, psl[p]] for ci, p in items]

    sc = [_dot_nt(jnp.concatenate([a, r], axis=0).astype(BF16),
                  jnp.concatenate([stack2(b), stack2(k)], axis=0).astype(BF16))
          for a, r, b, k in zip(a_t, r_t, b_t, k_t)]
    l_all = [jnp.where(strict2, s[0:c, :], 0.0) for s in sc]
    a_rbk = [jnp.where(incl2, s[c:, :], 0.0) for s in sc]

    x = [l[:, 0:LANES] for l in l_all]
    tinv = [eye_cat + xi for xi in x]
    x = [catmul(xi, xi) for xi in x]
    for _ in range(4):
        tx = [catmul(jnp.concatenate([ti_, xi], axis=0), xi) for ti_, xi in zip(tinv, x)]
        tinv = [ti_ + y[0:c] for ti_, y in zip(tinv, tx)]
        x = [y[c:] for y in tx]
    tinv = [ti_ + catmul(ti_, xi) for ti_, xi in zip(tinv, x)]

    a_eff = [catmul(ti_, a) for ti_, a in zip(tinv, a_t)]
    lakv = [catmul(l[:, LANES:], v) for l, v in zip(l_all, v_i)]
    w_loc = [catmul(ti_, y) for ti_, y in zip(tinv, lakv)]
    q_eff = [r + catmul(a[:, 0:LANES], ae) for r, a, ae in zip(r_t, a_rbk, a_eff)]
    o_loc = [_dot(a.astype(BF16), jnp.concatenate([stack2(w), stack2(v)], axis=0).astype(BF16))
             for a, w, v in zip(a_rbk, w_loc, v_i)]

    mn = []
    for ae, w, v, b, k, el in zip(a_eff, w_loc, v_i, b_t, k_t, e_last):
        xt = jnp.concatenate([ae, w, v, zeros_c], axis=0).T
        bh = b * el
        kh = k * el
        ymat = jnp.concatenate([
            jnp.concatenate([bh, zeros_c], axis=1),
            jnp.concatenate([zeros_c, bh], axis=1),
            jnp.concatenate([zeros_c, kh], axis=1),
            jnp.concatenate([zeros_c, zeros_c], axis=1)], axis=0)
        mn.append(jnp.where(same_head2, _dot(xt.astype(BF16), ymat.astype(BF16)), 0.0))

    gts = [gt_ref[p] for p in pairs]
    o_i = []
    for idx, (ci, p) in enumerate(items):
        gt = gts[p]
        gtb = gt.astype(BF16)
        o_i.append(_dot_nt(q_eff[idx].astype(BF16), gtb) + o_loc[idx])
        gts[p] = gt * e_last[idx] + _dot(gtb, mn[idx][:, 0:LANES].astype(BF16)) + mn[idx][:, LANES:]
    for p in pairs:
        gt_ref[p] = gts[p]

    for p in pairs:
        s = psl[p]
        o = jnp.concatenate([o_i[ci * N_PAIRS + p] for ci in range(nch)], axis=0)
        mean = _seg_sum(o, seg_ones) * (1.0 / HEAD_DIM)
        d = o - mean
        var = _seg_sum(d * d, seg_ones) * (1.0 / HEAD_DIM)
        on = d * lax.rsqrt(var + LNX_EPS) * lw_ref[:, s] + lb_ref[:, s]
        bonus = _seg_sum(r_all[:, s] * kmod_all[:, s] * rk_ref[:, s], seg_ones) * v_all[:, s]
        g = g_all[:, s]
        o_ref[:, s] = ((on + bonus) * (g * _sigmoid(g))).astype(o_ref.dtype)


def _rwkv(z_rwkv, w0, w2p, a0, a2p, k_k, k_a, r_k, lnx_w, lnx_b):
    s = z_rwkv.shape[0]
    c = CHUNK * RWKV_CHUNKS_PER_STEP
    row = lambda n: pl.BlockSpec((1, n), lambda i: (0, 0))
    return pl.pallas_call(
        _rwkv_kernel,
        grid=(s // c,),
        in_specs=[
            pl.BlockSpec((c, RWKV_COLS), lambda i: (i, 0)),
            row(D_RWKV),
            pl.BlockSpec((2 * LORA, D_RWKV), lambda i: (0, 0)),
            row(D_RWKV),
            pl.BlockSpec((2 * LORA, D_RWKV), lambda i: (0, 0)),
            row(D_RWKV), row(D_RWKV), row(D_RWKV), row(D_RWKV), row(D_RWKV),
        ],
        out_specs=pl.BlockSpec((c, D_RWKV), lambda i: (i, 0)),
        out_shape=jax.ShapeDtypeStruct((s, D_RWKV), BF16),
        scratch_shapes=[
            pltpu.VMEM((N_PAIRS, LANES, LANES), F32),
        ],
        compiler_params=pltpu.CompilerParams(
            dimension_semantics=("arbitrary",), vmem_limit_bytes=VMEM_LIMIT),
        name="rwkv7",
    )(z_rwkv, w0, w2p, a0, a2p, k_k, k_a, r_k, lnx_w, lnx_b)


def _swa_kernel(sinks_ref, relb_ref, bucket_ref, q_ref, g_ref, k_ref, v_ref, qnw_ref, knw_ref,
                o_ref, bias_ref, kprev_ref, vprev_ref):
    i = pl.program_id(0)
    bq = BLOCK

    @pl.when(i == 0)
    def _():
        kprev_ref[...] = jnp.zeros_like(kprev_ref)
        vprev_ref[...] = jnp.zeros_like(vprev_ref)
        bucket = bucket_ref[...]
        qi = lax.broadcasted_iota(jnp.int32, (bq, 2 * bq), 0)
        kj = lax.broadcasted_iota(jnp.int32, (bq, 2 * bq), 1)
        dist = bq + qi - kj
        inwin = (dist >= 0) & (dist < WINDOW)
        for h in range(N_Q_HEADS):
            acc = jnp.zeros((bq, 2 * bq), F32)
            for b in range(N_BUCKETS):
                acc = jnp.where(bucket == b, relb_ref[b, h], acc)
            acc = jnp.where(inwin, acc, NEG_BIG)
            bias_ref[0, h] = acc
            bias_ref[1, h] = jnp.where(kj >= bq, acc, NEG_BIG)

    first = jnp.where(i == 0, 1, 0)
    lane = lax.broadcasted_iota(jnp.int32, (bq, LANES), 1)
    head0 = lane < HEAD_DIM
    lane2 = lax.broadcasted_iota(jnp.int32, (2 * bq, LANES), 1)
    head0_2 = lane2 < HEAD_DIM
    r128 = lax.broadcasted_iota(jnp.int32, (LANES, LANES), 0)
    c128 = lax.broadcasted_iota(jnp.int32, (LANES, LANES), 1)
    seg_ones = jnp.where((r128 // HEAD_DIM) == (c128 // HEAD_DIM), 1.0, 0.0).astype(BF16)

    def rms_heads(t, w):
        ss = _seg_sum(t * t, seg_ones) * (1.0 / HEAD_DIM)
        return t * lax.rsqrt(ss + NORM_EPS) * w

    kn = rms_heads(k_ref[...], knw_ref[...])
    vc = v_ref[...]
    kcat = jnp.concatenate([kprev_ref[...], kn], axis=0)
    vcat = jnp.concatenate([vprev_ref[...], vc], axis=0)
    kprev_ref[...] = kn
    vprev_ref[...] = vc
    krol = pltpu.roll(kcat, HEAD_DIM, axis=1)
    vrol = pltpu.roll(vcat, HEAD_DIM, axis=1)
    kdup = [jnp.where(head0_2, kcat, krol).astype(BF16), jnp.where(head0_2, krol, kcat).astype(BF16)]
    vdup = [jnp.where(head0_2, vcat, vrol).astype(BF16), jnp.where(head0_2, vrol, vcat).astype(BF16)]

    scale = HEAD_DIM ** -0.5
    pairs = range(N_PAIRS)
    heads = range(N_Q_HEADS)
    psl = [slice(p * LANES, (p + 1) * LANES) for p in pairs]
    kvh = [(2 * p) // (N_Q_HEADS // N_KV_HEADS) for p in pairs]
    qn = [rms_heads(q_ref[:, s], qnw_ref[:, s]) * scale for s in psl]
    q2 = [jnp.concatenate([jnp.where(head0, q, 0.0), jnp.where(head0, 0.0, q)], axis=0).astype(BF16)
          for q in qn]
    lg2 = [_dot_nt(q, kdup[kv]) for q, kv in zip(q2, kvh)]
    lg = [lg2[h // 2][(h % 2) * bq:(h % 2 + 1) * bq, :] + bias_ref[first, h] for h in heads]
    m = [jnp.maximum(jnp.max(l, axis=-1, keepdims=True), sinks_ref[h]) for h, l in zip(heads, lg)]
    e = [jnp.exp(l - mm) for l, mm in zip(lg, m)]
    denom = [jnp.sum(ee, axis=-1, keepdims=True) + jnp.exp(sinks_ref[h] - mm)
             for h, ee, mm in zip(heads, e, m)]
    pv = [_dot(ee.astype(BF16), vdup[kvh[h // 2]]) for h, ee in zip(heads, e)]
    outs = [x / d for x, d in zip(pv, denom)]
    for p in pairs:
        o = jnp.where(head0, outs[2 * p], outs[2 * p + 1])
        g = g_ref[:, psl[p]]
        o_ref[:, psl[p]] = (o * (g * _sigmoid(g))).astype(o_ref.dtype)


def _swa(z_att, bucket, sinks, rel_bias, qnw, knw):
    s = z_att.shape[0]
    bq = BLOCK
    smem = pl.BlockSpec(memory_space=pltpu.SMEM)
    return pl.pallas_call(
        _swa_kernel,
        grid=(s // bq,),
        in_specs=[
            smem, smem,
            pl.BlockSpec((bq, 2 * bq), lambda i: (0, 0)),
            pl.BlockSpec((bq, D_ATT), lambda i: (i, 0)),
            pl.BlockSpec((bq, D_ATT), lambda i: (i, 1)),
            pl.BlockSpec((bq, D_KV), lambda i: (i, 2 * D_ATT // D_KV)),
            pl.BlockSpec((bq, D_KV), lambda i: (i, 2 * D_ATT // D_KV + 1)),
            pl.BlockSpec((1, D_ATT), lambda i: (0, 0)),
            pl.BlockSpec((1, D_KV), lambda i: (0, 0)),
        ],
        out_specs=pl.BlockSpec((bq, D_ATT), lambda i: (i, 0)),
        out_shape=jax.ShapeDtypeStruct((s, D_ATT), BF16),
        scratch_shapes=[
            pltpu.VMEM((2, N_Q_HEADS, bq, 2 * bq), F32),
            pltpu.VMEM((bq, D_KV), F32),
            pltpu.VMEM((bq, D_KV), F32),
        ],
        compiler_params=pltpu.CompilerParams(
            dimension_semantics=("arbitrary",), vmem_limit_bytes=VMEM_LIMIT),
        name="swa_sink",
    )(sinks, rel_bias, bucket, z_att, z_att, z_att, z_att, qnw, knw)


def _proj_out_kernel(x_ref, yr_ref, ya_ref, wr_ref, wa_ref, o_ref):
    o_ref[...] = x_ref[...] + _dot(yr_ref[...], wr_ref[...]) + _dot(ya_ref[...], wa_ref[...])


def _proj_out(x2, y_rwkv, y_att, w_r, w_a, tm):
    s, d = x2.shape
    return pl.pallas_call(
        _proj_out_kernel,
        grid=(s // tm,),
        in_specs=[
            pl.BlockSpec((tm, d), lambda i: (i, 0)),
            pl.BlockSpec((tm, D_RWKV), lambda i: (i, 0)),
            pl.BlockSpec((tm, D_ATT), lambda i: (i, 0)),
            pl.BlockSpec((D_RWKV, d), lambda i: (0, 0)),
            pl.BlockSpec((D_ATT, d), lambda i: (0, 0)),
        ],
        out_specs=pl.BlockSpec((tm, d), lambda i: (i, 0)),
        out_shape=jax.ShapeDtypeStruct((s, d), F32),
        compiler_params=pltpu.CompilerParams(
            dimension_semantics=("arbitrary",), vmem_limit_bytes=VMEM_LIMIT),
        name="proj_out",
    )(x2, y_rwkv, y_att, w_r, w_a)


def _t5_bucket(dist):
    n = jnp.maximum(dist, 0)
    nf = jnp.maximum(n, 1).astype(F32)
    large = MAX_EXACT + (jnp.log(nf / MAX_EXACT) / math.log(MAX_DISTANCE / MAX_EXACT)
                         * (N_BUCKETS - MAX_EXACT)).astype(jnp.int32)
    large = jnp.minimum(large, N_BUCKETS - 1)
    return jnp.where(n < MAX_EXACT, n, large)


def kernel(x, norm_w, w_in, w_out, mu_rwkv, w0, w2, a0, a2, k_k, k_a, r_k, lnx_w, lnx_b,
           q_norm_w, k_norm_w, sinks, rel_bias):
    b, s, d = x.shape
    assert (b, s, d) == (1, SEQ, D_MODEL) and norm_w.shape[0] == 1
    x2 = x.reshape(s, d)
    l = 0
    row = lambda t: t.reshape(1, -1).astype(F32)

    w_in_l = w_in[l]
    w_rwkv = w_in_l[:, :RWKV_COLS].astype(BF16)
    wa = w_in_l[:, RWKV_COLS:]
    w_att = jnp.concatenate([wa[:, :D_ATT], wa[:, D_ATT + 2 * D_KV:], wa[:, D_ATT:D_ATT + 2 * D_KV]],
                            axis=1).astype(BF16)
    zeros_l = jnp.zeros((LORA, D_RWKV), F32)
    w2p = jnp.concatenate([w2[l], zeros_l], axis=0).astype(BF16)
    a2p = jnp.concatenate([zeros_l, a2[l]], axis=0).astype(BF16)
    w_out_r = w_out[l][:D_RWKV].astype(BF16)
    w_out_a = w_out[l][D_RWKV:].astype(BF16)
    qnw = jnp.tile(q_norm_w[l], N_Q_HEADS).reshape(1, D_ATT)
    knw = jnp.tile(k_norm_w[l], N_KV_HEADS).reshape(1, D_KV)
    qi = jnp.arange(BLOCK)[:, None]
    kj = jnp.arange(2 * BLOCK)[None, :]
    bucket = _t5_bucket(BLOCK + qi - kj).astype(jnp.int32)

    nw = row(norm_w[l])
    z_rwkv = _proj_in(x2, nw, w_rwkv, 512, 1408, "proj_in_rwkv", mu=row(mu_rwkv[l]))
    z_att = _proj_in(x2, nw, w_att, 512, 1152, "proj_in_att")

    y_rwkv = _rwkv(z_rwkv, row(w0[l]), w2p, row(a0[l]), a2p, row(k_k[l]), row(k_a[l]),
                   row(r_k[l]), row(lnx_w[l]), row(lnx_b[l]))
    y_att = _swa(z_att, bucket, sinks[l].astype(F32), rel_bias.astype(F32), qnw, knw)

    out = _proj_out(x2, y_rwkv, y_att, w_out_r, w_out_a, 512)
    return out.reshape(b, s, d)
```

```python
import functools
import math

import jax
import jax.numpy as jnp
import numpy as np
from jax import lax
from jax.experimental import pallas as pl
from jax.experimental.pallas import tpu as pltpu

F32 = jnp.float32
BF16 = jnp.bfloat16

D_MODEL = 2048
SEQ = 8192
HEAD_DIM = 64
D_RWKV = 1024
D_ATT = 1024
LORA = 64
N_Q_HEADS = 16
N_KV_HEADS = 2
D_KV = N_KV_HEADS * HEAD_DIM
WINDOW = 128
BLOCK = 128
N_BUCKETS = 32
MAX_EXACT = N_BUCKETS // 2
MAX_DISTANCE = 128
NORM_EPS = 1e-6
LNX_EPS = 64e-5
RWKV_COLS = 4 * D_RWKV + 2 * LORA
ATT_COLS = 2 * D_ATT + 2 * D_KV

LANES = 128
MXU_WIDTH = 256
N_PAIRS = D_RWKV // LANES
CHUNK = 64
RWKV_CHUNKS_PER_STEP = 4
NEG_BIG = -1e30

VMEM_LIMIT = 48 * 1024 * 1024


def _dot(a, b):
    return jnp.dot(a, b, preferred_element_type=F32)


def _dot_nt(a, b):
    return lax.dot_general(a, b, (((1,), (1,)), ((), ())), preferred_element_type=F32)


def _split3(x):
    hi = x.astype(BF16)
    r1 = x - hi.astype(F32)
    mid = r1.astype(BF16)
    lo = (r1 - mid.astype(F32)).astype(BF16)
    return hi, mid, lo


def _seg_sum(x, seg_ones):
    hi = x.astype(BF16)
    lo = (x - hi.astype(F32)).astype(BF16)
    return _dot(hi, seg_ones) + _dot(lo, seg_ones)


def _dot_exact_lhs(m, x):
    hi, mid, lo = _split3(x)
    return _dot(m, hi) + _dot(m, mid) + _dot(m, lo)


def _sigmoid(x):
    return 1.0 / (1.0 + jnp.exp(-x))


def _proj_in_kernel(x_ref, nw_ref, w_ref, *rest, token_shift):
    if token_shift:
        mu_ref, o_ref, wb_ref, prev_ref = rest
    else:
        o_ref, wb_ref = rest

    @pl.when(pl.program_id(1) == 0)
    def _():
        wb_ref[...] = w_ref[...].astype(BF16)
        if token_shift:
            prev_ref[...] = jnp.zeros_like(prev_ref)

    tm, tn = o_ref.shape
    x = x_ref[...]
    h = (x * nw_ref[...]).astype(BF16)
    rs = lax.rsqrt(jnp.mean(x * x, axis=-1, keepdims=True) + NORM_EPS)
    for c0 in range(0, tn, MXU_WIDTH):
        cs = slice(c0, min(c0 + MXU_WIDTH, tn))
        z = _dot(h, wb_ref[:, cs]) * rs
        if token_shift:
            row = lax.broadcasted_iota(jnp.int32, z.shape, 0)
            zprev = jnp.where(row == 0, prev_ref[:, cs], pltpu.roll(z, 1, axis=0))
            prev_ref[:, cs] = z[tm - 1:tm, :]
            z = z + (zprev - z) * mu_ref[:, cs]
        o_ref[:, cs] = z


def _proj_in(x2, norm_w, w, n, tm, tn, name, mu=None):
    s, d = x2.shape
    shift = mu is not None
    in_specs = [
        pl.BlockSpec((tm, d), lambda j, i: (i, 0)),
        pl.BlockSpec((1, d), lambda j, i: (0, 0)),
        pl.BlockSpec((d, tn), lambda j, i: (0, j)),
    ]
    args = [x2, norm_w, w]
    scratch = [pltpu.VMEM((d, tn), BF16)]
    if shift:
        in_specs.append(pl.BlockSpec((1, tn), lambda j, i: (0, j)))
        args.append(mu)
        scratch.append(pltpu.VMEM((1, tn), F32))
    return pl.pallas_call(
        functools.partial(_proj_in_kernel, token_shift=shift),
        grid=(n // tn, s // tm),
        in_specs=in_specs,
        out_specs=pl.BlockSpec((tm, tn), lambda j, i: (i, j)),
        out_shape=jax.ShapeDtypeStruct((s, n), F32),
        scratch_shapes=scratch,
        compiler_params=pltpu.CompilerParams(
            dimension_semantics=("arbitrary", "arbitrary"), vmem_limit_bytes=VMEM_LIMIT),
        name=name,
    )(*args)


def _rwkv_kernel(z_ref, w0_ref, w2_ref, a0_ref, a2_ref, kk_ref, ka_ref, rk_ref,
                 lw_ref, lb_ref, o_ref, gt_ref):
    c = CHUNK
    nch = RWKV_CHUNKS_PER_STEP
    t = c * nch
    i = pl.program_id(0)

    @pl.when(i == 0)
    def _():
        gt_ref[...] = jnp.zeros_like(gt_ref)

    r_all = z_ref[:, 0:D_RWKV]
    k_all = z_ref[:, D_RWKV:2 * D_RWKV]
    v_all = z_ref[:, 2 * D_RWKV:3 * D_RWKV]
    g_all = z_ref[:, 3 * D_RWKV:4 * D_RWKV]
    lora_in = z_ref[:, 4 * D_RWKV:]

    wlin = w0_ref[...] + _dot(jnp.tanh(lora_in).astype(BF16), w2_ref[...])
    neg = -wlin
    softplus = jnp.maximum(neg, 0.0) + jnp.log(1.0 + jnp.exp(-jnp.abs(neg)))
    logw_all = -jnp.exp(-softplus - 0.5)
    av_all = _sigmoid(a0_ref[...] + _dot(lora_in.astype(BF16), a2_ref[...]))

    ti = lax.broadcasted_iota(jnp.int32, (t, t), 0)
    si = lax.broadcasted_iota(jnp.int32, (t, t), 1)
    tril_blk = jnp.where((si <= ti) & ((si // c) == (ti // c)), 1.0, 0.0).astype(BF16)
    logp_all = _dot_exact_lhs(tril_blk, logw_all)

    e_p_all = jnp.exp(logp_all)
    e_px_all = jnp.exp(logp_all - logw_all)
    e_n_all = jnp.exp(-logp_all)
    kmod_all = k_all * (1.0 + (av_all - 1.0) * ka_ref[...])
    kk_all = k_all * kk_ref[...]
    rt_all = r_all * e_p_all
    kt_all = kmod_all * e_n_all

    lane = lax.broadcasted_iota(jnp.int32, (c, LANES), 1)
    head0 = lane < HEAD_DIM
    trow = lax.broadcasted_iota(jnp.int32, (c, LANES), 0)
    eye_cat = jnp.where((lane % HEAD_DIM) == trow, 1.0, 0.0)
    t2 = lax.broadcasted_iota(jnp.int32, (c, 2 * LANES), 0)
    s2 = lax.broadcasted_iota(jnp.int32, (c, 2 * LANES), 1) % HEAD_DIM
    strict2 = s2 < t2
    incl2 = s2 <= t2
    r128 = lax.broadcasted_iota(jnp.int32, (LANES, LANES), 0)
    c128 = lax.broadcasted_iota(jnp.int32, (LANES, LANES), 1)
    same_head = (r128 // HEAD_DIM) == (c128 // HEAD_DIM)
    same_head2 = jnp.concatenate([same_head, same_head], axis=1)
    seg_ones = jnp.where(same_head, 1.0, 0.0).astype(BF16)
    zeros_c = jnp.zeros((c, LANES), F32)

    def stack2(y):
        return jnp.concatenate([jnp.where(head0, y, 0.0), jnp.where(head0, 0.0, y)], axis=0)

    def catmul(a_cat, y):
        return _dot(a_cat.astype(BF16), stack2(y).astype(BF16))

    pairs = range(N_PAIRS)
    psl = [slice(p * LANES, (p + 1) * LANES) for p in pairs]
    items = [(ci, p) for ci in range(nch) for p in pairs]
    rows = lambda ci: slice(ci * c, (ci + 1) * c)

    ss = [_seg_sum(kk_all[:, s] * kk_all[:, s], seg_ones) for s in psl]
    kkn = [kk_all[:, s] * lax.rsqrt(jnp.maximum(q, 1e-24)) for s, q in zip(psl, ss)]
    at_p = [-n * e_px_all[:, s] for s, n in zip(psl, kkn)]
    bt_p = [n * av_all[:, s] * e_n_all[:, s] for s, n in zip(psl, kkn)]

    a_t = [at_p[p][rows(ci)] for ci, p in items]
    b_t = [bt_p[p][rows(ci)] for ci, p in items]
    r_t = [rt_all[rows(ci), psl[p]] for ci, p in items]
    k_t = [kt_all[rows(ci), psl[p]] for ci, p in items]
    v_i = [v_all[rows(ci), psl[p]] for ci, p in items]
    e_last = [e_p_all[ci * c + c - 1:ci * c + c, psl[p]] for ci, p in items]

    sc = [_dot_nt(jnp.concatenate([a, r], axis=0).astype(BF16),
                  jnp.concatenate([stack2(b), stack2(k)], axis=0).astype(BF16))
          for a, r, b, k in zip(a_t, r_t, b_t, k_t)]
    l_all = [jnp.where(strict2, s[0:c, :], 0.0) for s in sc]
    a_rbk = [jnp.where(incl2, s[c:, :], 0.0) for s in sc]

    x = [l[:, 0:LANES] for l in l_all]
    tinv = [eye_cat + xi for xi in x]
    x = [catmul(xi, xi) for xi in x]
    for _ in range(4):
        tx = [catmul(jnp.concatenate([ti_, xi], axis=0), xi) for ti_, xi in zip(tinv, x)]
        tinv = [ti_ + y[0:c] for ti_, y in zip(tinv, tx)]
        x = [y[c:] for y in tx]
    tinv = [ti_ + catmul(ti_, xi) for ti_, xi in zip(tinv, x)]

    a_eff = [catmul(ti_, a) for ti_, a in zip(tinv, a_t)]
    lakv = [catmul(l[:, LANES:], v) for l, v in zip(l_all, v_i)]
    w_loc = [catmul(ti_, y) for ti_, y in zip(tinv, lakv)]
    q_eff = [r + catmul(a[:, 0:LANES], ae) for r, a, ae in zip(r_t, a_rbk, a_eff)]
    o_loc = [_dot(a.astype(BF16), jnp.concatenate([stack2(w), stack2(v)], axis=0).astype(BF16))
             for a, w, v in zip(a_rbk, w_loc, v_i)]

    bh = [b * el for b, el in zip(b_t, e_last)]
    kh = [k * el for k, el in zip(k_t, e_last)]
    m_mat = [jnp.where(same_head, _dot(jnp.concatenate([ae, zeros_c], axis=0).T.astype(BF16),
                                       jnp.concatenate([b, zeros_c], axis=0).astype(BF16)), 0.0)
             for ae, b in zip(a_eff, bh)]
    n_mat = [jnp.where(same_head, _dot(jnp.concatenate([w, v], axis=0).T.astype(BF16),
                                       jnp.concatenate([b, k], axis=0).astype(BF16)), 0.0)
             for w, v, b, k in zip(w_loc, v_i, bh, kh)]

    gts = [gt_ref[p] for p in pairs]
    o_i = []
    for idx, (ci, p) in enumerate(items):
        gt = gts[p]
        gtb = gt.astype(BF16)
        o_i.append(_dot_nt(q_eff[idx].astype(BF16), gtb) + o_loc[idx])
        gts[p] = gt * e_last[idx] + _dot(gtb, m_mat[idx].astype(BF16)) + n_mat[idx]
    for p in pairs:
        gt_ref[p] = gts[p]

    for p in pairs:
        s = psl[p]
        o = jnp.concatenate([o_i[ci * N_PAIRS + p] for ci in range(nch)], axis=0)
        mean = _seg_sum(o, seg_ones) * (1.0 / HEAD_DIM)
        d = o - mean
        var = _seg_sum(d * d, seg_ones) * (1.0 / HEAD_DIM)
        on = d * lax.rsqrt(var + LNX_EPS) * lw_ref[:, s] + lb_ref[:, s]
        bonus = _seg_sum(r_all[:, s] * kmod_all[:, s] * rk_ref[:, s], seg_ones) * v_all[:, s]
        g = g_all[:, s]
        o_ref[:, s] = ((on + bonus) * (g * _sigmoid(g))).astype(o_ref.dtype)


def _rwkv(z_rwkv, w0, w2p, a0, a2p, k_k, k_a, r_k, lnx_w, lnx_b):
    s = z_rwkv.shape[0]
    c = CHUNK * RWKV_CHUNKS_PER_STEP
    row = lambda n: pl.BlockSpec((1, n), lambda i: (0, 0))
    return pl.pallas_call(
        _rwkv_kernel,
        grid=(s // c,),
        in_specs=[
            pl.BlockSpec((c, RWKV_COLS), lambda i: (i, 0)),
            row(D_RWKV),
            pl.BlockSpec((2 * LORA, D_RWKV), lambda i: (0, 0)),
            row(D_RWKV),
            pl.BlockSpec((2 * LORA, D_RWKV), lambda i: (0, 0)),
            row(D_RWKV), row(D_RWKV), row(D_RWKV), row(D_RWKV), row(D_RWKV),
        ],
        out_specs=pl.BlockSpec((c, D_RWKV), lambda i: (i, 0)),
        out_shape=jax.ShapeDtypeStruct((s, D_RWKV), BF16),
        scratch_shapes=[
            pltpu.VMEM((N_PAIRS, LANES, LANES), F32),
        ],
        compiler_params=pltpu.CompilerParams(
            dimension_semantics=("arbitrary",), vmem_limit_bytes=VMEM_LIMIT),
        name="rwkv7",
    )(z_rwkv, w0, w2p, a0, a2p, k_k, k_a, r_k, lnx_w, lnx_b)


def _swa_kernel(sinks_ref, relb_ref, bucket_ref, q_ref, g_ref, k_ref, v_ref, qnw_ref, knw_ref,
                o_ref, bias_ref, kprev_ref, vprev_ref):
    i = pl.program_id(0)
    bq = BLOCK

    @pl.when(i == 0)
    def _():
        kprev_ref[...] = jnp.zeros_like(kprev_ref)
        vprev_ref[...] = jnp.zeros_like(vprev_ref)
        bucket = bucket_ref[...]
        qi = lax.broadcasted_iota(jnp.int32, (bq, 2 * bq), 0)
        kj = lax.broadcasted_iota(jnp.int32, (bq, 2 * bq), 1)
        dist = bq + qi - kj
        inwin = (dist >= 0) & (dist < WINDOW)
        for h in range(N_Q_HEADS):
            acc = jnp.zeros((bq, 2 * bq), F32)
            for b in range(N_BUCKETS):
                acc = jnp.where(bucket == b, relb_ref[b, h], acc)
            acc = jnp.where(inwin, acc, NEG_BIG)
            bias_ref[0, h] = acc
            bias_ref[1, h] = jnp.where(kj >= bq, acc, NEG_BIG)

    first = jnp.where(i == 0, 1, 0)
    lane = lax.broadcasted_iota(jnp.int32, (bq, LANES), 1)
    head0 = lane < HEAD_DIM
    lane2 = lax.broadcasted_iota(jnp.int32, (2 * bq, LANES), 1)
    head0_2 = lane2 < HEAD_DIM
    r128 = lax.broadcasted_iota(jnp.int32, (LANES, LANES), 0)
    c128 = lax.broadcasted_iota(jnp.int32, (LANES, LANES), 1)
    seg_ones = jnp.where((r128 // HEAD_DIM) == (c128 // HEAD_DIM), 1.0, 0.0).astype(BF16)

    def rms_heads(t, w):
        ss = _seg_sum(t * t, seg_ones) * (1.0 / HEAD_DIM)
        return t * lax.rsqrt(ss + NORM_EPS) * w

    kn = rms_heads(k_ref[...], knw_ref[...])
    vc = v_ref[...]
    kcat = jnp.concatenate([kprev_ref[...], kn], axis=0)
    vcat = jnp.concatenate([vprev_ref[...], vc], axis=0)
    kprev_ref[...] = kn
    vprev_ref[...] = vc
    krol = pltpu.roll(kcat, HEAD_DIM, axis=1)
    vrol = pltpu.roll(vcat, HEAD_DIM, axis=1)
    kdup = [jnp.where(head0_2, kcat, krol).astype(BF16), jnp.where(head0_2, krol, kcat).astype(BF16)]
    vdup = [jnp.where(head0_2, vcat, vrol).astype(BF16), jnp.where(head0_2, vrol, vcat).astype(BF16)]

    scale = HEAD_DIM ** -0.5
    pairs = range(N_PAIRS)
    heads = range(N_Q_HEADS)
    psl = [slice(p * LANES, (p + 1) * LANES) for p in pairs]
    kvh = [(2 * p) // (N_Q_HEADS // N_KV_HEADS) for p in pairs]
    qn = [rms_heads(q_ref[:, s], qnw_ref[:, s]) * scale for s in psl]
    q2 = [jnp.concatenate([jnp.where(head0, q, 0.0), jnp.where(head0, 0.0, q)], axis=0).astype(BF16)
          for q in qn]
    lg2 = [_dot_nt(q, kdup[kv]) for q, kv in zip(q2, kvh)]
    lg = [lg2[h // 2][(h % 2) * bq:(h % 2 + 1) * bq, :] + bias_ref[first, h] for h in heads]
    m = [jnp.maximum(jnp.max(l, axis=-1, keepdims=True), sinks_ref[h]) for h, l in zip(heads, lg)]
    e = [jnp.exp(l - mm) for l, mm in zip(lg, m)]
    denom = [jnp.sum(ee, axis=-1, keepdims=True) + jnp.exp(sinks_ref[h] - mm)
             for h, ee, mm in zip(heads, e, m)]
    pv = [_dot(ee.astype(BF16), vdup[kvh[h // 2]]) for h, ee in zip(heads, e)]
    outs = [x / d for x, d in zip(pv, denom)]
    for p in pairs:
        o = jnp.where(head0, outs[2 * p], outs[2 * p + 1])
        g = g_ref[:, psl[p]]
        o_ref[:, psl[p]] = (o * (g * _sigmoid(g))).astype(o_ref.dtype)


def _swa(z_att, bucket, sinks, rel_bias, qnw, knw):
    s = z_att.shape[0]
    bq = BLOCK
    smem = pl.BlockSpec(memory_space=pltpu.SMEM)
    return pl.pallas_call(
        _swa_kernel,
        grid=(s // bq,),
        in_specs=[
            smem, smem,
            pl.BlockSpec((bq, 2 * bq), lambda i: (0, 0)),
            pl.BlockSpec((bq, D_ATT), lambda i: (i, 0)),
            pl.BlockSpec((bq, D_ATT), lambda i: (i, 1)),
            pl.BlockSpec((bq, D_KV), lambda i: (i, 2 * D_ATT // D_KV)),
            pl.BlockSpec((bq, D_KV), lambda i: (i, 2 * D_ATT // D_KV + 1)),
            pl.BlockSpec((1, D_ATT), lambda i: (0, 0)),
            pl.BlockSpec((1, D_KV), lambda i: (0, 0)),
        ],
        out_specs=pl.BlockSpec((bq, D_ATT), lambda i: (i, 0)),
        out_shape=jax.ShapeDtypeStruct((s, D_ATT), BF16),
        scratch_shapes=[
            pltpu.VMEM((2, N_Q_HEADS, bq, 2 * bq), F32),
            pltpu.VMEM((bq, D_KV), F32),
            pltpu.VMEM((bq, D_KV), F32),
        ],
        compiler_params=pltpu.CompilerParams(
            dimension_semantics=("arbitrary",), vmem_limit_bytes=VMEM_LIMIT),
        name="swa_sink",
    )(sinks, rel_bias, bucket, z_att, z_att, z_att, z_att, qnw, knw)


def _proj_out_kernel(x_ref, yr_ref, ya_ref, wr_ref, wa_ref, o_ref):
    o_ref[...] = x_ref[...] + _dot(yr_ref[...], wr_ref[...]) + _dot(ya_ref[...], wa_ref[...])


def _proj_out(x2, y_rwkv, y_att, w_r, w_a, tm):
    s, d = x2.shape
    return pl.pallas_call(
        _proj_out_kernel,
        grid=(s // tm,),
        in_specs=[
            pl.BlockSpec((tm, d), lambda i: (i, 0)),
            pl.BlockSpec((tm, D_RWKV), lambda i: (i, 0)),
            pl.BlockSpec((tm, D_ATT), lambda i: (i, 0)),
            pl.BlockSpec((D_RWKV, d), lambda i: (0, 0)),
            pl.BlockSpec((D_ATT, d), lambda i: (0, 0)),
        ],
        out_specs=pl.BlockSpec((tm, d), lambda i: (i, 0)),
        out_shape=jax.ShapeDtypeStruct((s, d), F32),
        compiler_params=pltpu.CompilerParams(
            dimension_semantics=("arbitrary",), vmem_limit_bytes=VMEM_LIMIT),
        name="proj_out",
    )(x2, y_rwkv, y_att, w_r, w_a)


def _t5_bucket_table():
    dist = BLOCK + np.arange(BLOCK)[:, None] - np.arange(2 * BLOCK)[None, :]
    n = np.maximum(dist, 0)
    nf = np.maximum(n, 1).astype(np.float64)
    large = MAX_EXACT + (np.log(nf / MAX_EXACT) / math.log(MAX_DISTANCE / MAX_EXACT)
                         * (N_BUCKETS - MAX_EXACT)).astype(np.int32)
    large = np.minimum(large, N_BUCKETS - 1)
    return np.where(n < MAX_EXACT, n, large).astype(np.int32)


def kernel(x, norm_w, w_in, w_out, mu_rwkv, w0, w2, a0, a2, k_k, k_a, r_k, lnx_w, lnx_b,
           q_norm_w, k_norm_w, sinks, rel_bias):
    b, s, d = x.shape
    assert (b, s, d) == (1, SEQ, D_MODEL) and norm_w.shape[0] == 1
    x2 = x.reshape(s, d)
    l = 0
    row = lambda t: t.reshape(1, -1).astype(F32)

    w_in_l = w_in[l]
    wa = w_in_l[:, RWKV_COLS:]
    w_att = jnp.concatenate([wa[:, :D_ATT], wa[:, D_ATT + 2 * D_KV:], wa[:, D_ATT:D_ATT + 2 * D_KV]], axis=1)
    zeros_l = jnp.zeros((LORA, D_RWKV), F32)
    w2p = jnp.concatenate([w2[l], zeros_l], axis=0).astype(BF16)
    a2p = jnp.concatenate([zeros_l, a2[l]], axis=0).astype(BF16)
    w_out_r = w_out[l][:D_RWKV].astype(BF16)
    w_out_a = w_out[l][D_RWKV:].astype(BF16)
    qnw = jnp.tile(q_norm_w[l], N_Q_HEADS).reshape(1, D_ATT)
    knw = jnp.tile(k_norm_w[l], N_KV_HEADS).reshape(1, D_KV)
    bucket = jnp.asarray(_t5_bucket_table())

    nw = row(norm_w[l])
    z_rwkv = _proj_in(x2, nw, w_in_l, RWKV_COLS, 512, 1408, "proj_in_rwkv", mu=row(mu_rwkv[l]))
    z_att = _proj_in(x2, nw, w_att, ATT_COLS, 512, 1152, "proj_in_att")

    y_rwkv = _rwkv(z_rwkv, row(w0[l]), w2p, row(a0[l]), a2p, row(k_k[l]), row(k_a[l]),
                   row(r_k[l]), row(lnx_w[l]), row(lnx_b[l]))
    y_att = _swa(z_att, bucket, sinks[l].astype(F32), rel_bias.astype(F32), qnw, knw)

    out = _proj_out(x2, y_rwkv, y_att, w_out_r, w_out_a, 512)
    return out.reshape(b, s, d)
```

```python
import functools
import math

import jax
import jax.numpy as jnp
import numpy as np
from jax import lax
from jax.experimental import pallas as pl
from jax.experimental.pallas import tpu as pltpu

F32 = jnp.float32
BF16 = jnp.bfloat16

D_MODEL = 2048
SEQ = 8192
HEAD_DIM = 64
D_RWKV = 1024
D_ATT = 1024
LORA = 64
N_Q_HEADS = 16
N_KV_HEADS = 2
D_KV = N_KV_HEADS * HEAD_DIM
WINDOW = 128
BLOCK = 128
N_BUCKETS = 32
MAX_EXACT = N_BUCKETS // 2
MAX_DISTANCE = 128
NORM_EPS = 1e-6
LNX_EPS = 64e-5
RWKV_COLS = 4 * D_RWKV + 2 * LORA
ATT_COLS = 2 * D_ATT + 2 * D_KV

LANES = 128
MXU_WIDTH = 256
N_PAIRS = D_RWKV // LANES
CHUNK = 64
RWKV_CHUNKS_PER_STEP = 4
NEG_BIG = -1e30

VMEM_LIMIT = 56 * 1024 * 1024


def _dot(a, b):
    return jnp.dot(a, b, preferred_element_type=F32)


def _dot_nt(a, b):
    return lax.dot_general(a, b, (((1,), (1,)), ((), ())), preferred_element_type=F32)


def _split3(x):
    hi = x.astype(BF16)
    r1 = x - hi.astype(F32)
    mid = r1.astype(BF16)
    lo = (r1 - mid.astype(F32)).astype(BF16)
    return hi, mid, lo


def _seg_sum(x, seg_ones):
    hi = x.astype(BF16)
    lo = (x - hi.astype(F32)).astype(BF16)
    return _dot(hi, seg_ones) + _dot(lo, seg_ones)


def _dot_exact_lhs(m, x):
    hi, mid, lo = _split3(x)
    return _dot(m, hi) + _dot(m, mid) + _dot(m, lo)


def _sigmoid(x):
    return 1.0 / (1.0 + jnp.exp(-x))


def _proj_in_kernel(x_ref, nw_ref, w_ref, *rest, token_shift):
    if token_shift:
        mu_ref, o_ref, wb_ref, prev_ref = rest
    else:
        o_ref, wb_ref = rest

    @pl.when(pl.program_id(1) == 0)
    def _():
        wb_ref[...] = w_ref[...].astype(BF16)
        if token_shift:
            prev_ref[...] = jnp.zeros_like(prev_ref)

    tm, tn = o_ref.shape
    x = x_ref[...]
    h = (x * nw_ref[...]).astype(BF16)
    rs = lax.rsqrt(jnp.mean(x * x, axis=-1, keepdims=True) + NORM_EPS)
    for c0 in range(0, tn, MXU_WIDTH):
        cs = slice(c0, min(c0 + MXU_WIDTH, tn))
        z = _dot(h, wb_ref[:, cs]) * rs
        if token_shift:
            row = lax.broadcasted_iota(jnp.int32, z.shape, 0)
            zprev = jnp.where(row == 0, prev_ref[:, cs], pltpu.roll(z, 1, axis=0))
            prev_ref[:, cs] = z[tm - 1:tm, :]
            z = z + (zprev - z) * mu_ref[:, cs]
        o_ref[:, cs] = z


def _proj_in(x2, norm_w, w, n, tm, tn, name, mu=None):
    s, d = x2.shape
    shift = mu is not None
    in_specs = [
        pl.BlockSpec((tm, d), lambda j, i: (i, 0)),
        pl.BlockSpec((1, d), lambda j, i: (0, 0)),
        pl.BlockSpec((d, tn), lambda j, i: (0, j), pipeline_mode=pl.Buffered(1)),
    ]
    args = [x2, norm_w, w]
    scratch = [pltpu.VMEM((d, tn), BF16)]
    if shift:
        in_specs.append(pl.BlockSpec((1, tn), lambda j, i: (0, j)))
        args.append(mu)
        scratch.append(pltpu.VMEM((1, tn), F32))
    return pl.pallas_call(
        functools.partial(_proj_in_kernel, token_shift=shift),
        grid=(n // tn, s // tm),
        in_specs=in_specs,
        out_specs=pl.BlockSpec((tm, tn), lambda j, i: (i, j)),
        out_shape=jax.ShapeDtypeStruct((s, n), F32),
        scratch_shapes=scratch,
        compiler_params=pltpu.CompilerParams(
            dimension_semantics=("arbitrary", "arbitrary"), vmem_limit_bytes=VMEM_LIMIT),
        name=name,
    )(*args)


def _rwkv_kernel(z_ref, w0_ref, w2_ref, a0_ref, a2_ref, kk_ref, ka_ref, rk_ref,
                 lw_ref, lb_ref, o_ref, gt_ref):
    c = CHUNK
    nch = RWKV_CHUNKS_PER_STEP
    t = c * nch
    i = pl.program_id(0)

    @pl.when(i == 0)
    def _():
        gt_ref[...] = jnp.zeros_like(gt_ref)

    r_all = z_ref[:, 0:D_RWKV]
    k_all = z_ref[:, D_RWKV:2 * D_RWKV]
    v_all = z_ref[:, 2 * D_RWKV:3 * D_RWKV]
    g_all = z_ref[:, 3 * D_RWKV:4 * D_RWKV]
    lora_in = z_ref[:, 4 * D_RWKV:]

    wlin = w0_ref[...] + _dot(jnp.tanh(lora_in).astype(BF16), w2_ref[...])
    neg = -wlin
    softplus = jnp.maximum(neg, 0.0) + jnp.log(1.0 + jnp.exp(-jnp.abs(neg)))
    logw_all = -jnp.exp(-softplus - 0.5)
    av_all = _sigmoid(a0_ref[...] + _dot(lora_in.astype(BF16), a2_ref[...]))

    ti = lax.broadcasted_iota(jnp.int32, (t, t), 0)
    si = lax.broadcasted_iota(jnp.int32, (t, t), 1)
    tril_blk = jnp.where((si <= ti) & ((si // c) == (ti // c)), 1.0, 0.0).astype(BF16)
    logp_all = _dot_exact_lhs(tril_blk, logw_all)

    e_p_all = jnp.exp(logp_all)
    e_px_all = jnp.exp(logp_all - logw_all)
    e_n_all = jnp.exp(-logp_all)
    kmod_all = k_all * (1.0 + (av_all - 1.0) * ka_ref[...])
    kk_all = k_all * kk_ref[...]
    rt_all = r_all * e_p_all
    kt_all = kmod_all * e_n_all

    lane = lax.broadcasted_iota(jnp.int32, (c, LANES), 1)
    head0 = lane < HEAD_DIM
    trow = lax.broadcasted_iota(jnp.int32, (c, LANES), 0)
    eye_cat = jnp.where((lane % HEAD_DIM) == trow, 1.0, 0.0)
    t2 = lax.broadcasted_iota(jnp.int32, (c, 2 * LANES), 0)
    s2 = lax.broadcasted_iota(jnp.int32, (c, 2 * LANES), 1) % HEAD_DIM
    strict2 = s2 < t2
    incl2 = s2 <= t2
    r128 = lax.broadcasted_iota(jnp.int32, (LANES, LANES), 0)
    c128 = lax.broadcasted_iota(jnp.int32, (LANES, LANES), 1)
    same_head = (r128 // HEAD_DIM) == (c128 // HEAD_DIM)
    same_head2 = jnp.concatenate([same_head, same_head], axis=1)
    seg_ones = jnp.where(same_head, 1.0, 0.0).astype(BF16)
    zeros_c = jnp.zeros((c, LANES), F32)

    def stack2(y):
        return jnp.concatenate([jnp.where(head0, y, 0.0), jnp.where(head0, 0.0, y)], axis=0)

    def catmul(a_cat, y):
        return _dot(a_cat.astype(BF16), stack2(y).astype(BF16))

    pairs = range(N_PAIRS)
    psl = [slice(p * LANES, (p + 1) * LANES) for p in pairs]
    items = [(ci, p) for ci in range(nch) for p in pairs]
    rows = lambda ci: slice(ci * c, (ci + 1) * c)

    ss = [_seg_sum(kk_all[:, s] * kk_all[:, s], seg_ones) for s in psl]
    kkn = [kk_all[:, s] * lax.rsqrt(jnp.maximum(q, 1e-24)) for s, q in zip(psl, ss)]
    at_p = [-n * e_px_all[:, s] for s, n in zip(psl, kkn)]
    bt_p = [n * av_all[:, s] * e_n_all[:, s] for s, n in zip(psl, kkn)]

    a_t = [at_p[p][rows(ci)] for ci, p in items]
    b_t = [bt_p[p][rows(ci)] for ci, p in items]
    r_t = [rt_all[rows(ci), psl[p]] for ci, p in items]
    k_t = [kt_all[rows(ci), psl[p]] for ci, p in items]
    v_i = [v_all[rows(ci), psl[p]] for ci, p in items]
    e_last = [e_p_all[ci * c + c - 1:ci * c + c, psl[p]] for ci, p in items]

    sc = [_dot_nt(jnp.concatenate([a, r], axis=0).astype(BF16),
                  jnp.concatenate([stack2(b), stack2(k)], axis=0).astype(BF16))
          for a, r, b, k in zip(a_t, r_t, b_t, k_t)]
    l_all = [jnp.where(strict2, s[0:c, :], 0.0) for s in sc]
    a_rbk = [jnp.where(incl2, s[c:, :], 0.0) for s in sc]

    x = [l[:, 0:LANES] for l in l_all]
    tinv = [eye_cat + xi for xi in x]
    x = [catmul(xi, xi) for xi in x]
    for _ in range(4):
        tx = [catmul(jnp.concatenate([ti_, xi], axis=0), xi) for ti_, xi in zip(tinv, x)]
        tinv = [ti_ + y[0:c] for ti_, y in zip(tinv, tx)]
        x = [y[c:] for y in tx]
    tinv = [ti_ + catmul(ti_, xi) for ti_, xi in zip(tinv, x)]

    a_eff = [catmul(ti_, a) for ti_, a in zip(tinv, a_t)]
    lakv = [catmul(l[:, LANES:], v) for l, v in zip(l_all, v_i)]
    w_loc = [catmul(ti_, y) for ti_, y in zip(tinv, lakv)]
    q_eff = [r + catmul(a[:, 0:LANES], ae) for r, a, ae in zip(r_t, a_rbk, a_eff)]
    o_loc = [_dot(a.astype(BF16), jnp.concatenate([stack2(w), stack2(v)], axis=0).astype(BF16))
             for a, w, v in zip(a_rbk, w_loc, v_i)]

    bh = [b * el for b, el in zip(b_t, e_last)]
    kh = [k * el for k, el in zip(k_t, e_last)]
    m_mat = [jnp.where(same_head, _dot(jnp.concatenate([ae, zeros_c], axis=0).T.astype(BF16),
                                       jnp.concatenate([b, zeros_c], axis=0).astype(BF16)), 0.0)
             for ae, b in zip(a_eff, bh)]
    n_mat = [jnp.where(same_head, _dot(jnp.concatenate([w, v], axis=0).T.astype(BF16),
                                       jnp.concatenate([b, k], axis=0).astype(BF16)), 0.0)
             for w, v, b, k in zip(w_loc, v_i, bh, kh)]

    gts = [gt_ref[p] for p in pairs]
    o_i = []
    for idx, (ci, p) in enumerate(items):
        gt = gts[p]
        gtb = gt.astype(BF16)
        o_i.append(_dot_nt(q_eff[idx].astype(BF16), gtb) + o_loc[idx])
        gts[p] = gt * e_last[idx] + _dot(gtb, m_mat[idx].astype(BF16)) + n_mat[idx]
    for p in pairs:
        gt_ref[p] = gts[p]

    for p in pairs:
        s = psl[p]
        o = jnp.concatenate([o_i[ci * N_PAIRS + p] for ci in range(nch)], axis=0)
        mean = _seg_sum(o, seg_ones) * (1.0 / HEAD_DIM)
        d = o - mean
        var = _seg_sum(d * d, seg_ones) * (1.0 / HEAD_DIM)
        on = d * lax.rsqrt(var + LNX_EPS) * lw_ref[:, s] + lb_ref[:, s]
        bonus = _seg_sum(r_all[:, s] * kmod_all[:, s] * rk_ref[:, s], seg_ones) * v_all[:, s]
        g = g_all[:, s]
        o_ref[:, s] = ((on + bonus) * (g * _sigmoid(g))).astype(o_ref.dtype)


def _rwkv(z_rwkv, w0, w2p, a0, a2p, k_k, k_a, r_k, lnx_w, lnx_b):
    s = z_rwkv.shape[0]
    c = CHUNK * RWKV_CHUNKS_PER_STEP
    row = lambda n: pl.BlockSpec((1, n), lambda i: (0, 0))
    return pl.pallas_call(
        _rwkv_kernel,
        grid=(s // c,),
        in_specs=[
            pl.BlockSpec((c, RWKV_COLS), lambda i: (i, 0)),
            row(D_RWKV),
            pl.BlockSpec((2 * LORA, D_RWKV), lambda i: (0, 0)),
            row(D_RWKV),
            pl.BlockSpec((2 * LORA, D_RWKV), lambda i: (0, 0)),
            row(D_RWKV), row(D_RWKV), row(D_RWKV), row(D_RWKV), row(D_RWKV),
        ],
        out_specs=pl.BlockSpec((c, D_RWKV), lambda i: (i, 0)),
        out_shape=jax.ShapeDtypeStruct((s, D_RWKV), BF16),
        scratch_shapes=[
            pltpu.VMEM((N_PAIRS, LANES, LANES), F32),
        ],
        compiler_params=pltpu.CompilerParams(
            dimension_semantics=("arbitrary",), vmem_limit_bytes=VMEM_LIMIT),
        name="rwkv7",
    )(z_rwkv, w0, w2p, a0, a2p, k_k, k_a, r_k, lnx_w, lnx_b)


def _swa_kernel(sinks_ref, relb_ref, bucket_ref, q_ref, g_ref, k_ref, v_ref, qnw_ref, knw_ref,
                o_ref, bias_ref, kprev_ref, vprev_ref):
    i = pl.program_id(0)
    bq = BLOCK

    @pl.when(i == 0)
    def _():
        kprev_ref[...] = jnp.zeros_like(kprev_ref)
        vprev_ref[...] = jnp.zeros_like(vprev_ref)
        bucket = bucket_ref[...]
        qi = lax.broadcasted_iota(jnp.int32, (bq, 2 * bq), 0)
        kj = lax.broadcasted_iota(jnp.int32, (bq, 2 * bq), 1)
        dist = bq + qi - kj
        inwin = (dist >= 0) & (dist < WINDOW)
        for h in range(N_Q_HEADS):
            acc = jnp.zeros((bq, 2 * bq), F32)
            for b in range(N_BUCKETS):
                acc = jnp.where(bucket == b, relb_ref[b, h], acc)
            acc = jnp.where(inwin, acc, NEG_BIG)
            bias_ref[0, h] = acc
            bias_ref[1, h] = jnp.where(kj >= bq, acc, NEG_BIG)

    first = jnp.where(i == 0, 1, 0)
    lane = lax.broadcasted_iota(jnp.int32, (bq, LANES), 1)
    head0 = lane < HEAD_DIM
    lane2 = lax.broadcasted_iota(jnp.int32, (2 * bq, LANES), 1)
    head0_2 = lane2 < HEAD_DIM
    r128 = lax.broadcasted_iota(jnp.int32, (LANES, LANES), 0)
    c128 = lax.broadcasted_iota(jnp.int32, (LANES, LANES), 1)
    seg_ones = jnp.where((r128 // HEAD_DIM) == (c128 // HEAD_DIM), 1.0, 0.0).astype(BF16)

    def rms_heads(t, w):
        ss = _seg_sum(t * t, seg_ones) * (1.0 / HEAD_DIM)
        return t * lax.rsqrt(ss + NORM_EPS) * w

    kn = rms_heads(k_ref[...], knw_ref[...])
    vc = v_ref[...]
    kcat = jnp.concatenate([kprev_ref[...], kn], axis=0)
    vcat = jnp.concatenate([vprev_ref[...], vc], axis=0)
    kprev_ref[...] = kn
    vprev_ref[...] = vc
    krol = pltpu.roll(kcat, HEAD_DIM, axis=1)
    vrol = pltpu.roll(vcat, HEAD_DIM, axis=1)
    kdup = [jnp.where(head0_2, kcat, krol).astype(BF16), jnp.where(head0_2, krol, kcat).astype(BF16)]
    vdup = [jnp.where(head0_2, vcat, vrol).astype(BF16), jnp.where(head0_2, vrol, vcat).astype(BF16)]

    scale = HEAD_DIM ** -0.5
    pairs = range(N_PAIRS)
    heads = range(N_Q_HEADS)
    psl = [slice(p * LANES, (p + 1) * LANES) for p in pairs]
    kvh = [(2 * p) // (N_Q_HEADS // N_KV_HEADS) for p in pairs]
    qn = [rms_heads(q_ref[:, s], qnw_ref[:, s]) * scale for s in psl]
    q2 = [jnp.concatenate([jnp.where(head0, q, 0.0), jnp.where(head0, 0.0, q)], axis=0).astype(BF16)
          for q in qn]
    lg2 = [_dot_nt(q, kdup[kv]) for q, kv in zip(q2, kvh)]
    lg = [lg2[h // 2][(h % 2) * bq:(h % 2 + 1) * bq, :] + bias_ref[first, h] for h in heads]
    m = [jnp.maximum(jnp.max(l, axis=-1, keepdims=True), sinks_ref[h]) for h, l in zip(heads, lg)]
    e = [jnp.exp(l - mm) for l, mm in zip(lg, m)]
    denom = [jnp.sum(ee, axis=-1, keepdims=True) + jnp.exp(sinks_ref[h] - mm)
             for h, ee, mm in zip(heads, e, m)]
    pv = [_dot(ee.astype(BF16), vdup[kvh[h // 2]]) for h, ee in zip(heads, e)]
    outs = [x / d for x, d in zip(pv, denom)]
    for p in pairs:
        o = jnp.where(head0, outs[2 * p], outs[2 * p + 1])
        g = g_ref[:, psl[p]]
        o_ref[:, psl[p]] = (o * (g * _sigmoid(g))).astype(o_ref.dtype)


def _swa(z_att, bucket, sinks, rel_bias, qnw, knw):
    s = z_att.shape[0]
    bq = BLOCK
    smem = pl.BlockSpec(memory_space=pltpu.SMEM)
    return pl.pallas_call(
        _swa_kernel,
        grid=(s // bq,),
        in_specs=[
            smem, smem,
            pl.BlockSpec((bq, 2 * bq), lambda i: (0, 0)),
            pl.BlockSpec((bq, D_ATT), lambda i: (i, 0)),
            pl.BlockSpec((bq, D_ATT), lambda i: (i, 1)),
            pl.BlockSpec((bq, D_KV), lambda i: (i, 2 * D_ATT // D_KV)),
            pl.BlockSpec((bq, D_KV), lambda i: (i, 2 * D_ATT // D_KV + 1)),
            pl.BlockSpec((1, D_ATT), lambda i: (0, 0)),
            pl.BlockSpec((1, D_KV), lambda i: (0, 0)),
        ],
        out_specs=pl.BlockSpec((bq, D_ATT), lambda i: (i, 0)),
        out_shape=jax.ShapeDtypeStruct((s, D_ATT), BF16),
        scratch_shapes=[
            pltpu.VMEM((2, N_Q_HEADS, bq, 2 * bq), F32),
            pltpu.VMEM((bq, D_KV), F32),
            pltpu.VMEM((bq, D_KV), F32),
        ],
        compiler_params=pltpu.CompilerParams(
            dimension_semantics=("arbitrary",), vmem_limit_bytes=VMEM_LIMIT),
        name="swa_sink",
    )(sinks, rel_bias, bucket, z_att, z_att, z_att, z_att, qnw, knw)


def _proj_out_kernel(x_ref, yr_ref, ya_ref, wr_ref, wa_ref, o_ref):
    o_ref[...] = x_ref[...] + _dot(yr_ref[...], wr_ref[...]) + _dot(ya_ref[...], wa_ref[...])


def _proj_out(x2, y_rwkv, y_att, w_r, w_a, tm):
    s, d = x2.shape
    return pl.pallas_call(
        _proj_out_kernel,
        grid=(s // tm,),
        in_specs=[
            pl.BlockSpec((tm, d), lambda i: (i, 0)),
            pl.BlockSpec((tm, D_RWKV), lambda i: (i, 0)),
            pl.BlockSpec((tm, D_ATT), lambda i: (i, 0)),
            pl.BlockSpec((D_RWKV, d), lambda i: (0, 0)),
            pl.BlockSpec((D_ATT, d), lambda i: (0, 0)),
        ],
        out_specs=pl.BlockSpec((tm, d), lambda i: (i, 0)),
        out_shape=jax.ShapeDtypeStruct((s, d), F32),
        compiler_params=pltpu.CompilerParams(
            dimension_semantics=("arbitrary",), vmem_limit_bytes=VMEM_LIMIT),
        name="proj_out",
    )(x2, y_rwkv, y_att, w_r, w_a)


def _t5_bucket_table():
    dist = BLOCK + np.arange(BLOCK)[:, None] - np.arange(2 * BLOCK)[None, :]
    n = np.maximum(dist, 0)
    nf = np.maximum(n, 1).astype(np.float64)
    large = MAX_EXACT + (np.log(nf / MAX_EXACT) / math.log(MAX_DISTANCE / MAX_EXACT)
                         * (N_BUCKETS - MAX_EXACT)).astype(np.int32)
    large = np.minimum(large, N_BUCKETS - 1)
    return np.where(n < MAX_EXACT, n, large).astype(np.int32)


def kernel(x, norm_w, w_in, w_out, mu_rwkv, w0, w2, a0, a2, k_k, k_a, r_k, lnx_w, lnx_b,
           q_norm_w, k_norm_w, sinks, rel_bias):
    b, s, d = x.shape
    assert (b, s, d) == (1, SEQ, D_MODEL) and norm_w.shape[0] == 1
    x2 = x.reshape(s, d)
    l = 0
    row = lambda t: t.reshape(1, -1).astype(F32)

    w_in_l = w_in[l]
    wa = w_in_l[:, RWKV_COLS:]
    w_att = jnp.concatenate([wa[:, :D_ATT], wa[:, D_ATT + 2 * D_KV:], wa[:, D_ATT:D_ATT + 2 * D_KV]], axis=1)
    zeros_l = jnp.zeros((LORA, D_RWKV), F32)
    w2p = jnp.concatenate([w2[l], zeros_l], axis=0).astype(BF16)
    a2p = jnp.concatenate([zeros_l, a2[l]], axis=0).astype(BF16)
    w_out_r = w_out[l][:D_RWKV].astype(BF16)
    w_out_a = w_out[l][D_RWKV:].astype(BF16)
    qnw = jnp.tile(q_norm_w[l], N_Q_HEADS).reshape(1, D_ATT)
    knw = jnp.tile(k_norm_w[l], N_KV_HEADS).reshape(1, D_KV)
    bucket = jnp.asarray(_t5_bucket_table())

    nw = row(norm_w[l])
    z_rwkv = _proj_in(x2, nw, w_in_l, RWKV_COLS, 1024, 1408, "proj_in_rwkv", mu=row(mu_rwkv[l]))
    z_att = _proj_in(x2, nw, w_att, ATT_COLS, 1024, 1152, "proj_in_att")

    y_rwkv = _rwkv(z_rwkv, row(w0[l]), w2p, row(a0[l]), a2p, row(k_k[l]), row(k_a[l]),
                   row(r_k[l]), row(lnx_w[l]), row(lnx_b[l]))
    y_att = _swa(z_att, bucket, sinks[l].astype(F32), rel_bias.astype(F32), qnw, knw)

    out = _proj_out(x2, y_rwkv, y_att, w_out_r, w_out_a, 512)
    return out.reshape(b, s, d)
```

```python
import functools
import math

import jax
import jax.numpy as jnp
import numpy as np
from jax import lax
from jax.experimental import pallas as pl
from jax.experimental.pallas import tpu as pltpu

F32 = jnp.float32
BF16 = jnp.bfloat16

D_MODEL = 2048
SEQ = 8192
HEAD_DIM = 64
D_RWKV = 1024
D_ATT = 1024
LORA = 64
N_Q_HEADS = 16
N_KV_HEADS = 2
D_KV = N_KV_HEADS * HEAD_DIM
WINDOW = 128
BLOCK = 128
N_BUCKETS = 32
MAX_EXACT = N_BUCKETS // 2
MAX_DISTANCE = 128
NORM_EPS = 1e-6
LNX_EPS = 64e-5
RWKV_COLS = 4 * D_RWKV + 2 * LORA
ATT_COLS = 2 * D_ATT + 2 * D_KV

LANES = 128
MXU_WIDTH = 256
N_PAIRS = D_RWKV // LANES
CHUNK = 64
RWKV_CHUNKS_PER_GROUP = 4
RWKV_CHUNKS_PER_STEP = 8
NEG_BIG = -1e30

VMEM_LIMIT = 56 * 1024 * 1024


def _dot(a, b):
    return jnp.dot(a, b, preferred_element_type=F32)


def _dot_nt(a, b):
    return lax.dot_general(a, b, (((1,), (1,)), ((), ())), preferred_element_type=F32)


def _split3(x):
    hi = x.astype(BF16)
    r1 = x - hi.astype(F32)
    mid = r1.astype(BF16)
    lo = (r1 - mid.astype(F32)).astype(BF16)
    return hi, mid, lo


def _seg_sum(x, seg_ones):
    hi = x.astype(BF16)
    lo = (x - hi.astype(F32)).astype(BF16)
    return _dot(hi, seg_ones) + _dot(lo, seg_ones)


def _dot_exact_lhs(m, x):
    hi, mid, lo = _split3(x)
    return _dot(m, hi) + _dot(m, mid) + _dot(m, lo)


def _sigmoid(x):
    return 1.0 / (1.0 + jnp.exp(-x))


def _proj_in_kernel(x_ref, nw_ref, w_ref, *rest, token_shift):
    if token_shift:
        mu_ref, o_ref, wb_ref, prev_ref = rest
    else:
        o_ref, wb_ref = rest

    @pl.when(pl.program_id(1) == 0)
    def _():
        wb_ref[...] = w_ref[...].astype(BF16)
        if token_shift:
            prev_ref[...] = jnp.zeros_like(prev_ref)

    tm, tn = o_ref.shape
    x = x_ref[...]
    h = (x * nw_ref[...]).astype(BF16)
    rs = lax.rsqrt(jnp.mean(x * x, axis=-1, keepdims=True) + NORM_EPS)
    for c0 in range(0, tn, MXU_WIDTH):
        cs = slice(c0, min(c0 + MXU_WIDTH, tn))
        z = _dot(h, wb_ref[:, cs]) * rs
        if token_shift:
            row = lax.broadcasted_iota(jnp.int32, z.shape, 0)
            zprev = jnp.where(row == 0, prev_ref[:, cs], pltpu.roll(z, 1, axis=0))
            prev_ref[:, cs] = z[tm - 1:tm, :]
            z = z + (zprev - z) * mu_ref[:, cs]
        o_ref[:, cs] = z


def _proj_in(x2, norm_w, w, n, tm, tn, name, mu=None):
    s, d = x2.shape
    shift = mu is not None
    in_specs = [
        pl.BlockSpec((tm, d), lambda j, i: (i, 0)),
        pl.BlockSpec((1, d), lambda j, i: (0, 0)),
        pl.BlockSpec((d, tn), lambda j, i: (0, j)),
    ]
    args = [x2, norm_w, w]
    scratch = [pltpu.VMEM((d, tn), BF16)]
    if shift:
        in_specs.append(pl.BlockSpec((1, tn), lambda j, i: (0, j)))
        args.append(mu)
        scratch.append(pltpu.VMEM((1, tn), F32))
    return pl.pallas_call(
        functools.partial(_proj_in_kernel, token_shift=shift),
        grid=(n // tn, s // tm),
        in_specs=in_specs,
        out_specs=pl.BlockSpec((tm, tn), lambda j, i: (i, j)),
        out_shape=jax.ShapeDtypeStruct((s, n), F32),
        scratch_shapes=scratch,
        compiler_params=pltpu.CompilerParams(
            dimension_semantics=("arbitrary", "arbitrary"), vmem_limit_bytes=VMEM_LIMIT),
        name=name,
    )(*args)


def _rwkv_kernel(z_ref, w0_ref, w2_ref, a0_ref, a2_ref, kk_ref, ka_ref, rk_ref,
                 lw_ref, lb_ref, o_ref, gt_ref):
    c = CHUNK
    nch = RWKV_CHUNKS_PER_STEP
    t = c * nch
    i = pl.program_id(0)

    @pl.when(i == 0)
    def _():
        gt_ref[...] = jnp.zeros_like(gt_ref)

    lane = lax.broadcasted_iota(jnp.int32, (c, LANES), 1)
    head0 = lane < HEAD_DIM
    trow = lax.broadcasted_iota(jnp.int32, (c, LANES), 0)
    eye_cat = jnp.where((lane % HEAD_DIM) == trow, 1.0, 0.0)
    t2 = lax.broadcasted_iota(jnp.int32, (c, 2 * LANES), 0)
    s2 = lax.broadcasted_iota(jnp.int32, (c, 2 * LANES), 1) % HEAD_DIM
    strict2 = s2 < t2
    incl2 = s2 <= t2
    r128 = lax.broadcasted_iota(jnp.int32, (LANES, LANES), 0)
    c128 = lax.broadcasted_iota(jnp.int32, (LANES, LANES), 1)
    same_head = (r128 // HEAD_DIM) == (c128 // HEAD_DIM)
    seg_ones = jnp.where(same_head, 1.0, 0.0).astype(BF16)
    zeros_c = jnp.zeros((c, LANES), F32)
    ti = lax.broadcasted_iota(jnp.int32, (c, c), 0)
    si = lax.broadcasted_iota(jnp.int32, (c, c), 1)
    tril_ones = jnp.where(si <= ti, 1.0, 0.0).astype(BF16)

    pairs = range(N_PAIRS)
    psl = [slice(p * LANES, (p + 1) * LANES) for p in pairs]

    def stack2(y):
        return jnp.concatenate([jnp.where(head0, y, 0.0), jnp.where(head0, 0.0, y)], axis=0)

    def catmul(a_cat, y):
        return _dot(a_cat.astype(BF16), stack2(y).astype(BF16))

    seg_mean = jnp.where(same_head, 1.0 / HEAD_DIM, 0.0).astype(BF16)

    def seg_sum_pairs(x, seg=seg_ones):
        y = _seg_sum(jnp.concatenate([x[:, s] for s in psl], axis=0), seg)
        return jnp.concatenate([y[p * c:(p + 1) * c] for p in pairs], axis=1)

    def prepare(ci):
        rs = slice(ci * c, (ci + 1) * c)
        r = z_ref[rs, 0:D_RWKV]
        k = z_ref[rs, D_RWKV:2 * D_RWKV]
        v = z_ref[rs, 2 * D_RWKV:3 * D_RWKV]
        lora_in = z_ref[rs, 4 * D_RWKV:]
        u = w0_ref[...] + _dot(jnp.tanh(lora_in).astype(BF16), w2_ref[...])
        logw = -math.exp(-0.5) * _sigmoid(u)
        av = _sigmoid(a0_ref[...] + _dot(lora_in.astype(BF16), a2_ref[...]))
        logp = _dot_exact_lhs(tril_ones, logw)
        e_p = jnp.exp(logp)
        e_n = jnp.exp(-logp)
        kmod = k * (1.0 + (av - 1.0) * ka_ref[...])
        kk = k * kk_ref[...]
        kkn = kk * lax.rsqrt(jnp.maximum(seg_sum_pairs(kk * kk), 1e-24))
        return dict(
            r_t=r * e_p, k_t=kmod * e_n, a_t=-kkn * jnp.exp(logp - logw), b_t=kkn * av * e_n, v=v,
            e_last=e_p[c - 1:c, :],
            bonus=seg_sum_pairs(r * kmod * rk_ref[...]) * v)

    def intra(pcs, out):
        per_item = lambda name: [pc[name][:, s] for pc in pcs for s in psl]
        a_t, b_t, r_t, k_t, v_i, e_last = (per_item(n) for n in ("a_t", "b_t", "r_t", "k_t", "v", "e_last"))
        sc = [_dot_nt(jnp.concatenate([a, r], axis=0).astype(BF16),
                      jnp.concatenate([stack2(b), stack2(k)], axis=0).astype(BF16))
              for a, r, b, k in zip(a_t, r_t, b_t, k_t)]
        yield
        l_all = [jnp.where(strict2, x[0:c, :], 0.0) for x in sc]
        a_rbk = [jnp.where(incl2, x[c:, :], 0.0) for x in sc]
        x = [l[:, 0:LANES] for l in l_all]
        tinv = [eye_cat + xi for xi in x]
        x = [catmul(xi, xi) for xi in x]
        yield
        for _ in range(4):
            tx = [catmul(jnp.concatenate([ti_, xi], axis=0), xi) for ti_, xi in zip(tinv, x)]
            tinv = [ti_ + y[0:c] for ti_, y in zip(tinv, tx)]
            x = [y[c:] for y in tx]
            yield
        tinv = [ti_ + catmul(ti_, xi) for ti_, xi in zip(tinv, x)]
        yield
        a_eff = [catmul(ti_, a) for ti_, a in zip(tinv, a_t)]
        yield
        lakv = [catmul(l[:, LANES:], v) for l, v in zip(l_all, v_i)]
        yield
        w_loc = [catmul(ti_, y) for ti_, y in zip(tinv, lakv)]
        yield
        q_eff = [r + catmul(a[:, 0:LANES], ae) for r, a, ae in zip(r_t, a_rbk, a_eff)]
        yield
        o_loc = [_dot(a.astype(BF16), jnp.concatenate([stack2(w), stack2(v)], axis=0).astype(BF16))
                 for a, w, v in zip(a_rbk, w_loc, v_i)]
        yield
        bh = [b * el for b, el in zip(b_t, e_last)]
        kh = [k * el for k, el in zip(k_t, e_last)]
        m_mat = [jnp.where(same_head, _dot(jnp.concatenate([ae, zeros_c], axis=0).T.astype(BF16),
                                           jnp.concatenate([b, zeros_c], axis=0).astype(BF16)), 0.0)
                 for ae, b in zip(a_eff, bh)]
        yield
        n_mat = [jnp.where(same_head, _dot(jnp.concatenate([w, v], axis=0).T.astype(BF16),
                                           jnp.concatenate([b, k], axis=0).astype(BF16)), 0.0)
                 for w, v, b, k in zip(w_loc, v_i, bh, kh)]
        out.update(q_eff=q_eff, o_loc=o_loc, m_mat=m_mat, n_mat=n_mat, e_last=e_last)

    def advance(gts, ic, j):
        sel = lambda name: ic[name][j * N_PAIRS:(j + 1) * N_PAIRS]
        gtb = [g.astype(BF16) for g in gts]
        o = [_dot_nt(q.astype(BF16), g) + ol for q, g, ol in zip(sel("q_eff"), gtb, sel("o_loc"))]
        gts = [g * el + _dot(gb, m.astype(BF16)) + n
               for g, gb, el, m, n in zip(gts, gtb, sel("e_last"), sel("m_mat"), sel("n_mat"))]
        return gts, jnp.concatenate(o, axis=1)

    def finish(ci, o, bonus):
        rs = slice(ci * c, (ci + 1) * c)
        d = o - seg_sum_pairs(o, seg_mean)
        var = seg_sum_pairs(d * d, seg_mean)
        on = d * lax.rsqrt(var + LNX_EPS) * lw_ref[...] + lb_ref[...]
        g = z_ref[rs, 3 * D_RWKV:4 * D_RWKV]
        o_ref[rs, :] = ((on + bonus) * (g * _sigmoid(g))).astype(o_ref.dtype)

    gsz = RWKV_CHUNKS_PER_GROUP
    ngroups = nch // gsz
    state = dict(gts=[gt_ref[p] for p in pairs])
    prepared = {}

    def prepare_group(g):
        for ci in range(g * gsz, (g + 1) * gsz):
            prepared[ci] = prepare(ci)
            yield

    def tail_group(g, ic):
        for j in range(gsz):
            ci = g * gsz + j
            state["gts"], o = advance(state["gts"], ic, j)
            yield
            finish(ci, o, prepared[ci]["bonus"])
            yield

    def drive(main, sides):
        for _ in main:
            for s_ in sides:
                next(s_, None)
        for s_ in sides:
            for _ in s_:
                pass

    drive(prepare_group(0), [])
    results = {}
    for g in range(ngroups):
        results[g] = {}
        sides = []
        if g + 1 < ngroups:
            sides.append(prepare_group(g + 1))
        if g >= 1:
            sides.append(tail_group(g - 1, results[g - 1]))
        drive(intra([prepared[ci] for ci in range(g * gsz, (g + 1) * gsz)], results[g]), sides)
    drive(tail_group(ngroups - 1, results[ngroups - 1]), [])
    for p in pairs:
        gt_ref[p] = state["gts"][p]


def _rwkv(z_rwkv, w0, w2p, a0, a2p, k_k, k_a, r_k, lnx_w, lnx_b):
    s = z_rwkv.shape[0]
    c = CHUNK * RWKV_CHUNKS_PER_STEP
    row = lambda n: pl.BlockSpec((1, n), lambda i: (0, 0))
    return pl.pallas_call(
        _rwkv_kernel,
        grid=(s // c,),
        in_specs=[
            pl.BlockSpec((c, RWKV_COLS), lambda i: (i, 0)),
            row(D_RWKV),
            pl.BlockSpec((2 * LORA, D_RWKV), lambda i: (0, 0)),
            row(D_RWKV),
            pl.BlockSpec((2 * LORA, D_RWKV), lambda i: (0, 0)),
            row(D_RWKV), row(D_RWKV), row(D_RWKV), row(D_RWKV), row(D_RWKV),
        ],
        out_specs=pl.BlockSpec((c, D_RWKV), lambda i: (i, 0)),
        out_shape=jax.ShapeDtypeStruct((s, D_RWKV), BF16),
        scratch_shapes=[
            pltpu.VMEM((N_PAIRS, LANES, LANES), F32),
        ],
        compiler_params=pltpu.CompilerParams(
            dimension_semantics=("arbitrary",), vmem_limit_bytes=VMEM_LIMIT),
        name="rwkv7",
    )(z_rwkv, w0, w2p, a0, a2p, k_k, k_a, r_k, lnx_w, lnx_b)


def _swa_kernel(sinks_ref, relb_ref, bucket_ref, q_ref, g_ref, k_ref, v_ref, qnw_ref, knw_ref,
                o_ref, bias_ref, kprev_ref, vprev_ref):
    i = pl.program_id(0)
    bq = BLOCK

    @pl.when(i == 0)
    def _():
        kprev_ref[...] = jnp.zeros_like(kprev_ref)
        vprev_ref[...] = jnp.zeros_like(vprev_ref)
        bucket = bucket_ref[...]
        qi = lax.broadcasted_iota(jnp.int32, (bq, 2 * bq), 0)
        kj = lax.broadcasted_iota(jnp.int32, (bq, 2 * bq), 1)
        dist = bq + qi - kj
        inwin = (dist >= 0) & (dist < WINDOW)
        for h in range(N_Q_HEADS):
            acc = jnp.zeros((bq, 2 * bq), F32)
            for b in range(N_BUCKETS):
                acc = jnp.where(bucket == b, relb_ref[b, h], acc)
            acc = jnp.where(inwin, acc, NEG_BIG)
            bias_ref[0, h] = acc
            bias_ref[1, h] = jnp.where(kj >= bq, acc, NEG_BIG)

    first = jnp.where(i == 0, 1, 0)
    lane = lax.broadcasted_iota(jnp.int32, (bq, LANES), 1)
    head0 = lane < HEAD_DIM
    lane2 = lax.broadcasted_iota(jnp.int32, (2 * bq, LANES), 1)
    head0_2 = lane2 < HEAD_DIM
    r128 = lax.broadcasted_iota(jnp.int32, (LANES, LANES), 0)
    c128 = lax.broadcasted_iota(jnp.int32, (LANES, LANES), 1)
    seg_ones = jnp.where((r128 // HEAD_DIM) == (c128 // HEAD_DIM), 1.0, 0.0).astype(BF16)

    def rms_heads(t, w):
        ss = _seg_sum(t * t, seg_ones) * (1.0 / HEAD_DIM)
        return t * lax.rsqrt(ss + NORM_EPS) * w

    kn = rms_heads(k_ref[...], knw_ref[...])
    vc = v_ref[...]
    kcat = jnp.concatenate([kprev_ref[...], kn], axis=0)
    vcat = jnp.concatenate([vprev_ref[...], vc], axis=0)
    kprev_ref[...] = kn
    vprev_ref[...] = vc
    krol = pltpu.roll(kcat, HEAD_DIM, axis=1)
    vrol = pltpu.roll(vcat, HEAD_DIM, axis=1)
    kdup = [jnp.where(head0_2, kcat, krol).astype(BF16), jnp.where(head0_2, krol, kcat).astype(BF16)]
    vdup = [jnp.where(head0_2, vcat, vrol).astype(BF16), jnp.where(head0_2, vrol, vcat).astype(BF16)]

    scale = HEAD_DIM ** -0.5
    pairs = range(N_PAIRS)
    heads = range(N_Q_HEADS)
    psl = [slice(p * LANES, (p + 1) * LANES) for p in pairs]
    kvh = [(2 * p) // (N_Q_HEADS // N_KV_HEADS) for p in pairs]
    qn = [rms_heads(q_ref[:, s], qnw_ref[:, s]) * scale for s in psl]
    q2 = [jnp.concatenate([jnp.where(head0, q, 0.0), jnp.where(head0, 0.0, q)], axis=0).astype(BF16)
          for q in qn]
    lg2 = [_dot_nt(q, kdup[kv]) for q, kv in zip(q2, kvh)]
    lg = [lg2[h // 2][(h % 2) * bq:(h % 2 + 1) * bq, :] + bias_ref[first, h] for h in heads]
    m = [jnp.maximum(jnp.max(l, axis=-1, keepdims=True), sinks_ref[h]) for h, l in zip(heads, lg)]
    e = [jnp.exp(l - mm) for l, mm in zip(lg, m)]
    denom = [jnp.sum(ee, axis=-1, keepdims=True) + jnp.exp(sinks_ref[h] - mm)
             for h, ee, mm in zip(heads, e, m)]
    pv = [_dot(ee.astype(BF16), vdup[kvh[h // 2]]) for h, ee in zip(heads, e)]
    outs = [x / d for x, d in zip(pv, denom)]
    for p in pairs:
        o = jnp.where(head0, outs[2 * p], outs[2 * p + 1])
        g = g_ref[:, psl[p]]
        o_ref[:, psl[p]] = (o * (g * _sigmoid(g))).astype(o_ref.dtype)


def _swa(z_att, bucket, sinks, rel_bias, qnw, knw):
    s = z_att.shape[0]
    bq = BLOCK
    smem = pl.BlockSpec(memory_space=pltpu.SMEM)
    return pl.pallas_call(
        _swa_kernel,
        grid=(s // bq,),
        in_specs=[
            smem, smem,
            pl.BlockSpec((bq, 2 * bq), lambda i: (0, 0)),
            pl.BlockSpec((bq, D_ATT), lambda i: (i, 0)),
            pl.BlockSpec((bq, D_ATT), lambda i: (i, 1)),
            pl.BlockSpec((bq, D_KV), lambda i: (i, 2 * D_ATT // D_KV)),
            pl.BlockSpec((bq, D_KV), lambda i: (i, 2 * D_ATT // D_KV + 1)),
            pl.BlockSpec((1, D_ATT), lambda i: (0, 0)),
            pl.BlockSpec((1, D_KV), lambda i: (0, 0)),
        ],
        out_specs=pl.BlockSpec((bq, D_ATT), lambda i: (i, 0)),
        out_shape=jax.ShapeDtypeStruct((s, D_ATT), BF16),
        scratch_shapes=[
            pltpu.VMEM((2, N_Q_HEADS, bq, 2 * bq), F32),
            pltpu.VMEM((bq, D_KV), F32),
            pltpu.VMEM((bq, D_KV), F32),
        ],
        compiler_params=pltpu.CompilerParams(
            dimension_semantics=("arbitrary",), vmem_limit_bytes=VMEM_LIMIT),
        name="swa_sink",
    )(sinks, rel_bias, bucket, z_att, z_att, z_att, z_att, qnw, knw)


def _proj_out_kernel(x_ref, yr_ref, ya_ref, wr_ref, wa_ref, o_ref):
    o_ref[...] = x_ref[...] + _dot(yr_ref[...], wr_ref[...]) + _dot(ya_ref[...], wa_ref[...])


def _proj_out(x2, y_rwkv, y_att, w_r, w_a, tm):
    s, d = x2.shape
    return pl.pallas_call(
        _proj_out_kernel,
        grid=(s // tm,),
        in_specs=[
            pl.BlockSpec((tm, d), lambda i: (i, 0)),
            pl.BlockSpec((tm, D_RWKV), lambda i: (i, 0)),
            pl.BlockSpec((tm, D_ATT), lambda i: (i, 0)),
            pl.BlockSpec((D_RWKV, d), lambda i: (0, 0)),
            pl.BlockSpec((D_ATT, d), lambda i: (0, 0)),
        ],
        out_specs=pl.BlockSpec((tm, d), lambda i: (i, 0)),
        out_shape=jax.ShapeDtypeStruct((s, d), F32),
        compiler_params=pltpu.CompilerParams(
            dimension_semantics=("arbitrary",), vmem_limit_bytes=VMEM_LIMIT),
        name="proj_out",
    )(x2, y_rwkv, y_att, w_r, w_a)


def _t5_bucket_table():
    dist = BLOCK + np.arange(BLOCK)[:, None] - np.arange(2 * BLOCK)[None, :]
    n = np.maximum(dist, 0)
    nf = np.maximum(n, 1).astype(np.float64)
    large = MAX_EXACT + (np.log(nf / MAX_EXACT) / math.log(MAX_DISTANCE / MAX_EXACT)
                         * (N_BUCKETS - MAX_EXACT)).astype(np.int32)
    large = np.minimum(large, N_BUCKETS - 1)
    return np.where(n < MAX_EXACT, n, large).astype(np.int32)


def kernel(x, norm_w, w_in, w_out, mu_rwkv, w0, w2, a0, a2, k_k, k_a, r_k, lnx_w, lnx_b,
           q_norm_w, k_norm_w, sinks, rel_bias):
    b, s, d = x.shape
    assert (b, s, d) == (1, SEQ, D_MODEL) and norm_w.shape[0] == 1
    x2 = x.reshape(s, d)
    l = 0
    row = lambda t: t.reshape(1, -1).astype(F32)

    w_in_l = w_in[l]
    wa = w_in_l[:, RWKV_COLS:]
    w_att = jnp.concatenate([wa[:, :D_ATT], wa[:, D_ATT + 2 * D_KV:], wa[:, D_ATT:D_ATT + 2 * D_KV]], axis=1)
    zeros_l = jnp.zeros((LORA, D_RWKV), F32)
    w2p = jnp.concatenate([w2[l], zeros_l], axis=0).astype(BF16)
    a2p = jnp.concatenate([zeros_l, a2[l]], axis=0).astype(BF16)
    w_out_r = w_out[l][:D_RWKV].astype(BF16)
    w_out_a = w_out[l][D_RWKV:].astype(BF16)
    qnw = jnp.tile(q_norm_w[l], N_Q_HEADS).reshape(1, D_ATT)
    knw = jnp.tile(k_norm_w[l], N_KV_HEADS).reshape(1, D_KV)
    bucket = jnp.asarray(_t5_bucket_table())

    nw = row(norm_w[l])
    z_rwkv = _proj_in(x2, nw, w_in_l, RWKV_COLS, 512, 1408, "proj_in_rwkv", mu=row(mu_rwkv[l]))
    z_att = _proj_in(x2, nw, w_att, ATT_COLS, 512, 1152, "proj_in_att")

    y_rwkv = _rwkv(z_rwkv, row(w0[l]), w2p, row(a0[l]), a2p, row(k_k[l]), row(k_a[l]),
                   row(r_k[l]), row(lnx_w[l]), row(lnx_b[l]))
    y_att = _swa(z_att, bucket, sinks[l].astype(F32), rel_bias.astype(F32), qnw, knw)

    out = _proj_out(x2, y_rwkv, y_att, w_out_r, w_out_a, 512)
    return out.reshape(b, s, d)
```

```python
import functools
import math

import jax
import jax.numpy as jnp
import numpy as np
from jax import lax
from jax.experimental import pallas as pl
from jax.experimental.pallas import tpu as pltpu

F32 = jnp.float32
BF16 = jnp.bfloat16

D_MODEL = 2048
SEQ = 8192
HEAD_DIM = 64
D_RWKV = 1024
D_ATT = 1024
LORA = 64
N_Q_HEADS = 16
N_KV_HEADS = 2
D_KV = N_KV_HEADS * HEAD_DIM
WINDOW = 128
BLOCK = 128
N_BUCKETS = 32
MAX_EXACT = N_BUCKETS // 2
MAX_DISTANCE = 128
NORM_EPS = 1e-6
LNX_EPS = 64e-5
RWKV_COLS = 4 * D_RWKV + 2 * LORA
ATT_COLS = 2 * D_ATT + 2 * D_KV

LANES = 128
MXU_WIDTH = 256
N_PAIRS = D_RWKV // LANES
CHUNK = 64
RWKV_CHUNKS_PER_GROUP = 4
RWKV_CHUNKS_PER_STEP = 8
NEG_BIG = -1e30
LOG2E = math.log2(math.e)

VMEM_LIMIT = 56 * 1024 * 1024


def _dot(a, b):
    return jnp.dot(a, b, preferred_element_type=F32)


def _dot_nt(a, b):
    return lax.dot_general(a, b, (((1,), (1,)), ((), ())), preferred_element_type=F32)


def _split3(x):
    hi = x.astype(BF16)
    r1 = x - hi.astype(F32)
    mid = r1.astype(BF16)
    lo = (r1 - mid.astype(F32)).astype(BF16)
    return hi, mid, lo


def _seg_sum(x, seg_ones):
    hi = x.astype(BF16)
    lo = (x - hi.astype(F32)).astype(BF16)
    return _dot(hi, seg_ones) + _dot(lo, seg_ones)


def _dot_exact_lhs(m, x):
    hi, mid, lo = _split3(x)
    return _dot(m, hi) + _dot(m, mid) + _dot(m, lo)


def _sigmoid(x):
    return 1.0 / (1.0 + jnp.exp(-x))


def _proj_in_kernel(x_ref, nw_ref, w_ref, *rest, token_shift):
    if token_shift:
        mu_ref, o_ref, wb_ref, prev_ref = rest
    else:
        o_ref, wb_ref = rest

    @pl.when(pl.program_id(1) == 0)
    def _():
        wb_ref[...] = w_ref[...].astype(BF16)
        if token_shift:
            prev_ref[...] = jnp.zeros_like(prev_ref)

    tm, tn = o_ref.shape
    x = x_ref[...]
    h = (x * nw_ref[...]).astype(BF16)
    rs = lax.rsqrt(jnp.mean(x * x, axis=-1, keepdims=True) + NORM_EPS)
    for c0 in range(0, tn, MXU_WIDTH):
        cs = slice(c0, min(c0 + MXU_WIDTH, tn))
        z = _dot(h, wb_ref[:, cs]) * rs
        if token_shift:
            row = lax.broadcasted_iota(jnp.int32, z.shape, 0)
            zprev = jnp.where(row == 0, prev_ref[:, cs], pltpu.roll(z, 1, axis=0))
            prev_ref[:, cs] = z[tm - 1:tm, :]
            z = z + (zprev - z) * mu_ref[:, cs]
        o_ref[:, cs] = z


def _proj_in(x2, norm_w, w, n, tm, tn, name, mu=None):
    s, d = x2.shape
    shift = mu is not None
    in_specs = [
        pl.BlockSpec((tm, d), lambda j, i: (i, 0)),
        pl.BlockSpec((1, d), lambda j, i: (0, 0)),
        pl.BlockSpec((d, tn), lambda j, i: (0, j)),
    ]
    args = [x2, norm_w, w]
    scratch = [pltpu.VMEM((d, tn), BF16)]
    if shift:
        in_specs.append(pl.BlockSpec((1, tn), lambda j, i: (0, j)))
        args.append(mu)
        scratch.append(pltpu.VMEM((1, tn), F32))
    return pl.pallas_call(
        functools.partial(_proj_in_kernel, token_shift=shift),
        grid=(n // tn, s // tm),
        in_specs=in_specs,
        out_specs=pl.BlockSpec((tm, tn), lambda j, i: (i, j)),
        out_shape=jax.ShapeDtypeStruct((s, n), F32),
        scratch_shapes=scratch,
        compiler_params=pltpu.CompilerParams(
            dimension_semantics=("arbitrary", "arbitrary"), vmem_limit_bytes=VMEM_LIMIT),
        name=name,
    )(*args)


def _rwkv_kernel(z_ref, w0_ref, w2_ref, a0_ref, a2_ref, kk_ref, ka_ref, rk_ref,
                 lw_ref, lb_ref, o_ref, gt_ref):
    c = CHUNK
    nch = RWKV_CHUNKS_PER_STEP
    t = c * nch
    i = pl.program_id(0)

    @pl.when(i == 0)
    def _():
        gt_ref[...] = jnp.zeros_like(gt_ref)

    lane = lax.broadcasted_iota(jnp.int32, (c, LANES), 1)
    head0 = lane < HEAD_DIM
    trow = lax.broadcasted_iota(jnp.int32, (c, LANES), 0)
    eye_cat = jnp.where((lane % HEAD_DIM) == trow, 1.0, 0.0)
    t2 = lax.broadcasted_iota(jnp.int32, (c, 2 * LANES), 0)
    s2 = lax.broadcasted_iota(jnp.int32, (c, 2 * LANES), 1) % HEAD_DIM
    strict2 = s2 < t2
    incl2 = s2 <= t2
    r128 = lax.broadcasted_iota(jnp.int32, (LANES, LANES), 0)
    c128 = lax.broadcasted_iota(jnp.int32, (LANES, LANES), 1)
    same_head = (r128 // HEAD_DIM) == (c128 // HEAD_DIM)
    seg_ones = jnp.where(same_head, 1.0, 0.0).astype(BF16)
    zeros_c = jnp.zeros((c, LANES), F32)
    ti = lax.broadcasted_iota(jnp.int32, (c, c), 0)
    si = lax.broadcasted_iota(jnp.int32, (c, c), 1)
    tril_ones = jnp.where(si <= ti, 1.0, 0.0).astype(BF16)

    pairs = range(N_PAIRS)
    psl = [slice(p * LANES, (p + 1) * LANES) for p in pairs]

    def stack2(y):
        return jnp.concatenate([jnp.where(head0, y, 0.0), jnp.where(head0, 0.0, y)], axis=0)

    def catmul(a_cat, y):
        return _dot(a_cat.astype(BF16), stack2(y).astype(BF16))

    seg_mean = jnp.where(same_head, 1.0 / HEAD_DIM, 0.0).astype(BF16)

    def seg_sum_pairs(x, seg=seg_ones):
        y = _seg_sum(jnp.concatenate([x[:, s] for s in psl], axis=0), seg)
        return jnp.concatenate([y[p * c:(p + 1) * c] for p in pairs], axis=1)

    def prepare(ci):
        rs = slice(ci * c, (ci + 1) * c)
        r = z_ref[rs, 0:D_RWKV]
        k = z_ref[rs, D_RWKV:2 * D_RWKV]
        v = z_ref[rs, 2 * D_RWKV:3 * D_RWKV]
        lora_in = z_ref[rs, 4 * D_RWKV:]
        u = w0_ref[...] + _dot(jnp.tanh(lora_in).astype(BF16), w2_ref[...])
        logw = -math.exp(-0.5) * _sigmoid(u)
        av = _sigmoid(a0_ref[...] + _dot(lora_in.astype(BF16), a2_ref[...]))
        logp = _dot_exact_lhs(tril_ones, logw)
        e_p = jnp.exp(logp)
        e_n = 1.0 / e_p
        kmod = k * (1.0 + (av - 1.0) * ka_ref[...])
        kk = k * kk_ref[...]
        kkn = kk * lax.rsqrt(jnp.maximum(seg_sum_pairs(kk * kk), 1e-24))
        return dict(
            r_t=r * e_p, k_t=kmod * e_n, a_t=-kkn * jnp.exp(logp - logw), b_t=kkn * av * e_n, v=v,
            e_last=e_p[c - 1:c, :],
            bonus=seg_sum_pairs(r * kmod * rk_ref[...]) * v)

    def intra(pcs, out):
        per_item = lambda name: [pc[name][:, s] for pc in pcs for s in psl]
        a_t, b_t, r_t, k_t, v_i, e_last = (per_item(n) for n in ("a_t", "b_t", "r_t", "k_t", "v", "e_last"))
        sc = [_dot_nt(jnp.concatenate([a, r], axis=0).astype(BF16),
                      jnp.concatenate([stack2(b), stack2(k)], axis=0).astype(BF16))
              for a, r, b, k in zip(a_t, r_t, b_t, k_t)]
        yield
        l_all = [jnp.where(strict2, x[0:c, :], 0.0) for x in sc]
        a_rbk = [jnp.where(incl2, x[c:, :], 0.0) for x in sc]
        x = [l[:, 0:LANES] for l in l_all]
        tinv = [eye_cat + xi for xi in x]
        x = [catmul(xi, xi) for xi in x]
        yield
        for _ in range(4):
            tx = [catmul(jnp.concatenate([ti_, xi], axis=0), xi) for ti_, xi in zip(tinv, x)]
            tinv = [ti_ + y[0:c] for ti_, y in zip(tinv, tx)]
            x = [y[c:] for y in tx]
            yield
        tinv = [ti_ + catmul(ti_, xi) for ti_, xi in zip(tinv, x)]
        yield
        a_eff = [catmul(ti_, a) for ti_, a in zip(tinv, a_t)]
        yield
        lakv = [catmul(l[:, LANES:], v) for l, v in zip(l_all, v_i)]
        yield
        w_loc = [catmul(ti_, y) for ti_, y in zip(tinv, lakv)]
        yield
        q_eff = [r + catmul(a[:, 0:LANES], ae) for r, a, ae in zip(r_t, a_rbk, a_eff)]
        yield
        o_loc = [_dot(a.astype(BF16), jnp.concatenate([stack2(w), stack2(v)], axis=0).astype(BF16))
                 for a, w, v in zip(a_rbk, w_loc, v_i)]
        yield
        bh = [b * el for b, el in zip(b_t, e_last)]
        kh = [k * el for k, el in zip(k_t, e_last)]
        m_mat = [jnp.where(same_head, _dot(jnp.concatenate([ae, zeros_c], axis=0).T.astype(BF16),
                                           jnp.concatenate([b, zeros_c], axis=0).astype(BF16)), 0.0)
                 for ae, b in zip(a_eff, bh)]
        yield
        n_mat = [jnp.where(same_head, _dot(jnp.concatenate([w, v], axis=0).T.astype(BF16),
                                           jnp.concatenate([b, k], axis=0).astype(BF16)), 0.0)
                 for w, v, b, k in zip(w_loc, v_i, bh, kh)]
        out.update(q_eff=q_eff, o_loc=o_loc, m_mat=m_mat, n_mat=n_mat, e_last=e_last)

    def advance(gts, ic, j):
        sel = lambda name: ic[name][j * N_PAIRS:(j + 1) * N_PAIRS]
        gtb = [g.astype(BF16) for g in gts]
        o = [_dot_nt(q.astype(BF16), g) + ol for q, g, ol in zip(sel("q_eff"), gtb, sel("o_loc"))]
        gts = [g * el + _dot(gb, m.astype(BF16)) + n
               for g, gb, el, m, n in zip(gts, gtb, sel("e_last"), sel("m_mat"), sel("n_mat"))]
        return gts, jnp.concatenate(o, axis=1)

    def finish(ci, o, bonus):
        rs = slice(ci * c, (ci + 1) * c)
        d = o - seg_sum_pairs(o, seg_mean)
        var = seg_sum_pairs(d * d, seg_mean)
        on = d * lax.rsqrt(var + LNX_EPS) * lw_ref[...] + lb_ref[...]
        g = z_ref[rs, 3 * D_RWKV:4 * D_RWKV]
        o_ref[rs, :] = ((on + bonus) * (g * _sigmoid(g))).astype(o_ref.dtype)

    gsz = RWKV_CHUNKS_PER_GROUP
    ngroups = nch // gsz
    state = dict(gts=[gt_ref[p] for p in pairs])
    prepared = {}

    def prepare_group(g):
        for ci in range(g * gsz, (g + 1) * gsz):
            prepared[ci] = prepare(ci)
            yield

    def tail_group(g, ic):
        for j in range(gsz):
            ci = g * gsz + j
            state["gts"], o = advance(state["gts"], ic, j)
            yield
            finish(ci, o, prepared[ci]["bonus"])
            yield

    def drive(main, sides):
        for _ in main:
            for s_ in sides:
                next(s_, None)
        for s_ in sides:
            for _ in s_:
                pass

    drive(prepare_group(0), [])
    results = {}
    for g in range(ngroups):
        results[g] = {}
        sides = []
        if g + 1 < ngroups:
            sides.append(prepare_group(g + 1))
        if g >= 1:
            sides.append(tail_group(g - 1, results[g - 1]))
        drive(intra([prepared[ci] for ci in range(g * gsz, (g + 1) * gsz)], results[g]), sides)
    drive(tail_group(ngroups - 1, results[ngroups - 1]), [])
    for p in pairs:
        gt_ref[p] = state["gts"][p]


def _rwkv(z_rwkv, w0, w2p, a0, a2p, k_k, k_a, r_k, lnx_w, lnx_b):
    s = z_rwkv.shape[0]
    c = CHUNK * RWKV_CHUNKS_PER_STEP
    row = lambda n: pl.BlockSpec((1, n), lambda i: (0, 0))
    return pl.pallas_call(
        _rwkv_kernel,
        grid=(s // c,),
        in_specs=[
            pl.BlockSpec((c, RWKV_COLS), lambda i: (i, 0)),
            row(D_RWKV),
            pl.BlockSpec((2 * LORA, D_RWKV), lambda i: (0, 0)),
            row(D_RWKV),
            pl.BlockSpec((2 * LORA, D_RWKV), lambda i: (0, 0)),
            row(D_RWKV), row(D_RWKV), row(D_RWKV), row(D_RWKV), row(D_RWKV),
        ],
        out_specs=pl.BlockSpec((c, D_RWKV), lambda i: (i, 0)),
        out_shape=jax.ShapeDtypeStruct((s, D_RWKV), BF16),
        scratch_shapes=[
            pltpu.VMEM((N_PAIRS, LANES, LANES), F32),
        ],
        compiler_params=pltpu.CompilerParams(
            dimension_semantics=("arbitrary",), vmem_limit_bytes=VMEM_LIMIT),
        name="rwkv7",
    )(z_rwkv, w0, w2p, a0, a2p, k_k, k_a, r_k, lnx_w, lnx_b)


def _swa_kernel(sinks_ref, relb_ref, bucket_ref, q_ref, g_ref, k_ref, v_ref, qnw_ref, knw_ref,
                o_ref, bias_ref, kprev_ref, vprev_ref):
    i = pl.program_id(0)
    bq = BLOCK

    @pl.when(i == 0)
    def _():
        kprev_ref[...] = jnp.zeros_like(kprev_ref)
        vprev_ref[...] = jnp.zeros_like(vprev_ref)
        bucket = bucket_ref[...]
        qi = lax.broadcasted_iota(jnp.int32, (bq, 2 * bq), 0)
        kj = lax.broadcasted_iota(jnp.int32, (bq, 2 * bq), 1)
        dist = bq + qi - kj
        inwin = (dist >= 0) & (dist < WINDOW)
        for h in range(N_Q_HEADS):
            acc = jnp.zeros((bq, 2 * bq), F32)
            for b in range(N_BUCKETS):
                acc = jnp.where(bucket == b, relb_ref[b, h], acc)
            acc = jnp.where(inwin, acc * LOG2E, NEG_BIG)
            bias_ref[0, h] = acc
            bias_ref[1, h] = jnp.where(kj >= bq, acc, NEG_BIG)

    first = jnp.where(i == 0, 1, 0)
    lane = lax.broadcasted_iota(jnp.int32, (bq, LANES), 1)
    head0 = lane < HEAD_DIM
    lane2 = lax.broadcasted_iota(jnp.int32, (2 * bq, LANES), 1)
    head0_2 = lane2 < HEAD_DIM
    r128 = lax.broadcasted_iota(jnp.int32, (LANES, LANES), 0)
    c128 = lax.broadcasted_iota(jnp.int32, (LANES, LANES), 1)
    seg_mean = jnp.where((r128 // HEAD_DIM) == (c128 // HEAD_DIM), 1.0 / HEAD_DIM, 0.0).astype(BF16)

    def rms_heads(t, w):
        return t * lax.rsqrt(_seg_sum(t * t, seg_mean) + NORM_EPS) * w

    kn = rms_heads(k_ref[...], knw_ref[...])
    vc = v_ref[...]
    kcat = jnp.concatenate([kprev_ref[...], kn], axis=0)
    vcat = jnp.concatenate([vprev_ref[...], vc], axis=0)
    kprev_ref[...] = kn
    vprev_ref[...] = vc
    krol = pltpu.roll(kcat, HEAD_DIM, axis=1)
    vrol = pltpu.roll(vcat, HEAD_DIM, axis=1)
    kdup = [jnp.where(head0_2, kcat, krol).astype(BF16), jnp.where(head0_2, krol, kcat).astype(BF16)]
    vdup = [jnp.where(head0_2, vcat, vrol).astype(BF16), jnp.where(head0_2, vrol, vcat).astype(BF16)]

    sink2 = [sinks_ref[h] * LOG2E for h in range(N_Q_HEADS)]
    pairs = range(N_PAIRS)
    heads = range(N_Q_HEADS)
    psl = [slice(p * LANES, (p + 1) * LANES) for p in pairs]
    kvh = [(2 * p) // (N_Q_HEADS // N_KV_HEADS) for p in pairs]
    qn = [rms_heads(q_ref[:, s], qnw_ref[:, s]) for s in psl]
    q2 = [jnp.concatenate([jnp.where(head0, q, 0.0), jnp.where(head0, 0.0, q)], axis=0).astype(BF16)
          for q in qn]
    lg2 = [_dot_nt(q, kdup[kv]) for q, kv in zip(q2, kvh)]
    lg = [lg2[h // 2][(h % 2) * bq:(h % 2 + 1) * bq, :] + bias_ref[first, h] for h in heads]
    m = [jnp.maximum(jnp.max(l, axis=-1, keepdims=True), sink2[h]) for h, l in zip(heads, lg)]
    e = [jnp.exp2(l - mm) for l, mm in zip(lg, m)]
    denom = [jnp.sum(ee, axis=-1, keepdims=True) + jnp.exp2(sink2[h] - mm)
             for h, ee, mm in zip(heads, e, m)]
    pv = [_dot(ee.astype(BF16), vdup[kvh[h // 2]]) for h, ee in zip(heads, e)]
    outs = [x / d for x, d in zip(pv, denom)]
    for p in pairs:
        o = jnp.where(head0, outs[2 * p], outs[2 * p + 1])
        g = g_ref[:, psl[p]]
        o_ref[:, psl[p]] = (o * (g * _sigmoid(g))).astype(o_ref.dtype)


def _swa(z_att, bucket, sinks, rel_bias, qnw, knw):
    s = z_att.shape[0]
    bq = BLOCK
    smem = pl.BlockSpec(memory_space=pltpu.SMEM)
    return pl.pallas_call(
        _swa_kernel,
        grid=(s // bq,),
        in_specs=[
            smem, smem,
            pl.BlockSpec((bq, 2 * bq), lambda i: (0, 0)),
            pl.BlockSpec((bq, D_ATT), lambda i: (i, 0)),
            pl.BlockSpec((bq, D_ATT), lambda i: (i, 1)),
            pl.BlockSpec((bq, D_KV), lambda i: (i, 2 * D_ATT // D_KV)),
            pl.BlockSpec((bq, D_KV), lambda i: (i, 2 * D_ATT // D_KV + 1)),
            pl.BlockSpec((1, D_ATT), lambda i: (0, 0)),
            pl.BlockSpec((1, D_KV), lambda i: (0, 0)),
        ],
        out_specs=pl.BlockSpec((bq, D_ATT), lambda i: (i, 0)),
        out_shape=jax.ShapeDtypeStruct((s, D_ATT), BF16),
        scratch_shapes=[
            pltpu.VMEM((2, N_Q_HEADS, bq, 2 * bq), F32),
            pltpu.VMEM((bq, D_KV), F32),
            pltpu.VMEM((bq, D_KV), F32),
        ],
        compiler_params=pltpu.CompilerParams(
            dimension_semantics=("arbitrary",), vmem_limit_bytes=VMEM_LIMIT),
        name="swa_sink",
    )(sinks, rel_bias, bucket, z_att, z_att, z_att, z_att, qnw, knw)


def _proj_out_kernel(x_ref, yr_ref, ya_ref, wr_ref, wa_ref, o_ref):
    o_ref[...] = x_ref[...] + _dot(yr_ref[...], wr_ref[...]) + _dot(ya_ref[...], wa_ref[...])


def _proj_out(x2, y_rwkv, y_att, w_r, w_a, tm):
    s, d = x2.shape
    return pl.pallas_call(
        _proj_out_kernel,
        grid=(s // tm,),
        in_specs=[
            pl.BlockSpec((tm, d), lambda i: (i, 0)),
            pl.BlockSpec((tm, D_RWKV), lambda i: (i, 0)),
            pl.BlockSpec((tm, D_ATT), lambda i: (i, 0)),
            pl.BlockSpec((D_RWKV, d), lambda i: (0, 0)),
            pl.BlockSpec((D_ATT, d), lambda i: (0, 0)),
        ],
        out_specs=pl.BlockSpec((tm, d), lambda i: (i, 0)),
        out_shape=jax.ShapeDtypeStruct((s, d), F32),
        compiler_params=pltpu.CompilerParams(
            dimension_semantics=("arbitrary",), vmem_limit_bytes=VMEM_LIMIT),
        name="proj_out",
    )(x2, y_rwkv, y_att, w_r, w_a)


def _t5_bucket_table():
    dist = BLOCK + np.arange(BLOCK)[:, None] - np.arange(2 * BLOCK)[None, :]
    n = np.maximum(dist, 0)
    nf = np.maximum(n, 1).astype(np.float64)
    large = MAX_EXACT + (np.log(nf / MAX_EXACT) / math.log(MAX_DISTANCE / MAX_EXACT)
                         * (N_BUCKETS - MAX_EXACT)).astype(np.int32)
    large = np.minimum(large, N_BUCKETS - 1)
    return np.where(n < MAX_EXACT, n, large).astype(np.int32)


def kernel(x, norm_w, w_in, w_out, mu_rwkv, w0, w2, a0, a2, k_k, k_a, r_k, lnx_w, lnx_b,
           q_norm_w, k_norm_w, sinks, rel_bias):
    b, s, d = x.shape
    assert (b, s, d) == (1, SEQ, D_MODEL) and norm_w.shape[0] == 1
    x2 = x.reshape(s, d)
    l = 0
    row = lambda t: t.reshape(1, -1).astype(F32)

    w_in_l = w_in[l]
    wa = w_in_l[:, RWKV_COLS:]
    w_att = jnp.concatenate([wa[:, :D_ATT], wa[:, D_ATT + 2 * D_KV:], wa[:, D_ATT:D_ATT + 2 * D_KV]], axis=1)
    zeros_l = jnp.zeros((LORA, D_RWKV), F32)
    w2p = jnp.concatenate([w2[l], zeros_l], axis=0).astype(BF16)
    a2p = jnp.concatenate([zeros_l, a2[l]], axis=0).astype(BF16)
    w_out_r = w_out[l][:D_RWKV].astype(BF16)
    w_out_a = w_out[l][D_RWKV:].astype(BF16)
    qnw = jnp.tile(q_norm_w[l] * (HEAD_DIM ** -0.5 * LOG2E), N_Q_HEADS).reshape(1, D_ATT)
    knw = jnp.tile(k_norm_w[l], N_KV_HEADS).reshape(1, D_KV)
    bucket = jnp.asarray(_t5_bucket_table())

    nw = row(norm_w[l])
    z_rwkv = _proj_in(x2, nw, w_in_l, RWKV_COLS, 512, 1408, "proj_in_rwkv", mu=row(mu_rwkv[l]))
    z_att = _proj_in(x2, nw, w_att, ATT_COLS, 512, 1152, "proj_in_att")

    y_rwkv = _rwkv(z_rwkv, row(w0[l]), w2p, row(a0[l]), a2p, row(k_k[l]), row(k_a[l]),
                   row(r_k[l]), row(lnx_w[l]), row(lnx_b[l]))
    y_att = _swa(z_att, bucket, sinks[l].astype(F32), rel_bias.astype(F32), qnw, knw)

    out = _proj_out(x2, y_rwkv, y_att, w_out_r, w_out_a, 512)
    return out.reshape(b, s, d)
```

```python
import functools
import math

import jax
import jax.numpy as jnp
import numpy as np
from jax import lax
from jax.experimental import pallas as pl
from jax.experimental.pallas import tpu as pltpu

F32 = jnp.float32
BF16 = jnp.bfloat16

D_MODEL = 2048
SEQ = 8192
HEAD_DIM = 64
D_RWKV = 1024
D_ATT = 1024
LORA = 64
N_Q_HEADS = 16
N_KV_HEADS = 2
D_KV = N_KV_HEADS * HEAD_DIM
WINDOW = 128
BLOCK = 128
N_BUCKETS = 32
MAX_EXACT = N_BUCKETS // 2
MAX_DISTANCE = 128
NORM_EPS = 1e-6
LNX_EPS = 64e-5
RWKV_COLS = 4 * D_RWKV + 2 * LORA
ATT_COLS = 2 * D_ATT + 2 * D_KV

LANES = 128
MXU_WIDTH = 256
N_PAIRS = D_RWKV // LANES
CHUNK = 64
RWKV_CHUNKS_PER_GROUP = 4
RWKV_CHUNKS_PER_STEP = 8
NEG_BIG = -1e30
LOG2E = math.log2(math.e)

VMEM_LIMIT = 56 * 1024 * 1024


def _dot(a, b):
    return jnp.dot(a, b, preferred_element_type=F32)


def _dot_nt(a, b):
    return lax.dot_general(a, b, (((1,), (1,)), ((), ())), preferred_element_type=F32)


def _split3(x):
    hi = x.astype(BF16)
    r1 = x - hi.astype(F32)
    mid = r1.astype(BF16)
    lo = (r1 - mid.astype(F32)).astype(BF16)
    return hi, mid, lo


def _seg_sum(x, seg_ones):
    hi = x.astype(BF16)
    lo = (x - hi.astype(F32)).astype(BF16)
    return _dot(hi, seg_ones) + _dot(lo, seg_ones)


def _dot_exact_lhs(m, x):
    hi, mid, lo = _split3(x)
    return _dot(m, hi) + _dot(m, mid) + _dot(m, lo)


def _sigmoid(x):
    return 1.0 / (1.0 + jnp.exp(-x))


def _proj_in_kernel(x_ref, nw_ref, w_ref, *rest, token_shift):
    if token_shift:
        mu_ref, o_ref, wb_ref, prev_ref = rest
    else:
        o_ref, wb_ref = rest

    @pl.when(pl.program_id(1) == 0)
    def _():
        wb_ref[...] = w_ref[...].astype(BF16)
        if token_shift:
            prev_ref[...] = jnp.zeros_like(prev_ref)

    tm, tn = o_ref.shape
    x = x_ref[...]
    h = (x * nw_ref[...]).astype(BF16)
    rs = lax.rsqrt(jnp.mean(x * x, axis=-1, keepdims=True) + NORM_EPS)
    for c0 in range(0, tn, MXU_WIDTH):
        cs = slice(c0, min(c0 + MXU_WIDTH, tn))
        z = _dot(h, wb_ref[:, cs]) * rs
        if token_shift:
            row = lax.broadcasted_iota(jnp.int32, z.shape, 0)
            zprev = jnp.where(row == 0, prev_ref[:, cs], pltpu.roll(z, 1, axis=0))
            prev_ref[:, cs] = z[tm - 1:tm, :]
            z = z + (zprev - z) * mu_ref[:, cs]
        o_ref[:, cs] = z


def _proj_in(x2, norm_w, w, n, tm, tn, name, mu=None):
    s, d = x2.shape
    shift = mu is not None
    in_specs = [
        pl.BlockSpec((tm, d), lambda j, i: (i, 0)),
        pl.BlockSpec((1, d), lambda j, i: (0, 0)),
        pl.BlockSpec((d, tn), lambda j, i: (0, j)),
    ]
    args = [x2, norm_w, w]
    scratch = [pltpu.VMEM((d, tn), BF16)]
    if shift:
        in_specs.append(pl.BlockSpec((1, tn), lambda j, i: (0, j)))
        args.append(mu)
        scratch.append(pltpu.VMEM((1, tn), F32))
    return pl.pallas_call(
        functools.partial(_proj_in_kernel, token_shift=shift),
        grid=(n // tn, s // tm),
        in_specs=in_specs,
        out_specs=pl.BlockSpec((tm, tn), lambda j, i: (i, j)),
        out_shape=jax.ShapeDtypeStruct((s, n), F32),
        scratch_shapes=scratch,
        compiler_params=pltpu.CompilerParams(
            dimension_semantics=("arbitrary", "arbitrary"), vmem_limit_bytes=VMEM_LIMIT),
        name=name,
    )(*args)


def _rwkv_kernel(z_ref, w0_ref, w2_ref, a0_ref, a2_ref, kk_ref, ka_ref, rk_ref,
                 lw_ref, lb_ref, o_ref, gt_ref):
    c = CHUNK
    nch = RWKV_CHUNKS_PER_STEP
    t = c * nch
    i = pl.program_id(0)

    @pl.when(i == 0)
    def _():
        gt_ref[...] = jnp.zeros_like(gt_ref)

    lane = lax.broadcasted_iota(jnp.int32, (c, LANES), 1)
    head0 = lane < HEAD_DIM
    trow = lax.broadcasted_iota(jnp.int32, (c, LANES), 0)
    eye_cat = jnp.where((lane % HEAD_DIM) == trow, 1.0, 0.0)
    t2 = lax.broadcasted_iota(jnp.int32, (c, 2 * LANES), 0)
    s2 = lax.broadcasted_iota(jnp.int32, (c, 2 * LANES), 1) % HEAD_DIM
    strict2 = s2 < t2
    incl2 = s2 <= t2
    r128 = lax.broadcasted_iota(jnp.int32, (LANES, LANES), 0)
    c128 = lax.broadcasted_iota(jnp.int32, (LANES, LANES), 1)
    same_head = (r128 // HEAD_DIM) == (c128 // HEAD_DIM)
    seg_ones = jnp.where(same_head, 1.0, 0.0).astype(BF16)
    zeros_c = jnp.zeros((c, LANES), F32)
    ti = lax.broadcasted_iota(jnp.int32, (c, c), 0)
    si = lax.broadcasted_iota(jnp.int32, (c, c), 1)
    tril_ones = jnp.where(si <= ti, 1.0, 0.0).astype(BF16)

    pairs = range(N_PAIRS)
    psl = [slice(p * LANES, (p + 1) * LANES) for p in pairs]

    def stack2(y):
        return jnp.concatenate([jnp.where(head0, y, 0.0), jnp.where(head0, 0.0, y)], axis=0)

    def catmul(a_cat, y):
        return _dot(a_cat.astype(BF16), stack2(y).astype(BF16))

    seg_mean = jnp.where(same_head, 1.0 / HEAD_DIM, 0.0).astype(BF16)

    def seg_sum_pairs(x, seg=seg_ones):
        y = _seg_sum(jnp.concatenate([x[:, s] for s in psl], axis=0), seg)
        return jnp.concatenate([y[p * c:(p + 1) * c] for p in pairs], axis=1)

    def prepare(ci):
        rs = slice(ci * c, (ci + 1) * c)
        r = z_ref[rs, 0:D_RWKV]
        k = z_ref[rs, D_RWKV:2 * D_RWKV]
        v = z_ref[rs, 2 * D_RWKV:3 * D_RWKV]
        lora_in = z_ref[rs, 4 * D_RWKV:]
        u = w0_ref[...] + _dot(jnp.tanh(lora_in).astype(BF16), w2_ref[...])
        logw = -math.exp(-0.5) * _sigmoid(u)
        av = _sigmoid(a0_ref[...] + _dot(lora_in.astype(BF16), a2_ref[...]))
        logp = _dot_exact_lhs(tril_ones, logw)
        e_p = jnp.exp(logp)
        e_n = 1.0 / e_p
        kmod = k * (1.0 + (av - 1.0) * ka_ref[...])
        kk = k * kk_ref[...]
        kkn = kk * lax.rsqrt(jnp.maximum(seg_sum_pairs(kk * kk), 1e-24))
        return dict(
            r_t=r * e_p, k_t=kmod * e_n, a_t=-kkn * jnp.exp(logp - logw), b_t=kkn * av * e_n, v=v,
            e_last=e_p[c - 1:c, :],
            bonus=seg_sum_pairs(r * kmod * rk_ref[...]) * v)

    def intra(pcs, out):
        per_item = lambda name: [pc[name][:, s] for pc in pcs for s in psl]
        a_t, b_t, r_t, k_t, v_i, e_last = (per_item(n) for n in ("a_t", "b_t", "r_t", "k_t", "v", "e_last"))
        sc = [_dot_nt(jnp.concatenate([a, r], axis=0).astype(BF16),
                      jnp.concatenate([stack2(b), stack2(k)], axis=0).astype(BF16))
              for a, r, b, k in zip(a_t, r_t, b_t, k_t)]
        yield
        l_all = [jnp.where(strict2, x[0:c, :], 0.0) for x in sc]
        a_rbk = [jnp.where(incl2, x[c:, :], 0.0) for x in sc]
        x = [l[:, 0:LANES] for l in l_all]
        tinv = [eye_cat + xi for xi in x]
        x = [catmul(xi, xi) for xi in x]
        yield
        for _ in range(4):
            tx = [catmul(jnp.concatenate([ti_, xi], axis=0), xi) for ti_, xi in zip(tinv, x)]
            tinv = [ti_ + y[0:c] for ti_, y in zip(tinv, tx)]
            x = [y[c:] for y in tx]
            yield
        tinv = [ti_ + catmul(ti_, xi) for ti_, xi in zip(tinv, x)]
        yield
        a_eff = [catmul(ti_, a) for ti_, a in zip(tinv, a_t)]
        yield
        lakv = [catmul(l[:, LANES:], v) for l, v in zip(l_all, v_i)]
        yield
        w_loc = [catmul(ti_, y) for ti_, y in zip(tinv, lakv)]
        yield
        q_eff = [r + catmul(a[:, 0:LANES], ae) for r, a, ae in zip(r_t, a_rbk, a_eff)]
        yield
        o_loc = [_dot(a.astype(BF16), jnp.concatenate([stack2(w), stack2(v)], axis=0).astype(BF16))
                 for a, w, v in zip(a_rbk, w_loc, v_i)]
        yield
        bh = [b * el for b, el in zip(b_t, e_last)]
        kh = [k * el for k, el in zip(k_t, e_last)]
        m_mat = [jnp.where(same_head, _dot(jnp.concatenate([ae, zeros_c], axis=0).T.astype(BF16),
                                           jnp.concatenate([b, zeros_c], axis=0).astype(BF16)), 0.0)
                 for ae, b in zip(a_eff, bh)]
        yield
        n_mat = [jnp.where(same_head, _dot(jnp.concatenate([w, v], axis=0).T.astype(BF16),
                                           jnp.concatenate([b, k], axis=0).astype(BF16)), 0.0)
                 for w, v, b, k in zip(w_loc, v_i, bh, kh)]
        out.update(q_eff=q_eff, o_loc=o_loc, m_mat=m_mat, n_mat=n_mat, e_last=e_last)

    def advance(gts, ic, j):
        sel = lambda name: ic[name][j * N_PAIRS:(j + 1) * N_PAIRS]
        gtb = [g.astype(BF16) for g in gts]
        o = [_dot_nt(q.astype(BF16), g) + ol for q, g, ol in zip(sel("q_eff"), gtb, sel("o_loc"))]
        gts = [g * el + _dot(gb, m.astype(BF16)) + n
               for g, gb, el, m, n in zip(gts, gtb, sel("e_last"), sel("m_mat"), sel("n_mat"))]
        return gts, jnp.concatenate(o, axis=1)

    def finish(ci, o, bonus):
        rs = slice(ci * c, (ci + 1) * c)
        d = o - seg_sum_pairs(o, seg_mean)
        var = seg_sum_pairs(d * d, seg_mean)
        on = d * lax.rsqrt(var + LNX_EPS) * lw_ref[...] + lb_ref[...]
        g = z_ref[rs, 3 * D_RWKV:4 * D_RWKV]
        o_ref[rs, :] = ((on + bonus) * (g * _sigmoid(g))).astype(o_ref.dtype)

    gsz = RWKV_CHUNKS_PER_GROUP
    ngroups = nch // gsz
    state = dict(gts=[gt_ref[p] for p in pairs])
    prepared = {}

    def prepare_group(g):
        for ci in range(g * gsz, (g + 1) * gsz):
            prepared[ci] = prepare(ci)
            yield

    def tail_group(g, ic):
        for j in range(gsz):
            ci = g * gsz + j
            state["gts"], o = advance(state["gts"], ic, j)
            yield
            finish(ci, o, prepared[ci]["bonus"])
            yield

    def drive(main, sides):
        for _ in main:
            for s_ in sides:
                next(s_, None)
        for s_ in sides:
            for _ in s_:
                pass

    drive(prepare_group(0), [])
    results = {}
    for g in range(ngroups):
        results[g] = {}
        sides = []
        if g + 1 < ngroups:
            sides.append(prepare_group(g + 1))
        if g >= 1:
            sides.append(tail_group(g - 1, results[g - 1]))
        drive(intra([prepared[ci] for ci in range(g * gsz, (g + 1) * gsz)], results[g]), sides)
    drive(tail_group(ngroups - 1, results[ngroups - 1]), [])
    for p in pairs:
        gt_ref[p] = state["gts"][p]


def _rwkv(z_rwkv, w0, w2p, a0, a2p, k_k, k_a, r_k, lnx_w, lnx_b):
    s = z_rwkv.shape[0]
    c = CHUNK * RWKV_CHUNKS_PER_STEP
    row = lambda n: pl.BlockSpec((1, n), lambda i: (0, 0))
    return pl.pallas_call(
        _rwkv_kernel,
        grid=(s // c,),
        in_specs=[
            pl.BlockSpec((c, RWKV_COLS), lambda i: (i, 0)),
            row(D_RWKV),
            pl.BlockSpec((2 * LORA, D_RWKV), lambda i: (0, 0)),
            row(D_RWKV),
            pl.BlockSpec((2 * LORA, D_RWKV), lambda i: (0, 0)),
            row(D_RWKV), row(D_RWKV), row(D_RWKV), row(D_RWKV), row(D_RWKV),
        ],
        out_specs=pl.BlockSpec((c, D_RWKV), lambda i: (i, 0)),
        out_shape=jax.ShapeDtypeStruct((s, D_RWKV), BF16),
        scratch_shapes=[
            pltpu.VMEM((N_PAIRS, LANES, LANES), F32),
        ],
        compiler_params=pltpu.CompilerParams(
            dimension_semantics=("arbitrary",), vmem_limit_bytes=VMEM_LIMIT),
        name="rwkv7",
    )(z_rwkv, w0, w2p, a0, a2p, k_k, k_a, r_k, lnx_w, lnx_b)


def _swa_out_kernel(sinks_ref, relb_ref, bucket_ref, q_ref, g_ref, k_ref, v_ref, qnw_ref, knw_ref,
                    x_ref, yr_ref, wr_ref, wa_ref, o_ref, bias_ref, kprev_ref, vprev_ref, yatt_ref):
    i = pl.program_id(0)
    bq = BLOCK
    tm = q_ref.shape[0]

    @pl.when(i == 0)
    def _():
        kprev_ref[...] = jnp.zeros_like(kprev_ref)
        vprev_ref[...] = jnp.zeros_like(vprev_ref)
        yatt_ref[...] = jnp.zeros_like(yatt_ref)
        bucket = bucket_ref[...]
        qi = lax.broadcasted_iota(jnp.int32, (bq, 2 * bq), 0)
        kj = lax.broadcasted_iota(jnp.int32, (bq, 2 * bq), 1)
        dist = bq + qi - kj
        inwin = (dist >= 0) & (dist < WINDOW)
        for h in range(N_Q_HEADS):
            acc = jnp.zeros((bq, 2 * bq), F32)
            for b in range(N_BUCKETS):
                acc = jnp.where(bucket == b, relb_ref[b, h], acc)
            acc = jnp.where(inwin, acc * LOG2E, NEG_BIG)
            bias_ref[0, h] = acc
            bias_ref[1, h] = jnp.where(kj >= bq, acc, NEG_BIG)

    first_step = jnp.where(i == 0, 1, 0)
    lane = lax.broadcasted_iota(jnp.int32, (bq, LANES), 1)
    head0 = lane < HEAD_DIM
    lane2 = lax.broadcasted_iota(jnp.int32, (2 * bq, LANES), 1)
    head0_2 = lane2 < HEAD_DIM
    r128 = lax.broadcasted_iota(jnp.int32, (LANES, LANES), 0)
    c128 = lax.broadcasted_iota(jnp.int32, (LANES, LANES), 1)
    seg_mean = jnp.where((r128 // HEAD_DIM) == (c128 // HEAD_DIM), 1.0 / HEAD_DIM, 0.0).astype(BF16)
    sink2 = [sinks_ref[h] * LOG2E for h in range(N_Q_HEADS)]
    pairs = range(N_PAIRS)
    heads = range(N_Q_HEADS)
    psl = [slice(p * LANES, (p + 1) * LANES) for p in pairs]
    kvh = [(2 * p) // (N_Q_HEADS // N_KV_HEADS) for p in pairs]

    def rms_heads(t, w):
        return t * lax.rsqrt(_seg_sum(t * t, seg_mean) + NORM_EPS) * w

    ya_prev = yatt_ref[...]
    yr_prev = yr_ref[...]

    def project_previous():
        for c0 in range(0, D_MODEL, MXU_WIDTH):
            cs = slice(c0, c0 + MXU_WIDTH)
            o_ref[:, cs] = x_ref[:, cs] + _dot(yr_prev, wr_ref[:, cs]) + _dot(ya_prev, wa_ref[:, cs])
            yield

    def attend():
        kp, vp = kprev_ref[...], vprev_ref[...]
        for b in range(tm // bq):
            rb = slice(b * bq, (b + 1) * bq)
            first = first_step if b == 0 else 0
            kn = rms_heads(k_ref[rb, :], knw_ref[...])
            vc = v_ref[rb, :]
            kcat = jnp.concatenate([kp, kn], axis=0)
            vcat = jnp.concatenate([vp, vc], axis=0)
            kp, vp = kn, vc
            krol = pltpu.roll(kcat, HEAD_DIM, axis=1)
            vrol = pltpu.roll(vcat, HEAD_DIM, axis=1)
            kdup = [jnp.where(head0_2, kcat, krol).astype(BF16), jnp.where(head0_2, krol, kcat).astype(BF16)]
            vdup = [jnp.where(head0_2, vcat, vrol).astype(BF16), jnp.where(head0_2, vrol, vcat).astype(BF16)]
            yield
            qn = [rms_heads(q_ref[rb, s], qnw_ref[:, s]) for s in psl]
            yield
            q2 = [jnp.concatenate([jnp.where(head0, q, 0.0), jnp.where(head0, 0.0, q)], axis=0).astype(BF16)
                  for q in qn]
            lg2 = [_dot_nt(q, kdup[kv]) for q, kv in zip(q2, kvh)]
            yield
            lg = [lg2[h // 2][(h % 2) * bq:(h % 2 + 1) * bq, :] + bias_ref[first, h] for h in heads]
            yield
            m = [jnp.maximum(jnp.max(l, axis=-1, keepdims=True), sink2[h]) for h, l in zip(heads, lg)]
            yield
            e = [jnp.exp2(l - mm) for l, mm in zip(lg, m)]
            yield
            denom = [jnp.sum(ee, axis=-1, keepdims=True) + jnp.exp2(sink2[h] - mm)
                     for h, ee, mm in zip(heads, e, m)]
            yield
            pv = [_dot(ee.astype(BF16), vdup[kvh[h // 2]]) for h, ee in zip(heads, e)]
            yield
            outs = [x / d for x, d in zip(pv, denom)]
            for p in pairs:
                o = jnp.where(head0, outs[2 * p], outs[2 * p + 1])
                g = g_ref[rb, psl[p]]
                yatt_ref[rb, psl[p]] = (o * (g * _sigmoid(g))).astype(yatt_ref.dtype)
            yield
        kprev_ref[...] = kp
        vprev_ref[...] = vp

    proj = project_previous()
    n_levels = 9 * (tm // bq)
    every = n_levels // (D_MODEL // MXU_WIDTH)
    for n, _ in enumerate(attend()):
        if n % every == every - 1:
            next(proj, None)
    for _ in proj:
        pass


def _swa_out(z_att, bucket, sinks, rel_bias, qnw, knw, x2, y_rwkv, w_r, w_a, tm):
    s, d = x2.shape
    bq = BLOCK
    nb = s // tm
    smem = pl.BlockSpec(memory_space=pltpu.SMEM)
    cur = lambda i: jnp.minimum(i, nb - 1)
    prev = lambda i: jnp.maximum(i - 1, 0)
    kv_col = 2 * D_ATT // D_KV
    return pl.pallas_call(
        _swa_out_kernel,
        grid=(nb + 1,),
        in_specs=[
            smem, smem,
            pl.BlockSpec((bq, 2 * bq), lambda i: (0, 0)),
            pl.BlockSpec((tm, D_ATT), lambda i: (cur(i), 0)),
            pl.BlockSpec((tm, D_ATT), lambda i: (cur(i), 1)),
            pl.BlockSpec((tm, D_KV), lambda i: (cur(i), kv_col)),
            pl.BlockSpec((tm, D_KV), lambda i: (cur(i), kv_col + 1)),
            pl.BlockSpec((1, D_ATT), lambda i: (0, 0)),
            pl.BlockSpec((1, D_KV), lambda i: (0, 0)),
            pl.BlockSpec((tm, d), lambda i: (prev(i), 0)),
            pl.BlockSpec((tm, D_RWKV), lambda i: (prev(i), 0)),
            pl.BlockSpec((D_RWKV, d), lambda i: (0, 0)),
            pl.BlockSpec((D_ATT, d), lambda i: (0, 0)),
        ],
        out_specs=pl.BlockSpec((tm, d), lambda i: (prev(i), 0)),
        out_shape=jax.ShapeDtypeStruct((s, d), F32),
        scratch_shapes=[
            pltpu.VMEM((2, N_Q_HEADS, bq, 2 * bq), F32),
            pltpu.VMEM((bq, D_KV), F32),
            pltpu.VMEM((bq, D_KV), F32),
            pltpu.VMEM((tm, D_ATT), BF16),
        ],
        compiler_params=pltpu.CompilerParams(
            dimension_semantics=("arbitrary",), vmem_limit_bytes=VMEM_LIMIT),
        name="swa_out",
    )(sinks, rel_bias, bucket, z_att, z_att, z_att, z_att, qnw, knw, x2, y_rwkv, w_r, w_a)


def _t5_bucket_table():
    dist = BLOCK + np.arange(BLOCK)[:, None] - np.arange(2 * BLOCK)[None, :]
    n = np.maximum(dist, 0)
    nf = np.maximum(n, 1).astype(np.float64)
    large = MAX_EXACT + (np.log(nf / MAX_EXACT) / math.log(MAX_DISTANCE / MAX_EXACT)
                         * (N_BUCKETS - MAX_EXACT)).astype(np.int32)
    large = np.minimum(large, N_BUCKETS - 1)
    return np.where(n < MAX_EXACT, n, large).astype(np.int32)


def kernel(x, norm_w, w_in, w_out, mu_rwkv, w0, w2, a0, a2, k_k, k_a, r_k, lnx_w, lnx_b,
           q_norm_w, k_norm_w, sinks, rel_bias):
    b, s, d = x.shape
    assert (b, s, d) == (1, SEQ, D_MODEL) and norm_w.shape[0] == 1
    x2 = x.reshape(s, d)
    l = 0
    row = lambda t: t.reshape(1, -1).astype(F32)

    w_in_l = w_in[l]
    wa = w_in_l[:, RWKV_COLS:]
    w_att = jnp.concatenate([wa[:, :D_ATT], wa[:, D_ATT + 2 * D_KV:], wa[:, D_ATT:D_ATT + 2 * D_KV]], axis=1)
    zeros_l = jnp.zeros((LORA, D_RWKV), F32)
    w2p = jnp.concatenate([w2[l], zeros_l], axis=0).astype(BF16)
    a2p = jnp.concatenate([zeros_l, a2[l]], axis=0).astype(BF16)
    w_out_r = w_out[l][:D_RWKV].astype(BF16)
    w_out_a = w_out[l][D_RWKV:].astype(BF16)
    qnw = jnp.tile(q_norm_w[l] * (HEAD_DIM ** -0.5 * LOG2E), N_Q_HEADS).reshape(1, D_ATT)
    knw = jnp.tile(k_norm_w[l], N_KV_HEADS).reshape(1, D_KV)
    bucket = jnp.asarray(_t5_bucket_table())

    nw = row(norm_w[l])
    z_rwkv = _proj_in(x2, nw, w_in_l, RWKV_COLS, 512, 1408, "proj_in_rwkv", mu=row(mu_rwkv[l]))
    z_att = _proj_in(x2, nw, w_att, ATT_COLS, 512, 1152, "proj_in_att")

    y_rwkv = _rwkv(z_rwkv, row(w0[l]), w2p, row(a0[l]), a2p, row(k_k[l]), row(k_a[l]),
                   row(r_k[l]), row(lnx_w[l]), row(lnx_b[l]))
    out = _swa_out(z_att, bucket, sinks[l].astype(F32), rel_bias.astype(F32), qnw, knw,
                   x2, y_rwkv, w_out_r, w_out_a, 512)
    return out.reshape(b, s, d)
```

```python
import functools
import math

import jax
import jax.numpy as jnp
import numpy as np
from jax import lax
from jax.experimental import pallas as pl
from jax.experimental.pallas import tpu as pltpu

F32 = jnp.float32
BF16 = jnp.bfloat16

D_MODEL = 2048
SEQ = 8192
HEAD_DIM = 64
D_RWKV = 1024
D_ATT = 1024
LORA = 64
N_Q_HEADS = 16
N_KV_HEADS = 2
D_KV = N_KV_HEADS * HEAD_DIM
WINDOW = 128
BLOCK = 128
N_BUCKETS = 32
MAX_EXACT = N_BUCKETS // 2
MAX_DISTANCE = 128
NORM_EPS = 1e-6
LNX_EPS = 64e-5
RWKV_COLS = 4 * D_RWKV + 2 * LORA
ATT_COLS = 2 * D_ATT + 2 * D_KV

LANES = 128
MXU_WIDTH = 256
N_PAIRS = D_RWKV // LANES
CHUNK = 64
RWKV_CHUNKS_PER_GROUP = 4
RWKV_CHUNKS_PER_STEP = 8
NEG_BIG = -1e30
LOG2E = math.log2(math.e)

VMEM_LIMIT = 56 * 1024 * 1024


def _dot(a, b):
    return jnp.dot(a, b, preferred_element_type=F32)


def _dot_nt(a, b):
    return lax.dot_general(a, b, (((1,), (1,)), ((), ())), preferred_element_type=F32)


def _split3(x):
    hi = x.astype(BF16)
    r1 = x - hi.astype(F32)
    mid = r1.astype(BF16)
    lo = (r1 - mid.astype(F32)).astype(BF16)
    return hi, mid, lo


def _seg_sum(x, seg_ones):
    hi = x.astype(BF16)
    lo = (x - hi.astype(F32)).astype(BF16)
    return _dot(hi, seg_ones) + _dot(lo, seg_ones)


def _dot_exact_lhs(m, x):
    hi, mid, lo = _split3(x)
    return _dot(m, hi) + _dot(m, mid) + _dot(m, lo)


def _sigmoid(x):
    return 1.0 / (1.0 + jnp.exp(-x))


def _proj_in_kernel(x_ref, nw_ref, w_ref, *rest, token_shift):
    if token_shift:
        mu_ref, o_ref, wb_ref, prev_ref = rest
    else:
        o_ref, wb_ref = rest

    @pl.when(pl.program_id(1) == 0)
    def _():
        wb_ref[...] = w_ref[...].astype(BF16)
        if token_shift:
            prev_ref[...] = jnp.zeros_like(prev_ref)

    tm, tn = o_ref.shape
    x = x_ref[...]
    h = (x * nw_ref[...]).astype(BF16)
    rs = lax.rsqrt(jnp.mean(x * x, axis=-1, keepdims=True) + NORM_EPS)
    for c0 in range(0, tn, MXU_WIDTH):
        cs = slice(c0, min(c0 + MXU_WIDTH, tn))
        z = _dot(h, wb_ref[:, cs]) * rs
        if token_shift:
            row = lax.broadcasted_iota(jnp.int32, z.shape, 0)
            zprev = jnp.where(row == 0, prev_ref[:, cs], pltpu.roll(z, 1, axis=0))
            prev_ref[:, cs] = z[tm - 1:tm, :]
            z = z + (zprev - z) * mu_ref[:, cs]
        o_ref[:, cs] = z


def _proj_in(x2, norm_w, w, n, tm, tn, name, mu=None):
    s, d = x2.shape
    shift = mu is not None
    in_specs = [
        pl.BlockSpec((tm, d), lambda j, i: (i, 0)),
        pl.BlockSpec((1, d), lambda j, i: (0, 0)),
        pl.BlockSpec((d, tn), lambda j, i: (0, j)),
    ]
    args = [x2, norm_w, w]
    scratch = [pltpu.VMEM((d, tn), BF16)]
    if shift:
        in_specs.append(pl.BlockSpec((1, tn), lambda j, i: (0, j)))
        args.append(mu)
        scratch.append(pltpu.VMEM((1, tn), F32))
    return pl.pallas_call(
        functools.partial(_proj_in_kernel, token_shift=shift),
        grid=(n // tn, s // tm),
        in_specs=in_specs,
        out_specs=pl.BlockSpec((tm, tn), lambda j, i: (i, j)),
        out_shape=jax.ShapeDtypeStruct((s, n), F32),
        scratch_shapes=scratch,
        compiler_params=pltpu.CompilerParams(
            dimension_semantics=("arbitrary", "arbitrary"), vmem_limit_bytes=VMEM_LIMIT),
        name=name,
    )(*args)


def _att_prep_kernel(x_ref, nw_ref, w_ref, r_ref, k_ref, v_ref, lora_ref, w0_ref, w2_ref, a0_ref, a2_ref,
                     kk_ref, ka_ref, rk_ref,
                     o_ref, rt_ref, kt_ref, at_ref, bt_ref, vb_ref, bonus_ref, el_ref, wb_ref):
    c = CHUNK

    @pl.when(pl.program_id(1) == 0)
    def _():
        wb_ref[...] = w_ref[...].astype(BF16)

    tm, tn = o_ref.shape
    x = x_ref[...]
    h = (x * nw_ref[...]).astype(BF16)
    rs = lax.rsqrt(jnp.mean(x * x, axis=-1, keepdims=True) + NORM_EPS)

    def project():
        for c0 in range(0, tn, MXU_WIDTH):
            cs = slice(c0, min(c0 + MXU_WIDTH, tn))
            o_ref[:, cs] = _dot(h, wb_ref[:, cs]) * rs
            yield

    r128 = lax.broadcasted_iota(jnp.int32, (LANES, LANES), 0)
    c128 = lax.broadcasted_iota(jnp.int32, (LANES, LANES), 1)
    seg_ones = jnp.where((r128 // HEAD_DIM) == (c128 // HEAD_DIM), 1.0, 0.0).astype(BF16)
    ti = lax.broadcasted_iota(jnp.int32, (c, c), 0)
    si = lax.broadcasted_iota(jnp.int32, (c, c), 1)
    tril_ones = jnp.where(si <= ti, 1.0, 0.0).astype(BF16)
    pairs = range(N_PAIRS)
    psl = [slice(p * LANES, (p + 1) * LANES) for p in pairs]

    def seg_sum_pairs(y):
        z = _seg_sum(jnp.concatenate([y[:, s] for s in psl], axis=0), seg_ones)
        return jnp.concatenate([z[p * c:(p + 1) * c] for p in pairs], axis=1)

    def prepare():
        lora_in = lora_ref[...]
        u_lin = _dot(jnp.tanh(lora_in).astype(BF16), w2_ref[...])
        a_lin = _dot(lora_in.astype(BF16), a2_ref[...])
        yield
        for ci in range(r_ref.shape[0] // c):
            rows = slice(ci * c, (ci + 1) * c)
            logw = -math.exp(-0.5) * _sigmoid(w0_ref[...] + u_lin[rows, :])
            av = _sigmoid(a0_ref[...] + a_lin[rows, :])
            logp = _dot_exact_lhs(tril_ones, logw)
            e_p = jnp.exp(logp)
            e_n = 1.0 / e_p
            el_ref[0, ci:ci + 1, :] = e_p[c - 1:c, :]
            yield
            r = r_ref[rows, :]
            k = k_ref[rows, :]
            v = v_ref[rows, :]
            kmod = k * (1.0 + (av - 1.0) * ka_ref[...])
            rt_ref[rows, :] = (r * e_p).astype(rt_ref.dtype)
            kt_ref[rows, :] = (kmod * e_n).astype(kt_ref.dtype)
            vb_ref[rows, :] = v.astype(vb_ref.dtype)
            bonus_ref[rows, :] = seg_sum_pairs(r * kmod * rk_ref[...]) * v
            yield
            kk = k * kk_ref[...]
            kkn = kk * lax.rsqrt(jnp.maximum(seg_sum_pairs(kk * kk), 1e-24))
            at_ref[rows, :] = (-kkn * jnp.exp(logp - logw)).astype(at_ref.dtype)
            bt_ref[rows, :] = (kkn * av * e_n).astype(bt_ref.dtype)
            yield

    proj = project()
    n_slices = 1 + 3 * (r_ref.shape[0] // c)
    every = max(1, n_slices // (-(-tn // MXU_WIDTH)))
    for n, _ in enumerate(prepare()):
        if n % every == every - 1:
            next(proj, None)
    for _ in proj:
        pass


def _att_prep(x2, norm_w, w_att, z_rwkv, w0, w2p, a0, a2p, k_k, k_a, r_k, tm, tn, tp):
    s, d = x2.shape
    n = w_att.shape[1]
    nj, ni = n // tn, s // tm
    assert nj * ni * tp == s and tp % CHUNK == 0
    blk = lambda j, i: j * ni + i
    row = pl.BlockSpec((1, D_RWKV), lambda j, i: (0, 0))
    lora_w = pl.BlockSpec((2 * LORA, D_RWKV), lambda j, i: (0, 0))
    zcol = lambda cb: pl.BlockSpec((tp, D_RWKV), lambda j, i: (blk(j, i), cb))
    tok = pl.BlockSpec((tp, D_RWKV), lambda j, i: (blk(j, i), 0))
    sds = lambda dt: jax.ShapeDtypeStruct((s, D_RWKV), dt)
    return pl.pallas_call(
        _att_prep_kernel,
        grid=(nj, ni),
        in_specs=[
            pl.BlockSpec((tm, d), lambda j, i: (i, 0)),
            pl.BlockSpec((1, d), lambda j, i: (0, 0)),
            pl.BlockSpec((d, tn), lambda j, i: (0, j)),
            zcol(0), zcol(1), zcol(2),
            pl.BlockSpec((tp, 2 * LORA), lambda j, i: (blk(j, i), 4 * D_RWKV // (2 * LORA))),
            row, lora_w, row, lora_w, row, row, row,
        ],
        out_specs=[
            pl.BlockSpec((tm, tn), lambda j, i: (i, j)),
            tok, tok, tok, tok, tok, tok,
            pl.BlockSpec((1, tp // CHUNK, D_RWKV), lambda j, i: (blk(j, i), 0, 0)),
        ],
        out_shape=[
            jax.ShapeDtypeStruct((s, n), F32),
            sds(BF16), sds(BF16), sds(BF16), sds(BF16), sds(BF16), sds(F32),
            jax.ShapeDtypeStruct((s // tp, tp // CHUNK, D_RWKV), F32),
        ],
        scratch_shapes=[pltpu.VMEM((d, tn), BF16)],
        compiler_params=pltpu.CompilerParams(
            dimension_semantics=("arbitrary", "arbitrary"), vmem_limit_bytes=VMEM_LIMIT),
        name="proj_att_rwkv_prep",
    )(x2, norm_w, w_att, z_rwkv, z_rwkv, z_rwkv, z_rwkv, w0, w2p, a0, a2p, k_k, k_a, r_k)


def _rwkv_kernel(rt_ref, kt_ref, at_ref, bt_ref, v_ref, bonus_ref, el_ref, g_ref,
                 lw_ref, lb_ref, o_ref, gt_ref):
    c = CHUNK
    nch = RWKV_CHUNKS_PER_STEP
    i = pl.program_id(0)

    @pl.when(i == 0)
    def _():
        gt_ref[...] = jnp.zeros_like(gt_ref)

    lane = lax.broadcasted_iota(jnp.int32, (c, LANES), 1)
    head0 = lane < HEAD_DIM
    trow = lax.broadcasted_iota(jnp.int32, (c, LANES), 0)
    eye_cat = jnp.where((lane % HEAD_DIM) == trow, 1.0, 0.0)
    t2 = lax.broadcasted_iota(jnp.int32, (c, 2 * LANES), 0)
    s2 = lax.broadcasted_iota(jnp.int32, (c, 2 * LANES), 1) % HEAD_DIM
    strict2 = s2 < t2
    incl2 = s2 <= t2
    r128 = lax.broadcasted_iota(jnp.int32, (LANES, LANES), 0)
    c128 = lax.broadcasted_iota(jnp.int32, (LANES, LANES), 1)
    same_head = (r128 // HEAD_DIM) == (c128 // HEAD_DIM)
    zeros_c = jnp.zeros((c, LANES), F32)

    pairs = range(N_PAIRS)
    psl = [slice(p * LANES, (p + 1) * LANES) for p in pairs]

    def stack2(y):
        return jnp.concatenate([jnp.where(head0, y, 0.0), jnp.where(head0, 0.0, y)], axis=0)

    def catmul(a_cat, y):
        return _dot(a_cat.astype(BF16), stack2(y).astype(BF16))

    seg_mean = jnp.where(same_head, 1.0 / HEAD_DIM, 0.0).astype(BF16)

    def seg_sum_pairs(x, seg):
        y = _seg_sum(jnp.concatenate([x[:, s] for s in psl], axis=0), seg)
        return jnp.concatenate([y[p * c:(p + 1) * c] for p in pairs], axis=1)

    el_chunks = el_ref.shape[1]

    def intra(chunks, out):
        per_item = lambda ref: [ref[ci * c:(ci + 1) * c, s].astype(F32) for ci in chunks for s in psl]
        a_t, b_t, r_t, k_t, v_i = (per_item(ref) for ref in (at_ref, bt_ref, rt_ref, kt_ref, v_ref))
        e_last = [el_ref[ci // el_chunks, ci % el_chunks:ci % el_chunks + 1, s]
                  for ci in chunks for s in psl]
        sc = [_dot_nt(jnp.concatenate([a, r], axis=0).astype(BF16),
                      jnp.concatenate([stack2(b), stack2(k)], axis=0).astype(BF16))
              for a, r, b, k in zip(a_t, r_t, b_t, k_t)]
        yield
        l_all = [jnp.where(strict2, x[0:c, :], 0.0) for x in sc]
        a_rbk = [jnp.where(incl2, x[c:, :], 0.0) for x in sc]
        x = [l[:, 0:LANES] for l in l_all]
        tinv = [eye_cat + xi for xi in x]
        x = [catmul(xi, xi) for xi in x]
        yield
        for _ in range(4):
            tx = [catmul(jnp.concatenate([ti_, xi], axis=0), xi) for ti_, xi in zip(tinv, x)]
            tinv = [ti_ + y[0:c] for ti_, y in zip(tinv, tx)]
            x = [y[c:] for y in tx]
            yield
        tinv = [ti_ + catmul(ti_, xi) for ti_, xi in zip(tinv, x)]
        yield
        a_eff = [catmul(ti_, a) for ti_, a in zip(tinv, a_t)]
        yield
        lakv = [catmul(l[:, LANES:], v) for l, v in zip(l_all, v_i)]
        yield
        w_loc = [catmul(ti_, y) for ti_, y in zip(tinv, lakv)]
        yield
        q_eff = [r + catmul(a[:, 0:LANES], ae) for r, a, ae in zip(r_t, a_rbk, a_eff)]
        yield
        o_loc = [_dot(a.astype(BF16), jnp.concatenate([stack2(w), stack2(v)], axis=0).astype(BF16))
                 for a, w, v in zip(a_rbk, w_loc, v_i)]
        yield
        bh = [b * el for b, el in zip(b_t, e_last)]
        kh = [k * el for k, el in zip(k_t, e_last)]
        m_mat = [jnp.where(same_head, _dot(jnp.concatenate([ae, zeros_c], axis=0).T.astype(BF16),
                                           jnp.concatenate([b, zeros_c], axis=0).astype(BF16)), 0.0)
                 for ae, b in zip(a_eff, bh)]
        yield
        n_mat = [jnp.where(same_head, _dot(jnp.concatenate([w, v], axis=0).T.astype(BF16),
                                           jnp.concatenate([b, k], axis=0).astype(BF16)), 0.0)
                 for w, v, b, k in zip(w_loc, v_i, bh, kh)]
        out.update(q_eff=q_eff, o_loc=o_loc, m_mat=m_mat, n_mat=n_mat, e_last=e_last)

    def advance(gts, ic, j):
        sel = lambda name: ic[name][j * N_PAIRS:(j + 1) * N_PAIRS]
        gtb = [g.astype(BF16) for g in gts]
        o = [_dot_nt(q.astype(BF16), g) + ol for q, g, ol in zip(sel("q_eff"), gtb, sel("o_loc"))]
        gts = [g * el + _dot(gb, m.astype(BF16)) + n
               for g, gb, el, m, n in zip(gts, gtb, sel("e_last"), sel("m_mat"), sel("n_mat"))]
        return gts, jnp.concatenate(o, axis=1)

    def finish(ci, o):
        rs = slice(ci * c, (ci + 1) * c)
        d = o - seg_sum_pairs(o, seg_mean)
        var = seg_sum_pairs(d * d, seg_mean)
        on = d * lax.rsqrt(var + LNX_EPS) * lw_ref[...] + lb_ref[...]
        g = g_ref[rs, :]
        o_ref[rs, :] = ((on + bonus_ref[rs, :]) * (g * _sigmoid(g))).astype(o_ref.dtype)

    gsz = RWKV_CHUNKS_PER_GROUP
    ngroups = nch // gsz
    state = dict(gts=[gt_ref[p] for p in pairs])

    def tail_group(g, ic):
        for j in range(gsz):
            state["gts"], o = advance(state["gts"], ic, j)
            yield
            finish(g * gsz + j, o)
            yield

    def drive(main, sides):
        for _ in main:
            for s_ in sides:
                next(s_, None)
        for s_ in sides:
            for _ in s_:
                pass

    results = {}
    for g in range(ngroups):
        results[g] = {}
        sides = [tail_group(g - 1, results[g - 1])] if g >= 1 else []
        drive(intra(range(g * gsz, (g + 1) * gsz), results[g]), sides)
    drive(tail_group(ngroups - 1, results[ngroups - 1]), [])
    for p in pairs:
        gt_ref[p] = state["gts"][p]


def _rwkv(prep, z_rwkv, lnx_w, lnx_b):
    s = z_rwkv.shape[0]
    t = CHUNK * RWKV_CHUNKS_PER_STEP
    el = prep[-1]
    blocks_per_step = RWKV_CHUNKS_PER_STEP // el.shape[1]
    tok = pl.BlockSpec((t, D_RWKV), lambda i: (i, 0))
    row = pl.BlockSpec((1, D_RWKV), lambda i: (0, 0))
    return pl.pallas_call(
        _rwkv_kernel,
        grid=(s // t,),
        in_specs=[
            tok, tok, tok, tok, tok, tok,
            pl.BlockSpec((blocks_per_step, el.shape[1], D_RWKV), lambda i: (i, 0, 0)),
            pl.BlockSpec((t, D_RWKV), lambda i: (i, 3)),
            row, row,
        ],
        out_specs=tok,
        out_shape=jax.ShapeDtypeStruct((s, D_RWKV), BF16),
        scratch_shapes=[
            pltpu.VMEM((N_PAIRS, LANES, LANES), F32),
        ],
        compiler_params=pltpu.CompilerParams(
            dimension_semantics=("arbitrary",), vmem_limit_bytes=VMEM_LIMIT),
        name="rwkv7",
    )(*prep, z_rwkv, lnx_w, lnx_b)


def _swa_out_kernel(sinks_ref, relb_ref, bucket_ref, q_ref, g_ref, k_ref, v_ref, qnw_ref, knw_ref,
                    x_ref, yr_ref, wr_ref, wa_ref, o_ref, bias_ref, kprev_ref, vprev_ref, yatt_ref):
    i = pl.program_id(0)
    bq = BLOCK
    tm = q_ref.shape[0]

    @pl.when(i == 0)
    def _():
        kprev_ref[...] = jnp.zeros_like(kprev_ref)
        vprev_ref[...] = jnp.zeros_like(vprev_ref)
        yatt_ref[...] = jnp.zeros_like(yatt_ref)
        bucket = bucket_ref[...]
        qi = lax.broadcasted_iota(jnp.int32, (bq, 2 * bq), 0)
        kj = lax.broadcasted_iota(jnp.int32, (bq, 2 * bq), 1)
        dist = bq + qi - kj
        inwin = (dist >= 0) & (dist < WINDOW)
        for h in range(N_Q_HEADS):
            acc = jnp.zeros((bq, 2 * bq), F32)
            for b in range(N_BUCKETS):
                acc = jnp.where(bucket == b, relb_ref[b, h], acc)
            acc = jnp.where(inwin, acc * LOG2E, NEG_BIG)
            bias_ref[0, h] = acc
            bias_ref[1, h] = jnp.where(kj >= bq, acc, NEG_BIG)

    first_step = jnp.where(i == 0, 1, 0)
    lane = lax.broadcasted_iota(jnp.int32, (bq, LANES), 1)
    head0 = lane < HEAD_DIM
    lane2 = lax.broadcasted_iota(jnp.int32, (2 * bq, LANES), 1)
    head0_2 = lane2 < HEAD_DIM
    r128 = lax.broadcasted_iota(jnp.int32, (LANES, LANES), 0)
    c128 = lax.broadcasted_iota(jnp.int32, (LANES, LANES), 1)
    seg_mean = jnp.where((r128 // HEAD_DIM) == (c128 // HEAD_DIM), 1.0 / HEAD_DIM, 0.0).astype(BF16)
    sink2 = [sinks_ref[h] * LOG2E for h in range(N_Q_HEADS)]
    pairs = range(N_PAIRS)
    heads = range(N_Q_HEADS)
    psl = [slice(p * LANES, (p + 1) * LANES) for p in pairs]
    kvh = [(2 * p) // (N_Q_HEADS // N_KV_HEADS) for p in pairs]

    def rms_heads(t, w):
        return t * lax.rsqrt(_seg_sum(t * t, seg_mean) + NORM_EPS) * w

    ya_prev = yatt_ref[...]
    yr_prev = yr_ref[...]

    def project_previous():
        for c0 in range(0, D_MODEL, MXU_WIDTH):
            cs = slice(c0, c0 + MXU_WIDTH)
            o_ref[:, cs] = x_ref[:, cs] + _dot(yr_prev, wr_ref[:, cs]) + _dot(ya_prev, wa_ref[:, cs])
            yield

    def attend():
        kp, vp = kprev_ref[...], vprev_ref[...]
        for b in range(tm // bq):
            rb = slice(b * bq, (b + 1) * bq)
            first = first_step if b == 0 else 0
            kn = rms_heads(k_ref[rb, :], knw_ref[...])
            vc = v_ref[rb, :]
            kcat = jnp.concatenate([kp, kn], axis=0)
            vcat = jnp.concatenate([vp, vc], axis=0)
            kp, vp = kn, vc
            krol = pltpu.roll(kcat, HEAD_DIM, axis=1)
            vrol = pltpu.roll(vcat, HEAD_DIM, axis=1)
            kdup = [jnp.where(head0_2, kcat, krol).astype(BF16), jnp.where(head0_2, krol, kcat).astype(BF16)]
            vdup = [jnp.where(head0_2, vcat, vrol).astype(BF16), jnp.where(head0_2, vrol, vcat).astype(BF16)]
            yield
            qn = [rms_heads(q_ref[rb, s], qnw_ref[:, s]) for s in psl]
            yield
            q2 = [jnp.concatenate([jnp.where(head0, q, 0.0), jnp.where(head0, 0.0, q)], axis=0).astype(BF16)
                  for q in qn]
            lg2 = [_dot_nt(q, kdup[kv]) for q, kv in zip(q2, kvh)]
            yield
            lg = [lg2[h // 2][(h % 2) * bq:(h % 2 + 1) * bq, :] + bias_ref[first, h] for h in heads]
            yield
            m = [jnp.maximum(jnp.max(l, axis=-1, keepdims=True), sink2[h]) for h, l in zip(heads, lg)]
            yield
            e = [jnp.exp2(l - mm) for l, mm in zip(lg, m)]
            yield
            denom = [jnp.sum(ee, axis=-1, keepdims=True) + jnp.exp2(sink2[h] - mm)
                     for h, ee, mm in zip(heads, e, m)]
            yield
            pv = [_dot(ee.astype(BF16), vdup[kvh[h // 2]]) for h, ee in zip(heads, e)]
            yield
            outs = [x / d for x, d in zip(pv, denom)]
            for p in pairs:
                o = jnp.where(head0, outs[2 * p], outs[2 * p + 1])
                g = g_ref[rb, psl[p]]
                yatt_ref[rb, psl[p]] = (o * (g * _sigmoid(g))).astype(yatt_ref.dtype)
            yield
        kprev_ref[...] = kp
        vprev_ref[...] = vp

    proj = project_previous()
    n_levels = 9 * (tm // bq)
    every = n_levels // (D_MODEL // MXU_WIDTH)
    for n, _ in enumerate(attend()):
        if n % every == every - 1:
            next(proj, None)
    for _ in proj:
        pass


def _swa_out(z_att, bucket, sinks, rel_bias, qnw, knw, x2, y_rwkv, w_r, w_a, tm):
    s, d = x2.shape
    bq = BLOCK
    nb = s // tm
    smem = pl.BlockSpec(memory_space=pltpu.SMEM)
    cur = lambda i: jnp.minimum(i, nb - 1)
    prev = lambda i: jnp.maximum(i - 1, 0)
    kv_col = 2 * D_ATT // D_KV
    return pl.pallas_call(
        _swa_out_kernel,
        grid=(nb + 1,),
        in_specs=[
            smem, smem,
            pl.BlockSpec((bq, 2 * bq), lambda i: (0, 0)),
            pl.BlockSpec((tm, D_ATT), lambda i: (cur(i), 0)),
            pl.BlockSpec((tm, D_ATT), lambda i: (cur(i), 1)),
            pl.BlockSpec((tm, D_KV), lambda i: (cur(i), kv_col)),
            pl.BlockSpec((tm, D_KV), lambda i: (cur(i), kv_col + 1)),
            pl.BlockSpec((1, D_ATT), lambda i: (0, 0)),
            pl.BlockSpec((1, D_KV), lambda i: (0, 0)),
            pl.BlockSpec((tm, d), lambda i: (prev(i), 0)),
            pl.BlockSpec((tm, D_RWKV), lambda i: (prev(i), 0)),
            pl.BlockSpec((D_RWKV, d), lambda i: (0, 0)),
            pl.BlockSpec((D_ATT, d), lambda i: (0, 0)),
        ],
        out_specs=pl.BlockSpec((tm, d), lambda i: (prev(i), 0)),
        out_shape=jax.ShapeDtypeStruct((s, d), F32),
        scratch_shapes=[
            pltpu.VMEM((2, N_Q_HEADS, bq, 2 * bq), F32),
            pltpu.VMEM((bq, D_KV), F32),
            pltpu.VMEM((bq, D_KV), F32),
            pltpu.VMEM((tm, D_ATT), BF16),
        ],
        compiler_params=pltpu.CompilerParams(
            dimension_semantics=("arbitrary",), vmem_limit_bytes=VMEM_LIMIT),
        name="swa_out",
    )(sinks, rel_bias, bucket, z_att, z_att, z_att, z_att, qnw, knw, x2, y_rwkv, w_r, w_a)


def _t5_bucket_table():
    dist = BLOCK + np.arange(BLOCK)[:, None] - np.arange(2 * BLOCK)[None, :]
    n = np.maximum(dist, 0)
    nf = np.maximum(n, 1).astype(np.float64)
    large = MAX_EXACT + (np.log(nf / MAX_EXACT) / math.log(MAX_DISTANCE / MAX_EXACT)
                         * (N_BUCKETS - MAX_EXACT)).astype(np.int32)
    large = np.minimum(large, N_BUCKETS - 1)
    return np.where(n < MAX_EXACT, n, large).astype(np.int32)


def kernel(x, norm_w, w_in, w_out, mu_rwkv, w0, w2, a0, a2, k_k, k_a, r_k, lnx_w, lnx_b,
           q_norm_w, k_norm_w, sinks, rel_bias):
    b, s, d = x.shape
    assert (b, s, d) == (1, SEQ, D_MODEL) and norm_w.shape[0] == 1
    x2 = x.reshape(s, d)
    l = 0
    row = lambda t: t.reshape(1, -1).astype(F32)

    w_in_l = w_in[l]
    wa = w_in_l[:, RWKV_COLS:]
    w_att = jnp.concatenate([wa[:, :D_ATT], wa[:, D_ATT + 2 * D_KV:], wa[:, D_ATT:D_ATT + 2 * D_KV]], axis=1)
    zeros_l = jnp.zeros((LORA, D_RWKV), F32)
    w2p = jnp.concatenate([w2[l], zeros_l], axis=0).astype(BF16)
    a2p = jnp.concatenate([zeros_l, a2[l]], axis=0).astype(BF16)
    w_out_r = w_out[l][:D_RWKV].astype(BF16)
    w_out_a = w_out[l][D_RWKV:].astype(BF16)
    qnw = jnp.tile(q_norm_w[l] * (HEAD_DIM ** -0.5 * LOG2E), N_Q_HEADS).reshape(1, D_ATT)
    knw = jnp.tile(k_norm_w[l], N_KV_HEADS).reshape(1, D_KV)
    bucket = jnp.asarray(_t5_bucket_table())

    nw = row(norm_w[l])
    z_rwkv = _proj_in(x2, nw, w_in_l, RWKV_COLS, 512, 1408, "proj_in_rwkv", mu=row(mu_rwkv[l]))
    z_att, *prep = _att_prep(x2, nw, w_att, z_rwkv, row(w0[l]), w2p, row(a0[l]), a2p, row(k_k[l]), row(k_a[l]),
                             row(r_k[l]), 512, 1152, 256)
    y_rwkv = _rwkv(prep, z_rwkv, row(lnx_w[l]), row(lnx_b[l]))
    out = _swa_out(z_att, bucket, sinks[l].astype(F32), rel_bias.astype(F32), qnw, knw,
                   x2, y_rwkv, w_out_r, w_out_a, 512)
    return out.reshape(b, s, d)
```

```python
import functools
import math

import jax
import jax.numpy as jnp
import numpy as np
from jax import lax
from jax.experimental import pallas as pl
from jax.experimental.pallas import tpu as pltpu

F32 = jnp.float32
BF16 = jnp.bfloat16

D_MODEL = 2048
SEQ = 8192
HEAD_DIM = 64
D_RWKV = 1024
D_ATT = 1024
LORA = 64
N_Q_HEADS = 16
N_KV_HEADS = 2
D_KV = N_KV_HEADS * HEAD_DIM
WINDOW = 128
BLOCK = 128
N_BUCKETS = 32
MAX_EXACT = N_BUCKETS // 2
MAX_DISTANCE = 128
NORM_EPS = 1e-6
LNX_EPS = 64e-5
RWKV_COLS = 4 * D_RWKV + 2 * LORA
ATT_COLS = 2 * D_ATT + 2 * D_KV

LANES = 128
MXU_WIDTH = 256
N_PAIRS = D_RWKV // LANES
CHUNK = 64
RWKV_CHUNKS_PER_GROUP = 4
RWKV_CHUNKS_PER_STEP = 8
NEG_BIG = -1e30
LOG2E = math.log2(math.e)

VMEM_LIMIT = 56 * 1024 * 1024


def _dot(a, b):
    return jnp.dot(a, b, preferred_element_type=F32)


def _dot_nt(a, b):
    return lax.dot_general(a, b, (((1,), (1,)), ((), ())), preferred_element_type=F32)


def _split3(x):
    hi = x.astype(BF16)
    r1 = x - hi.astype(F32)
    mid = r1.astype(BF16)
    lo = (r1 - mid.astype(F32)).astype(BF16)
    return hi, mid, lo


def _seg_sum(x, seg_ones):
    hi = x.astype(BF16)
    lo = (x - hi.astype(F32)).astype(BF16)
    return _dot(hi, seg_ones) + _dot(lo, seg_ones)


def _dot_exact_lhs(m, x):
    hi, mid, lo = _split3(x)
    return _dot(m, hi) + _dot(m, mid) + _dot(m, lo)


def _sigmoid(x):
    return 1.0 / (1.0 + jnp.exp(-x))


def _proj_in_kernel(x_ref, nw_ref, w_ref, *rest, token_shift):
    if token_shift:
        mu_ref, o_ref, wb_ref, prev_ref = rest
    else:
        o_ref, wb_ref = rest

    @pl.when(pl.program_id(1) == 0)
    def _():
        wb_ref[...] = w_ref[...].astype(BF16)
        if token_shift:
            prev_ref[...] = jnp.zeros_like(prev_ref)

    tm, tn = o_ref.shape
    x = x_ref[...]
    h = (x * nw_ref[...]).astype(BF16)
    rs = lax.rsqrt(jnp.mean(x * x, axis=-1, keepdims=True) + NORM_EPS)
    for c0 in range(0, tn, MXU_WIDTH):
        cs = slice(c0, min(c0 + MXU_WIDTH, tn))
        z = _dot(h, wb_ref[:, cs]) * rs
        if token_shift:
            row = lax.broadcasted_iota(jnp.int32, z.shape, 0)
            zprev = jnp.where(row == 0, prev_ref[:, cs], pltpu.roll(z, 1, axis=0))
            prev_ref[:, cs] = z[tm - 1:tm, :]
            z = z + (zprev - z) * mu_ref[:, cs]
        o_ref[:, cs] = z


def _proj_in(x2, norm_w, w, n, tm, tn, name, mu=None):
    s, d = x2.shape
    shift = mu is not None
    in_specs = [
        pl.BlockSpec((tm, d), lambda j, i: (i, 0)),
        pl.BlockSpec((1, d), lambda j, i: (0, 0)),
        pl.BlockSpec((d, tn), lambda j, i: (0, j)),
    ]
    args = [x2, norm_w, w]
    scratch = [pltpu.VMEM((d, tn), BF16)]
    if shift:
        in_specs.append(pl.BlockSpec((1, tn), lambda j, i: (0, j)))
        args.append(mu)
        scratch.append(pltpu.VMEM((1, tn), F32))
    return pl.pallas_call(
        functools.partial(_proj_in_kernel, token_shift=shift),
        grid=(n // tn, s // tm),
        in_specs=in_specs,
        out_specs=pl.BlockSpec((tm, tn), lambda j, i: (i, j)),
        out_shape=jax.ShapeDtypeStruct((s, n), F32),
        scratch_shapes=scratch,
        compiler_params=pltpu.CompilerParams(
            dimension_semantics=("arbitrary", "arbitrary"), vmem_limit_bytes=VMEM_LIMIT),
        name=name,
    )(*args)


def _att_prep_kernel(x_ref, nw_ref, w_ref, r_ref, k_ref, v_ref, lora_ref, w0_ref, w2_ref, a0_ref, a2_ref,
                     kk_ref, ka_ref, rk_ref,
                     o_ref, rt_ref, kt_ref, at_ref, bt_ref, vb_ref, bonus_ref, el_ref, wb_ref):
    c = CHUNK

    @pl.when(pl.program_id(1) == 0)
    def _():
        wb_ref[...] = w_ref[...].astype(BF16)

    tm, tn = o_ref.shape
    x = x_ref[...]
    h = (x * nw_ref[...]).astype(BF16)
    rs = lax.rsqrt(jnp.mean(x * x, axis=-1, keepdims=True) + NORM_EPS)

    def project():
        for c0 in range(0, tn, MXU_WIDTH):
            cs = slice(c0, min(c0 + MXU_WIDTH, tn))
            o_ref[:, cs] = _dot(h, wb_ref[:, cs]) * rs
            yield

    r128 = lax.broadcasted_iota(jnp.int32, (LANES, LANES), 0)
    c128 = lax.broadcasted_iota(jnp.int32, (LANES, LANES), 1)
    seg_ones = jnp.where((r128 // HEAD_DIM) == (c128 // HEAD_DIM), 1.0, 0.0).astype(BF16)
    ti = lax.broadcasted_iota(jnp.int32, (c, c), 0)
    si = lax.broadcasted_iota(jnp.int32, (c, c), 1)
    tril_ones = jnp.where(si <= ti, 1.0, 0.0).astype(BF16)
    pairs = range(N_PAIRS)
    psl = [slice(p * LANES, (p + 1) * LANES) for p in pairs]

    def seg_sum_pairs(y):
        z = _seg_sum(jnp.concatenate([y[:, s] for s in psl], axis=0), seg_ones)
        return jnp.concatenate([z[p * c:(p + 1) * c] for p in pairs], axis=1)

    def prepare():
        lora_in = lora_ref[...]
        u_lin = _dot(jnp.tanh(lora_in).astype(BF16), w2_ref[...])
        a_lin = _dot(lora_in.astype(BF16), a2_ref[...])
        yield
        for ci in range(r_ref.shape[0] // c):
            rows = slice(ci * c, (ci + 1) * c)
            logw = -math.exp(-0.5) * _sigmoid(w0_ref[...] + u_lin[rows, :])
            av = _sigmoid(a0_ref[...] + a_lin[rows, :])
            logp = _dot_exact_lhs(tril_ones, logw)
            e_p = jnp.exp(logp)
            e_n = 1.0 / e_p
            el_ref[0, ci:ci + 1, :] = e_p[c - 1:c, :]
            yield
            r = r_ref[rows, :]
            k = k_ref[rows, :]
            v = v_ref[rows, :]
            kmod = k * (1.0 + (av - 1.0) * ka_ref[...])
            rt_ref[rows, :] = (r * e_p).astype(rt_ref.dtype)
            kt_ref[rows, :] = (kmod * e_n).astype(kt_ref.dtype)
            vb_ref[rows, :] = v.astype(vb_ref.dtype)
            bonus_ref[rows, :] = seg_sum_pairs(r * kmod * rk_ref[...]) * v
            yield
            kk = k * kk_ref[...]
            kkn = kk * lax.rsqrt(jnp.maximum(seg_sum_pairs(kk * kk), 1e-24))
            at_ref[rows, :] = (-kkn * jnp.exp(logp - logw)).astype(at_ref.dtype)
            bt_ref[rows, :] = (kkn * av * e_n).astype(bt_ref.dtype)
            yield

    proj = project()
    n_slices = 1 + 3 * (r_ref.shape[0] // c)
    every = max(1, n_slices // (-(-tn // MXU_WIDTH)))
    for n, _ in enumerate(prepare()):
        if n % every == every - 1:
            next(proj, None)
    for _ in proj:
        pass


def _att_prep(x2, norm_w, w_att, z_rwkv, w0, w2p, a0, a2p, k_k, k_a, r_k, tm, tn, tp):
    s, d = x2.shape
    n = w_att.shape[1]
    nj, ni = n // tn, s // tm
    assert nj * ni * tp == s and tp % CHUNK == 0
    blk = lambda j, i: j * ni + i
    row = pl.BlockSpec((1, D_RWKV), lambda j, i: (0, 0))
    lora_w = pl.BlockSpec((2 * LORA, D_RWKV), lambda j, i: (0, 0))
    zcol = lambda cb: pl.BlockSpec((tp, D_RWKV), lambda j, i: (blk(j, i), cb))
    tok = pl.BlockSpec((tp, D_RWKV), lambda j, i: (blk(j, i), 0))
    sds = lambda dt: jax.ShapeDtypeStruct((s, D_RWKV), dt)
    return pl.pallas_call(
        _att_prep_kernel,
        grid=(nj, ni),
        in_specs=[
            pl.BlockSpec((tm, d), lambda j, i: (i, 0)),
            pl.BlockSpec((1, d), lambda j, i: (0, 0)),
            pl.BlockSpec((d, tn), lambda j, i: (0, j)),
            zcol(0), zcol(1), zcol(2),
            pl.BlockSpec((tp, 2 * LORA), lambda j, i: (blk(j, i), 4 * D_RWKV // (2 * LORA))),
            row, lora_w, row, lora_w, row, row, row,
        ],
        out_specs=[
            pl.BlockSpec((tm, tn), lambda j, i: (i, j)),
            tok, tok, tok, tok, tok, tok,
            pl.BlockSpec((1, tp // CHUNK, D_RWKV), lambda j, i: (blk(j, i), 0, 0)),
        ],
        out_shape=[
            jax.ShapeDtypeStruct((s, n), F32),
            sds(BF16), sds(BF16), sds(BF16), sds(BF16), sds(BF16), sds(F32),
            jax.ShapeDtypeStruct((s // tp, tp // CHUNK, D_RWKV), F32),
        ],
        scratch_shapes=[pltpu.VMEM((d, tn), BF16)],
        compiler_params=pltpu.CompilerParams(
            dimension_semantics=("arbitrary", "arbitrary"), vmem_limit_bytes=VMEM_LIMIT),
        name="proj_att_rwkv_prep",
    )(x2, norm_w, w_att, z_rwkv, z_rwkv, z_rwkv, z_rwkv, w0, w2p, a0, a2p, k_k, k_a, r_k)


def _rwkv_kernel(rt_ref, kt_ref, at_ref, bt_ref, v_ref, bonus_ref, el_ref, g_ref, lw_ref, lb_ref,
                 sinks_ref, relb_ref, bucket_ref, qa_ref, ga_ref, ka_ref, va_ref, qnw_ref, knw_ref,
                 o_ref, yatt_ref, gt_ref, bias_ref, kprev_ref, vprev_ref):
    c = CHUNK
    nch = RWKV_CHUNKS_PER_STEP
    i = pl.program_id(0)

    @pl.when(i == 0)
    def _():
        gt_ref[...] = jnp.zeros_like(gt_ref)
        _swa_init(relb_ref, bucket_ref, bias_ref, kprev_ref, vprev_ref)

    swa = _swa_stream(jnp.where(i == 0, 1, 0), sinks_ref, qa_ref, ga_ref, ka_ref, va_ref, qnw_ref, knw_ref,
                      bias_ref, kprev_ref, vprev_ref, yatt_ref)

    lane = lax.broadcasted_iota(jnp.int32, (c, LANES), 1)
    head0 = lane < HEAD_DIM
    trow = lax.broadcasted_iota(jnp.int32, (c, LANES), 0)
    eye_cat = jnp.where((lane % HEAD_DIM) == trow, 1.0, 0.0)
    t2 = lax.broadcasted_iota(jnp.int32, (c, 2 * LANES), 0)
    s2 = lax.broadcasted_iota(jnp.int32, (c, 2 * LANES), 1) % HEAD_DIM
    strict2 = s2 < t2
    incl2 = s2 <= t2
    r128 = lax.broadcasted_iota(jnp.int32, (LANES, LANES), 0)
    c128 = lax.broadcasted_iota(jnp.int32, (LANES, LANES), 1)
    same_head = (r128 // HEAD_DIM) == (c128 // HEAD_DIM)
    zeros_c = jnp.zeros((c, LANES), F32)

    pairs = range(N_PAIRS)
    psl = [slice(p * LANES, (p + 1) * LANES) for p in pairs]

    def stack2(y):
        return jnp.concatenate([jnp.where(head0, y, 0.0), jnp.where(head0, 0.0, y)], axis=0)

    def catmul(a_cat, y):
        return _dot(a_cat.astype(BF16), stack2(y).astype(BF16))

    seg_mean = jnp.where(same_head, 1.0 / HEAD_DIM, 0.0).astype(BF16)

    def seg_sum_pairs(x, seg):
        y = _seg_sum(jnp.concatenate([x[:, s] for s in psl], axis=0), seg)
        return jnp.concatenate([y[p * c:(p + 1) * c] for p in pairs], axis=1)

    el_chunks = el_ref.shape[1]

    def intra(chunks, out):
        per_item = lambda ref: [ref[ci * c:(ci + 1) * c, s].astype(F32) for ci in chunks for s in psl]
        a_t, b_t, r_t, k_t, v_i = (per_item(ref) for ref in (at_ref, bt_ref, rt_ref, kt_ref, v_ref))
        e_last = [el_ref[ci // el_chunks, ci % el_chunks:ci % el_chunks + 1, s]
                  for ci in chunks for s in psl]
        sc = [_dot_nt(jnp.concatenate([a, r], axis=0).astype(BF16),
                      jnp.concatenate([stack2(b), stack2(k)], axis=0).astype(BF16))
              for a, r, b, k in zip(a_t, r_t, b_t, k_t)]
        yield
        l_all = [jnp.where(strict2, x[0:c, :], 0.0) for x in sc]
        a_rbk = [jnp.where(incl2, x[c:, :], 0.0) for x in sc]
        x = [l[:, 0:LANES] for l in l_all]
        tinv = [eye_cat + xi for xi in x]
        x = [catmul(xi, xi) for xi in x]
        yield
        for _ in range(4):
            tx = [catmul(jnp.concatenate([ti_, xi], axis=0), xi) for ti_, xi in zip(tinv, x)]
            tinv = [ti_ + y[0:c] for ti_, y in zip(tinv, tx)]
            x = [y[c:] for y in tx]
            yield
        tinv = [ti_ + catmul(ti_, xi) for ti_, xi in zip(tinv, x)]
        yield
        a_eff = [catmul(ti_, a) for ti_, a in zip(tinv, a_t)]
        yield
        lakv = [catmul(l[:, LANES:], v) for l, v in zip(l_all, v_i)]
        yield
        w_loc = [catmul(ti_, y) for ti_, y in zip(tinv, lakv)]
        yield
        q_eff = [r + catmul(a[:, 0:LANES], ae) for r, a, ae in zip(r_t, a_rbk, a_eff)]
        yield
        o_loc = [_dot(a.astype(BF16), jnp.concatenate([stack2(w), stack2(v)], axis=0).astype(BF16))
                 for a, w, v in zip(a_rbk, w_loc, v_i)]
        yield
        bh = [b * el for b, el in zip(b_t, e_last)]
        kh = [k * el for k, el in zip(k_t, e_last)]
        m_mat = [jnp.where(same_head, _dot(jnp.concatenate([ae, zeros_c], axis=0).T.astype(BF16),
                                           jnp.concatenate([b, zeros_c], axis=0).astype(BF16)), 0.0)
                 for ae, b in zip(a_eff, bh)]
        yield
        n_mat = [jnp.where(same_head, _dot(jnp.concatenate([w, v], axis=0).T.astype(BF16),
                                           jnp.concatenate([b, k], axis=0).astype(BF16)), 0.0)
                 for w, v, b, k in zip(w_loc, v_i, bh, kh)]
        out.update(q_eff=q_eff, o_loc=o_loc, m_mat=m_mat, n_mat=n_mat, e_last=e_last)

    def advance(gts, ic, j):
        sel = lambda name: ic[name][j * N_PAIRS:(j + 1) * N_PAIRS]
        gtb = [g.astype(BF16) for g in gts]
        o = [_dot_nt(q.astype(BF16), g) + ol for q, g, ol in zip(sel("q_eff"), gtb, sel("o_loc"))]
        gts = [g * el + _dot(gb, m.astype(BF16)) + n
               for g, gb, el, m, n in zip(gts, gtb, sel("e_last"), sel("m_mat"), sel("n_mat"))]
        return gts, jnp.concatenate(o, axis=1)

    def finish(ci, o):
        rs = slice(ci * c, (ci + 1) * c)
        d = o - seg_sum_pairs(o, seg_mean)
        var = seg_sum_pairs(d * d, seg_mean)
        on = d * lax.rsqrt(var + LNX_EPS) * lw_ref[...] + lb_ref[...]
        g = g_ref[rs, :]
        o_ref[rs, :] = ((on + bonus_ref[rs, :]) * (g * _sigmoid(g))).astype(o_ref.dtype)

    gsz = RWKV_CHUNKS_PER_GROUP
    ngroups = nch // gsz
    state = dict(gts=[gt_ref[p] for p in pairs])

    def tail_group(g, ic):
        for j in range(gsz):
            state["gts"], o = advance(state["gts"], ic, j)
            yield
            finish(g * gsz + j, o)
            yield

    def drive(main, sides):
        for _ in main:
            for s_ in sides:
                next(s_, None)
            next(swa, None)
        for s_ in sides:
            for _ in s_:
                pass

    results = {}
    for g in range(ngroups):
        results[g] = {}
        sides = [tail_group(g - 1, results[g - 1])] if g >= 1 else []
        drive(intra(range(g * gsz, (g + 1) * gsz), results[g]), sides)
    drive(tail_group(ngroups - 1, results[ngroups - 1]), [])
    for _ in swa:
        pass
    for p in pairs:
        gt_ref[p] = state["gts"][p]


def _mixers(prep, z_rwkv, lnx_w, lnx_b, z_att, bucket, sinks, rel_bias, qnw, knw):
    s = z_rwkv.shape[0]
    t = CHUNK * RWKV_CHUNKS_PER_STEP
    bq = BLOCK
    el = prep[-1]
    blocks_per_step = RWKV_CHUNKS_PER_STEP // el.shape[1]
    tok = pl.BlockSpec((t, D_RWKV), lambda i: (i, 0))
    row = pl.BlockSpec((1, D_RWKV), lambda i: (0, 0))
    smem = pl.BlockSpec(memory_space=pltpu.SMEM)
    kv_col = 2 * D_ATT // D_KV
    return pl.pallas_call(
        _rwkv_kernel,
        grid=(s // t,),
        in_specs=[
            tok, tok, tok, tok, tok, tok,
            pl.BlockSpec((blocks_per_step, el.shape[1], D_RWKV), lambda i: (i, 0, 0)),
            pl.BlockSpec((t, D_RWKV), lambda i: (i, 3)),
            row, row,
            smem, smem,
            pl.BlockSpec((bq, 2 * bq), lambda i: (0, 0)),
            pl.BlockSpec((t, D_ATT), lambda i: (i, 0)),
            pl.BlockSpec((t, D_ATT), lambda i: (i, 1)),
            pl.BlockSpec((t, D_KV), lambda i: (i, kv_col)),
            pl.BlockSpec((t, D_KV), lambda i: (i, kv_col + 1)),
            pl.BlockSpec((1, D_ATT), lambda i: (0, 0)),
            pl.BlockSpec((1, D_KV), lambda i: (0, 0)),
        ],
        out_specs=[tok, pl.BlockSpec((t, D_ATT), lambda i: (i, 0))],
        out_shape=[jax.ShapeDtypeStruct((s, D_RWKV), BF16), jax.ShapeDtypeStruct((s, D_ATT), BF16)],
        scratch_shapes=[
            pltpu.VMEM((N_PAIRS, LANES, LANES), F32),
            pltpu.VMEM((2, N_Q_HEADS, bq, 2 * bq), F32),
            pltpu.VMEM((bq, D_KV), F32),
            pltpu.VMEM((bq, D_KV), F32),
        ],
        compiler_params=pltpu.CompilerParams(
            dimension_semantics=("arbitrary",), vmem_limit_bytes=VMEM_LIMIT),
        name="rwkv7_swa",
    )(*prep, z_rwkv, lnx_w, lnx_b, sinks, rel_bias, bucket, z_att, z_att, z_att, z_att, qnw, knw)


def _swa_init(relb_ref, bucket_ref, bias_ref, kprev_ref, vprev_ref):
    bq = BLOCK
    kprev_ref[...] = jnp.zeros_like(kprev_ref)
    vprev_ref[...] = jnp.zeros_like(vprev_ref)
    bucket = bucket_ref[...]
    qi = lax.broadcasted_iota(jnp.int32, (bq, 2 * bq), 0)
    kj = lax.broadcasted_iota(jnp.int32, (bq, 2 * bq), 1)
    dist = bq + qi - kj
    inwin = (dist >= 0) & (dist < WINDOW)
    for h in range(N_Q_HEADS):
        acc = jnp.zeros((bq, 2 * bq), F32)
        for b in range(N_BUCKETS):
            acc = jnp.where(bucket == b, relb_ref[b, h], acc)
        acc = jnp.where(inwin, acc * LOG2E, NEG_BIG)
        bias_ref[0, h] = acc
        bias_ref[1, h] = jnp.where(kj >= bq, acc, NEG_BIG)


def _swa_stream(first_step, sinks_ref, q_ref, g_ref, k_ref, v_ref, qnw_ref, knw_ref,
                bias_ref, kprev_ref, vprev_ref, yatt_ref):
    bq = BLOCK
    tm = q_ref.shape[0]
    lane = lax.broadcasted_iota(jnp.int32, (bq, LANES), 1)
    head0 = lane < HEAD_DIM
    lane2 = lax.broadcasted_iota(jnp.int32, (2 * bq, LANES), 1)
    head0_2 = lane2 < HEAD_DIM
    r128 = lax.broadcasted_iota(jnp.int32, (LANES, LANES), 0)
    c128 = lax.broadcasted_iota(jnp.int32, (LANES, LANES), 1)
    seg_mean = jnp.where((r128 // HEAD_DIM) == (c128 // HEAD_DIM), 1.0 / HEAD_DIM, 0.0).astype(BF16)
    sink2 = [sinks_ref[h] * LOG2E for h in range(N_Q_HEADS)]
    pairs = range(N_PAIRS)
    heads = range(N_Q_HEADS)
    psl = [slice(p * LANES, (p + 1) * LANES) for p in pairs]
    kvh = [(2 * p) // (N_Q_HEADS // N_KV_HEADS) for p in pairs]

    def rms_heads(t, w):
        return t * lax.rsqrt(_seg_sum(t * t, seg_mean) + NORM_EPS) * w

    kp, vp = kprev_ref[...], vprev_ref[...]
    for b in range(tm // bq):
        rb = slice(b * bq, (b + 1) * bq)
        first = first_step if b == 0 else 0
        kn = rms_heads(k_ref[rb, :], knw_ref[...])
        vc = v_ref[rb, :]
        kcat = jnp.concatenate([kp, kn], axis=0)
        vcat = jnp.concatenate([vp, vc], axis=0)
        kp, vp = kn, vc
        krol = pltpu.roll(kcat, HEAD_DIM, axis=1)
        vrol = pltpu.roll(vcat, HEAD_DIM, axis=1)
        kdup = [jnp.where(head0_2, kcat, krol).astype(BF16), jnp.where(head0_2, krol, kcat).astype(BF16)]
        vdup = [jnp.where(head0_2, vcat, vrol).astype(BF16), jnp.where(head0_2, vrol, vcat).astype(BF16)]
        yield
        qn = [rms_heads(q_ref[rb, s], qnw_ref[:, s]) for s in psl]
        yield
        q2 = [jnp.concatenate([jnp.where(head0, q, 0.0), jnp.where(head0, 0.0, q)], axis=0).astype(BF16)
              for q in qn]
        lg2 = [_dot_nt(q, kdup[kv]) for q, kv in zip(q2, kvh)]
        yield
        lg = [lg2[h // 2][(h % 2) * bq:(h % 2 + 1) * bq, :] + bias_ref[first, h] for h in heads]
        yield
        m = [jnp.maximum(jnp.max(l, axis=-1, keepdims=True), sink2[h]) for h, l in zip(heads, lg)]
        yield
        e = [jnp.exp2(l - mm) for l, mm in zip(lg, m)]
        yield
        denom = [jnp.sum(ee, axis=-1, keepdims=True) + jnp.exp2(sink2[h] - mm)
                 for h, ee, mm in zip(heads, e, m)]
        yield
        pv = [_dot(ee.astype(BF16), vdup[kvh[h // 2]]) for h, ee in zip(heads, e)]
        yield
        outs = [x / d for x, d in zip(pv, denom)]
        for p in pairs:
            o = jnp.where(head0, outs[2 * p], outs[2 * p + 1])
            g = g_ref[rb, psl[p]]
            yatt_ref[rb, psl[p]] = (o * (g * _sigmoid(g))).astype(yatt_ref.dtype)
        yield
    kprev_ref[...] = kp
    vprev_ref[...] = vp


def _proj_out_kernel(x_ref, yr_ref, ya_ref, wr_ref, wa_ref, o_ref):
    o_ref[...] = x_ref[...] + _dot(yr_ref[...], wr_ref[...]) + _dot(ya_ref[...], wa_ref[...])


def _proj_out(x2, y_rwkv, y_att, w_r, w_a, tm):
    s, d = x2.shape
    return pl.pallas_call(
        _proj_out_kernel,
        grid=(s // tm,),
        in_specs=[
            pl.BlockSpec((tm, d), lambda i: (i, 0)),
            pl.BlockSpec((tm, D_RWKV), lambda i: (i, 0)),
            pl.BlockSpec((tm, D_ATT), lambda i: (i, 0)),
            pl.BlockSpec((D_RWKV, d), lambda i: (0, 0)),
            pl.BlockSpec((D_ATT, d), lambda i: (0, 0)),
        ],
        out_specs=pl.BlockSpec((tm, d), lambda i: (i, 0)),
        out_shape=jax.ShapeDtypeStruct((s, d), F32),
        compiler_params=pltpu.CompilerParams(
            dimension_semantics=("arbitrary",), vmem_limit_bytes=VMEM_LIMIT),
        name="proj_out",
    )(x2, y_rwkv, y_att, w_r, w_a)


def _t5_bucket_table():
    dist = BLOCK + np.arange(BLOCK)[:, None] - np.arange(2 * BLOCK)[None, :]
    n = np.maximum(dist, 0)
    nf = np.maximum(n, 1).astype(np.float64)
    large = MAX_EXACT + (np.log(nf / MAX_EXACT) / math.log(MAX_DISTANCE / MAX_EXACT)
                         * (N_BUCKETS - MAX_EXACT)).astype(np.int32)
    large = np.minimum(large, N_BUCKETS - 1)
    return np.where(n < MAX_EXACT, n, large).astype(np.int32)


def kernel(x, norm_w, w_in, w_out, mu_rwkv, w0, w2, a0, a2, k_k, k_a, r_k, lnx_w, lnx_b,
           q_norm_w, k_norm_w, sinks, rel_bias):
    b, s, d = x.shape
    assert (b, s, d) == (1, SEQ, D_MODEL) and norm_w.shape[0] == 1
    x2 = x.reshape(s, d)
    l = 0
    row = lambda t: t.reshape(1, -1).astype(F32)

    w_in_l = w_in[l]
    wa = w_in_l[:, RWKV_COLS:]
    w_att = jnp.concatenate([wa[:, :D_ATT], wa[:, D_ATT + 2 * D_KV:], wa[:, D_ATT:D_ATT + 2 * D_KV]], axis=1)
    zeros_l = jnp.zeros((LORA, D_RWKV), F32)
    w2p = jnp.concatenate([w2[l], zeros_l], axis=0).astype(BF16)
    a2p = jnp.concatenate([zeros_l, a2[l]], axis=0).astype(BF16)
    w_out_r = w_out[l][:D_RWKV].astype(BF16)
    w_out_a = w_out[l][D_RWKV:].astype(BF16)
    qnw = jnp.tile(q_norm_w[l] * (HEAD_DIM ** -0.5 * LOG2E), N_Q_HEADS).reshape(1, D_ATT)
    knw = jnp.tile(k_norm_w[l], N_KV_HEADS).reshape(1, D_KV)
    bucket = jnp.asarray(_t5_bucket_table())

    nw = row(norm_w[l])
    z_rwkv = _proj_in(x2, nw, w_in_l, RWKV_COLS, 512, 1408, "proj_in_rwkv", mu=row(mu_rwkv[l]))
    z_att, *prep = _att_prep(x2, nw, w_att, z_rwkv, row(w0[l]), w2p, row(a0[l]), a2p, row(k_k[l]), row(k_a[l]),
                             row(r_k[l]), 512, 1152, 256)
    y_rwkv, y_att = _mixers(prep, z_rwkv, row(lnx_w[l]), row(lnx_b[l]),
                            z_att, bucket, sinks[l].astype(F32), rel_bias.astype(F32), qnw, knw)
    out = _proj_out(x2, y_rwkv, y_att, w_out_r, w_out_a, 512)
    return out.reshape(b, s, d)
```

```python
import math

import jax
import jax.numpy as jnp
import numpy as np
from jax import lax
from jax.experimental import pallas as pl
from jax.experimental.pallas import tpu as pltpu

F32 = jnp.float32
BF16 = jnp.bfloat16

D_MODEL = 2048
SEQ = 8192
HEAD_DIM = 64
D_RWKV = 1024
D_ATT = 1024
LORA = 64
N_Q_HEADS = 16
N_KV_HEADS = 2
D_KV = N_KV_HEADS * HEAD_DIM
WINDOW = 128
BLOCK = 128
N_BUCKETS = 32
MAX_EXACT = N_BUCKETS // 2
MAX_DISTANCE = 128
NORM_EPS = 1e-6
LNX_EPS = 64e-5
RWKV_COLS = 4 * D_RWKV + 2 * LORA
ATT_COLS = 2 * D_ATT + 2 * D_KV

LANES = 128
MXU_WIDTH = 256
N_PAIRS = D_RWKV // LANES
CHUNK = 64
RWKV_CHUNKS_PER_GROUP = 4
RWKV_CHUNKS_PER_STEP = 8
NEG_BIG = -1e30
LOG2E = math.log2(math.e)

VMEM_LIMIT = 56 * 1024 * 1024


def _dot(a, b):
    return jnp.dot(a, b, preferred_element_type=F32)


def _dot_nt(a, b):
    return lax.dot_general(a, b, (((1,), (1,)), ((), ())), preferred_element_type=F32)


def _split3(x):
    hi = x.astype(BF16)
    r1 = x - hi.astype(F32)
    mid = r1.astype(BF16)
    lo = (r1 - mid.astype(F32)).astype(BF16)
    return hi, mid, lo


def _seg_sum(x, seg_ones):
    hi = x.astype(BF16)
    lo = (x - hi.astype(F32)).astype(BF16)
    return _dot(hi, seg_ones) + _dot(lo, seg_ones)


def _dot_exact_lhs(m, x):
    hi, mid, lo = _split3(x)
    return _dot(m, hi) + _dot(m, mid) + _dot(m, lo)


def _sigmoid(x):
    return 1.0 / (1.0 + jnp.exp(-x))


def _proj_in_kernel(x_ref, nw_ref, w_ref, mu_ref, o_ref, h_ref, rs_ref, wb_ref, prev_ref):
    first_col_block = pl.program_id(0) == 0

    @pl.when(pl.program_id(1) == 0)
    def _():
        wb_ref[...] = w_ref[...].astype(BF16)
        prev_ref[...] = jnp.zeros_like(prev_ref)

    tm, tn = o_ref.shape
    x = x_ref[...]
    h = (x * nw_ref[...]).astype(BF16)
    rs = lax.rsqrt(jnp.mean(x * x, axis=-1, keepdims=True) + NORM_EPS)
    for c0 in range(0, tn, MXU_WIDTH):
        cs = slice(c0, min(c0 + MXU_WIDTH, tn))
        z = _dot(h, wb_ref[:, cs]) * rs
        row = lax.broadcasted_iota(jnp.int32, z.shape, 0)
        zprev = jnp.where(row == 0, prev_ref[:, cs], pltpu.roll(z, 1, axis=0))
        prev_ref[:, cs] = z[tm - 1:tm, :]
        o_ref[:, cs] = z + (zprev - z) * mu_ref[:, cs]

    @pl.when(first_col_block)
    def _():
        h_ref[...] = h
        rs_ref[...] = rs


def _proj_in(x2, norm_w, w, n, mu, tm, tn):
    s, d = x2.shape
    ni = s // tm
    once = lambda j, i: (jnp.where(j == 0, i, ni - 1), 0)
    return pl.pallas_call(
        _proj_in_kernel,
        grid=(n // tn, ni),
        in_specs=[
            pl.BlockSpec((tm, d), lambda j, i: (i, 0)),
            pl.BlockSpec((1, d), lambda j, i: (0, 0)),
            pl.BlockSpec((d, tn), lambda j, i: (0, j)),
            pl.BlockSpec((1, tn), lambda j, i: (0, j)),
        ],
        out_specs=[
            pl.BlockSpec((tm, tn), lambda j, i: (i, j)),
            pl.BlockSpec((tm, d), once),
            pl.BlockSpec((tm, 1), once),
        ],
        out_shape=[
            jax.ShapeDtypeStruct((s, n), F32),
            jax.ShapeDtypeStruct((s, d), BF16),
            jax.ShapeDtypeStruct((s, 1), F32),
        ],
        scratch_shapes=[pltpu.VMEM((d, tn), BF16), pltpu.VMEM((1, tn), F32)],
        compiler_params=pltpu.CompilerParams(
            dimension_semantics=("arbitrary", "arbitrary"), vmem_limit_bytes=VMEM_LIMIT),
        name="proj_in_rwkv",
    )(x2, norm_w, w, mu)


def _att_prep_kernel(h_ref, rs_ref, w_ref, r_ref, k_ref, v_ref, lora_ref, w0_ref, w2_ref, a0_ref, a2_ref,
                     kk_ref, ka_ref, rk_ref,
                     o_ref, rt_ref, kt_ref, at_ref, bt_ref, vb_ref, bonus_ref, el_ref, wb_ref):
    c = CHUNK

    @pl.when(pl.program_id(1) == 0)
    def _():
        wb_ref[...] = w_ref[...].astype(BF16)

    tm, tn = o_ref.shape
    h = h_ref[...]
    rs = rs_ref[...]

    def project():
        for c0 in range(0, tn, MXU_WIDTH):
            cs = slice(c0, min(c0 + MXU_WIDTH, tn))
            o_ref[:, cs] = _dot(h, wb_ref[:, cs]) * rs
            yield

    r128 = lax.broadcasted_iota(jnp.int32, (LANES, LANES), 0)
    c128 = lax.broadcasted_iota(jnp.int32, (LANES, LANES), 1)
    seg_ones = jnp.where((r128 // HEAD_DIM) == (c128 // HEAD_DIM), 1.0, 0.0).astype(BF16)
    ti = lax.broadcasted_iota(jnp.int32, (c, c), 0)
    si = lax.broadcasted_iota(jnp.int32, (c, c), 1)
    tril_ones = jnp.where(si <= ti, 1.0, 0.0).astype(BF16)
    pairs = range(N_PAIRS)
    psl = [slice(p * LANES, (p + 1) * LANES) for p in pairs]

    def seg_sum_pairs(y):
        z = _seg_sum(jnp.concatenate([y[:, s] for s in psl], axis=0), seg_ones)
        return jnp.concatenate([z[p * c:(p + 1) * c] for p in pairs], axis=1)

    def prepare():
        lora_in = lora_ref[...]
        u_lin = _dot(jnp.tanh(lora_in).astype(BF16), w2_ref[...])
        a_lin = _dot(lora_in.astype(BF16), a2_ref[...])
        yield
        for ci in range(r_ref.shape[0] // c):
            rows = slice(ci * c, (ci + 1) * c)
            logw = -math.exp(-0.5) * _sigmoid(w0_ref[...] + u_lin[rows, :])
            av = _sigmoid(a0_ref[...] + a_lin[rows, :])
            logp = _dot_exact_lhs(tril_ones, logw)
            e_p = jnp.exp(logp)
            e_n = 1.0 / e_p
            el_ref[0, ci:ci + 1, :] = e_p[c - 1:c, :]
            yield
            r = r_ref[rows, :]
            k = k_ref[rows, :]
            v = v_ref[rows, :]
            kmod = k * (1.0 + (av - 1.0) * ka_ref[...])
            rt_ref[rows, :] = (r * e_p).astype(rt_ref.dtype)
            kt_ref[rows, :] = (kmod * e_n).astype(kt_ref.dtype)
            vb_ref[rows, :] = v.astype(vb_ref.dtype)
            bonus_ref[rows, :] = seg_sum_pairs(r * kmod * rk_ref[...]) * v
            yield
            kk = k * kk_ref[...]
            kkn = kk * lax.rsqrt(jnp.maximum(seg_sum_pairs(kk * kk), 1e-24))
            at_ref[rows, :] = (-kkn * jnp.exp(logp - logw)).astype(at_ref.dtype)
            bt_ref[rows, :] = (kkn * av * e_n).astype(bt_ref.dtype)
            yield

    proj = project()
    n_slices = 1 + 3 * (r_ref.shape[0] // c)
    every = max(1, n_slices // (-(-tn // MXU_WIDTH)))
    for n, _ in enumerate(prepare()):
        if n % every == every - 1:
            next(proj, None)
    for _ in proj:
        pass


def _att_prep(h, rs, w_att, z_rwkv, w0, w2p, a0, a2p, k_k, k_a, r_k, tm, tn, tp):
    s, d = h.shape
    n = w_att.shape[1]
    nj, ni = n // tn, s // tm
    assert nj * ni * tp == s and tp % CHUNK == 0
    blk = lambda j, i: j * ni + i
    row = pl.BlockSpec((1, D_RWKV), lambda j, i: (0, 0))
    lora_w = pl.BlockSpec((2 * LORA, D_RWKV), lambda j, i: (0, 0))
    zcol = lambda cb: pl.BlockSpec((tp, D_RWKV), lambda j, i: (blk(j, i), cb))
    tok = pl.BlockSpec((tp, D_RWKV), lambda j, i: (blk(j, i), 0))
    sds = lambda dt: jax.ShapeDtypeStruct((s, D_RWKV), dt)
    return pl.pallas_call(
        _att_prep_kernel,
        grid=(nj, ni),
        in_specs=[
            pl.BlockSpec((tm, d), lambda j, i: (i, 0)),
            pl.BlockSpec((tm, 1), lambda j, i: (i, 0)),
            pl.BlockSpec((d, tn), lambda j, i: (0, j)),
            zcol(0), zcol(1), zcol(2),
            pl.BlockSpec((tp, 2 * LORA), lambda j, i: (blk(j, i), 4 * D_RWKV // (2 * LORA))),
            row, lora_w, row, lora_w, row, row, row,
        ],
        out_specs=[
            pl.BlockSpec((tm, tn), lambda j, i: (i, j)),
            tok, tok, tok, tok, tok, tok,
            pl.BlockSpec((1, tp // CHUNK, D_RWKV), lambda j, i: (blk(j, i), 0, 0)),
        ],
        out_shape=[
            jax.ShapeDtypeStruct((s, n), F32),
            sds(BF16), sds(BF16), sds(BF16), sds(BF16), sds(BF16), sds(F32),
            jax.ShapeDtypeStruct((s // tp, tp // CHUNK, D_RWKV), F32),
        ],
        scratch_shapes=[pltpu.VMEM((d, tn), BF16)],
        compiler_params=pltpu.CompilerParams(
            dimension_semantics=("arbitrary", "arbitrary"), vmem_limit_bytes=VMEM_LIMIT),
        name="proj_att_rwkv_prep",
    )(h, rs, w_att, z_rwkv, z_rwkv, z_rwkv, z_rwkv, w0, w2p, a0, a2p, k_k, k_a, r_k)


def _rwkv_kernel(rt_ref, kt_ref, at_ref, bt_ref, v_ref, bonus_ref, el_ref, g_ref,
                 lw_ref, lb_ref, o_ref, gt_ref):
    c = CHUNK
    nch = RWKV_CHUNKS_PER_STEP
    i = pl.program_id(0)

    @pl.when(i == 0)
    def _():
        gt_ref[...] = jnp.zeros_like(gt_ref)

    lane = lax.broadcasted_iota(jnp.int32, (c, LANES), 1)
    head0 = lane < HEAD_DIM
    trow = lax.broadcasted_iota(jnp.int32, (c, LANES), 0)
    eye_cat = jnp.where((lane % HEAD_DIM) == trow, 1.0, 0.0)
    t2 = lax.broadcasted_iota(jnp.int32, (c, 2 * LANES), 0)
    s2 = lax.broadcasted_iota(jnp.int32, (c, 2 * LANES), 1) % HEAD_DIM
    strict2 = s2 < t2
    incl2 = s2 <= t2
    r128 = lax.broadcasted_iota(jnp.int32, (LANES, LANES), 0)
    c128 = lax.broadcasted_iota(jnp.int32, (LANES, LANES), 1)
    same_head = (r128 // HEAD_DIM) == (c128 // HEAD_DIM)
    zeros_c = jnp.zeros((c, LANES), F32)

    pairs = range(N_PAIRS)
    psl = [slice(p * LANES, (p + 1) * LANES) for p in pairs]

    def stack2(y):
        return jnp.concatenate([jnp.where(head0, y, 0.0), jnp.where(head0, 0.0, y)], axis=0)

    def catmul(a_cat, y):
        return _dot(a_cat.astype(BF16), stack2(y).astype(BF16))

    seg_mean = jnp.where(same_head, 1.0 / HEAD_DIM, 0.0).astype(BF16)

    def seg_sum_pairs(x, seg):
        y = _seg_sum(jnp.concatenate([x[:, s] for s in psl], axis=0), seg)
        return jnp.concatenate([y[p * c:(p + 1) * c] for p in pairs], axis=1)

    el_chunks = el_ref.shape[1]

    def intra(chunks, out):
        per_item = lambda ref: [ref[ci * c:(ci + 1) * c, s].astype(F32) for ci in chunks for s in psl]
        a_t, b_t, r_t, k_t, v_i = (per_item(ref) for ref in (at_ref, bt_ref, rt_ref, kt_ref, v_ref))
        e_last = [el_ref[ci // el_chunks, ci % el_chunks:ci % el_chunks + 1, s]
                  for ci in chunks for s in psl]
        sc = [_dot_nt(jnp.concatenate([a, r], axis=0).astype(BF16),
                      jnp.concatenate([stack2(b), stack2(k)], axis=0).astype(BF16))
              for a, r, b, k in zip(a_t, r_t, b_t, k_t)]
        yield
        l_all = [jnp.where(strict2, x[0:c, :], 0.0) for x in sc]
        a_rbk = [jnp.where(incl2, x[c:, :], 0.0) for x in sc]
        x = [l[:, 0:LANES] for l in l_all]
        tinv = [eye_cat + xi for xi in x]
        x = [catmul(xi, xi) for xi in x]
        yield
        for _ in range(4):
            tx = [catmul(jnp.concatenate([ti_, xi], axis=0), xi) for ti_, xi in zip(tinv, x)]
            tinv = [ti_ + y[0:c] for ti_, y in zip(tinv, tx)]
            x = [y[c:] for y in tx]
            yield
        tinv = [ti_ + catmul(ti_, xi) for ti_, xi in zip(tinv, x)]
        yield
        a_eff = [catmul(ti_, a) for ti_, a in zip(tinv, a_t)]
        yield
        lakv = [catmul(l[:, LANES:], v) for l, v in zip(l_all, v_i)]
        yield
        w_loc = [catmul(ti_, y) for ti_, y in zip(tinv, lakv)]
        yield
        q_eff = [r + catmul(a[:, 0:LANES], ae) for r, a, ae in zip(r_t, a_rbk, a_eff)]
        yield
        o_loc = [_dot(a.astype(BF16), jnp.concatenate([stack2(w), stack2(v)], axis=0).astype(BF16))
                 for a, w, v in zip(a_rbk, w_loc, v_i)]
        yield
        bh = [b * el for b, el in zip(b_t, e_last)]
        kh = [k * el for k, el in zip(k_t, e_last)]
        m_mat = [jnp.where(same_head, _dot(jnp.concatenate([ae, zeros_c], axis=0).T.astype(BF16),
                                           jnp.concatenate([b, zeros_c], axis=0).astype(BF16)), 0.0)
                 for ae, b in zip(a_eff, bh)]
        yield
        n_mat = [jnp.where(same_head, _dot(jnp.concatenate([w, v], axis=0).T.astype(BF16),
                                           jnp.concatenate([b, k], axis=0).astype(BF16)), 0.0)
                 for w, v, b, k in zip(w_loc, v_i, bh, kh)]
        out.update(q_eff=q_eff, o_loc=o_loc, m_mat=m_mat, n_mat=n_mat, e_last=e_last)

    def advance(gts, ic, j):
        sel = lambda name: ic[name][j * N_PAIRS:(j + 1) * N_PAIRS]
        gtb = [g.astype(BF16) for g in gts]
        o = [_dot_nt(q.astype(BF16), g) + ol for q, g, ol in zip(sel("q_eff"), gtb, sel("o_loc"))]
        gts = [g * el + _dot(gb, m.astype(BF16)) + n
               for g, gb, el, m, n in zip(gts, gtb, sel("e_last"), sel("m_mat"), sel("n_mat"))]
        return gts, jnp.concatenate(o, axis=1)

    def finish(ci, o):
        rs = slice(ci * c, (ci + 1) * c)
        d = o - seg_sum_pairs(o, seg_mean)
        var = seg_sum_pairs(d * d, seg_mean)
        on = d * lax.rsqrt(var + LNX_EPS) * lw_ref[...] + lb_ref[...]
        g = g_ref[rs, :]
        o_ref[rs, :] = ((on + bonus_ref[rs, :]) * (g * _sigmoid(g))).astype(o_ref.dtype)

    gsz = RWKV_CHUNKS_PER_GROUP
    ngroups = nch // gsz
    state = dict(gts=[gt_ref[p] for p in pairs])

    def tail_group(g, ic):
        for j in range(gsz):
            state["gts"], o = advance(state["gts"], ic, j)
            yield
            finish(g * gsz + j, o)
            yield

    def drive(main, sides):
        for _ in main:
            for s_ in sides:
                next(s_, None)
        for s_ in sides:
            for _ in s_:
                pass

    results = {}
    for g in range(ngroups):
        results[g] = {}
        sides = [tail_group(g - 1, results[g - 1])] if g >= 1 else []
        drive(intra(range(g * gsz, (g + 1) * gsz), results[g]), sides)
    drive(tail_group(ngroups - 1, results[ngroups - 1]), [])
    for p in pairs:
        gt_ref[p] = state["gts"][p]


def _rwkv(prep, z_rwkv, lnx_w, lnx_b):
    s = z_rwkv.shape[0]
    t = CHUNK * RWKV_CHUNKS_PER_STEP
    el = prep[-1]
    blocks_per_step = RWKV_CHUNKS_PER_STEP // el.shape[1]
    tok = pl.BlockSpec((t, D_RWKV), lambda i: (i, 0))
    row = pl.BlockSpec((1, D_RWKV), lambda i: (0, 0))
    return pl.pallas_call(
        _rwkv_kernel,
        grid=(s // t,),
        in_specs=[
            tok, tok, tok, tok, tok, tok,
            pl.BlockSpec((blocks_per_step, el.shape[1], D_RWKV), lambda i: (i, 0, 0)),
            pl.BlockSpec((t, D_RWKV), lambda i: (i, 3)),
            row, row,
        ],
        out_specs=tok,
        out_shape=jax.ShapeDtypeStruct((s, D_RWKV), BF16),
        scratch_shapes=[
            pltpu.VMEM((N_PAIRS, LANES, LANES), F32),
        ],
        compiler_params=pltpu.CompilerParams(
            dimension_semantics=("arbitrary",), vmem_limit_bytes=VMEM_LIMIT),
        name="rwkv7",
    )(*prep, z_rwkv, lnx_w, lnx_b)


def _swa_out_kernel(sinks_ref, relb_ref, bucket_ref, q_ref, g_ref, k_ref, v_ref, qnw_ref, knw_ref,
                    x_ref, yr_ref, wr_ref, wa_ref, o_ref, bias_ref, kprev_ref, vprev_ref, yatt_ref):
    i = pl.program_id(0)
    bq = BLOCK
    tm = q_ref.shape[0]

    @pl.when(i == 0)
    def _():
        kprev_ref[...] = jnp.zeros_like(kprev_ref)
        vprev_ref[...] = jnp.zeros_like(vprev_ref)
        yatt_ref[...] = jnp.zeros_like(yatt_ref)
        bucket = bucket_ref[...]
        qi = lax.broadcasted_iota(jnp.int32, (bq, 2 * bq), 0)
        kj = lax.broadcasted_iota(jnp.int32, (bq, 2 * bq), 1)
        dist = bq + qi - kj
        inwin = (dist >= 0) & (dist < WINDOW)
        for h in range(N_Q_HEADS):
            acc = jnp.zeros((bq, 2 * bq), F32)
            for b in range(N_BUCKETS):
                acc = jnp.where(bucket == b, relb_ref[b, h], acc)
            acc = jnp.where(inwin, acc * LOG2E, NEG_BIG)
            bias_ref[0, h] = acc
            bias_ref[1, h] = jnp.where(kj >= bq, acc, NEG_BIG)

    first_step = jnp.where(i == 0, 1, 0)
    lane = lax.broadcasted_iota(jnp.int32, (bq, LANES), 1)
    head0 = lane < HEAD_DIM
    lane2 = lax.broadcasted_iota(jnp.int32, (2 * bq, LANES), 1)
    head0_2 = lane2 < HEAD_DIM
    r128 = lax.broadcasted_iota(jnp.int32, (LANES, LANES), 0)
    c128 = lax.broadcasted_iota(jnp.int32, (LANES, LANES), 1)
    seg_mean = jnp.where((r128 // HEAD_DIM) == (c128 // HEAD_DIM), 1.0 / HEAD_DIM, 0.0).astype(BF16)
    sink2 = [sinks_ref[h] * LOG2E for h in range(N_Q_HEADS)]
    pairs = range(N_PAIRS)
    heads = range(N_Q_HEADS)
    psl = [slice(p * LANES, (p + 1) * LANES) for p in pairs]
    kvh = [(2 * p) // (N_Q_HEADS // N_KV_HEADS) for p in pairs]

    def rms_heads(t, w):
        return t * lax.rsqrt(_seg_sum(t * t, seg_mean) + NORM_EPS) * w

    ya_prev = yatt_ref[...]
    yr_prev = yr_ref[...]

    def project_previous():
        for c0 in range(0, D_MODEL, MXU_WIDTH):
            cs = slice(c0, c0 + MXU_WIDTH)
            o_ref[:, cs] = x_ref[:, cs] + _dot(yr_prev, wr_ref[:, cs]) + _dot(ya_prev, wa_ref[:, cs])
            yield

    def attend():
        kp, vp = kprev_ref[...], vprev_ref[...]
        for b in range(tm // bq):
            rb = slice(b * bq, (b + 1) * bq)
            first = first_step if b == 0 else 0
            kn = rms_heads(k_ref[rb, :], knw_ref[...])
            vc = v_ref[rb, :]
            kcat = jnp.concatenate([kp, kn], axis=0)
            vcat = jnp.concatenate([vp, vc], axis=0)
            kp, vp = kn, vc
            krol = pltpu.roll(kcat, HEAD_DIM, axis=1)
            vrol = pltpu.roll(vcat, HEAD_DIM, axis=1)
            kdup = [jnp.where(head0_2, kcat, krol).astype(BF16), jnp.where(head0_2, krol, kcat).astype(BF16)]
            vdup = [jnp.where(head0_2, vcat, vrol).astype(BF16), jnp.where(head0_2, vrol, vcat).astype(BF16)]
            yield
            qn = [rms_heads(q_ref[rb, s], qnw_ref[:, s]) for s in psl]
            yield
            q2 = [jnp.concatenate([jnp.where(head0, q, 0.0), jnp.where(head0, 0.0, q)], axis=0).astype(BF16)
                  for q in qn]
            lg2 = [_dot_nt(q, kdup[kv]) for q, kv in zip(q2, kvh)]
            yield
            lg = [lg2[h // 2][(h % 2) * bq:(h % 2 + 1) * bq, :] + bias_ref[first, h] for h in heads]
            yield
            m = [jnp.maximum(jnp.max(l, axis=-1, keepdims=True), sink2[h]) for h, l in zip(heads, lg)]
            yield
            e = [jnp.exp2(l - mm) for l, mm in zip(lg, m)]
            yield
            denom = [jnp.sum(ee, axis=-1, keepdims=True) + jnp.exp2(sink2[h] - mm)
                     for h, ee, mm in zip(heads, e, m)]
            yield
            pv = [_dot(ee.astype(BF16), vdup[kvh[h // 2]]) for h, ee in zip(heads, e)]
            yield
            outs = [x / d for x, d in zip(pv, denom)]
            for p in pairs:
                o = jnp.where(head0, outs[2 * p], outs[2 * p + 1])
                g = g_ref[rb, psl[p]]
                yatt_ref[rb, psl[p]] = (o * (g * _sigmoid(g))).astype(yatt_ref.dtype)
            yield
        kprev_ref[...] = kp
        vprev_ref[...] = vp

    proj = project_previous()
    n_levels = 9 * (tm // bq)
    every = n_levels // (D_MODEL // MXU_WIDTH)
    for n, _ in enumerate(attend()):
        if n % every == every - 1:
            next(proj, None)
    for _ in proj:
        pass


def _swa_out(z_att, bucket, sinks, rel_bias, qnw, knw, x2, y_rwkv, w_r, w_a, tm):
    s, d = x2.shape
    bq = BLOCK
    nb = s // tm
    smem = pl.BlockSpec(memory_space=pltpu.SMEM)
    cur = lambda i: jnp.minimum(i, nb - 1)
    prev = lambda i: jnp.maximum(i - 1, 0)
    kv_col = 2 * D_ATT // D_KV
    return pl.pallas_call(
        _swa_out_kernel,
        grid=(nb + 1,),
        in_specs=[
            smem, smem,
            pl.BlockSpec((bq, 2 * bq), lambda i: (0, 0)),
            pl.BlockSpec((tm, D_ATT), lambda i: (cur(i), 0)),
            pl.BlockSpec((tm, D_ATT), lambda i: (cur(i), 1)),
            pl.BlockSpec((tm, D_KV), lambda i: (cur(i), kv_col)),
            pl.BlockSpec((tm, D_KV), lambda i: (cur(i), kv_col + 1)),
            pl.BlockSpec((1, D_ATT), lambda i: (0, 0)),
            pl.BlockSpec((1, D_KV), lambda i: (0, 0)),
            pl.BlockSpec((tm, d), lambda i: (prev(i), 0)),
            pl.BlockSpec((tm, D_RWKV), lambda i: (prev(i), 0)),
            pl.BlockSpec((D_RWKV, d), lambda i: (0, 0)),
            pl.BlockSpec((D_ATT, d), lambda i: (0, 0)),
        ],
        out_specs=pl.BlockSpec((tm, d), lambda i: (prev(i), 0)),
        out_shape=jax.ShapeDtypeStruct((s, d), F32),
        scratch_shapes=[
            pltpu.VMEM((2, N_Q_HEADS, bq, 2 * bq), F32),
            pltpu.VMEM((bq, D_KV), F32),
            pltpu.VMEM((bq, D_KV), F32),
            pltpu.VMEM((tm, D_ATT), BF16),
        ],
        compiler_params=pltpu.CompilerParams(
            dimension_semantics=("arbitrary",), vmem_limit_bytes=VMEM_LIMIT),
        name="swa_out",
    )(sinks, rel_bias, bucket, z_att, z_att, z_att, z_att, qnw, knw, x2, y_rwkv, w_r, w_a)


def _t5_bucket_table():
    dist = BLOCK + np.arange(BLOCK)[:, None] - np.arange(2 * BLOCK)[None, :]
    n = np.maximum(dist, 0)
    nf = np.maximum(n, 1).astype(np.float64)
    large = MAX_EXACT + (np.log(nf / MAX_EXACT) / math.log(MAX_DISTANCE / MAX_EXACT)
                         * (N_BUCKETS - MAX_EXACT)).astype(np.int32)
    large = np.minimum(large, N_BUCKETS - 1)
    return np.where(n < MAX_EXACT, n, large).astype(np.int32)


def kernel(x, norm_w, w_in, w_out, mu_rwkv, w0, w2, a0, a2, k_k, k_a, r_k, lnx_w, lnx_b,
           q_norm_w, k_norm_w, sinks, rel_bias):
    b, s, d = x.shape
    assert (b, s, d) == (1, SEQ, D_MODEL) and norm_w.shape[0] == 1
    x2 = x.reshape(s, d)
    l = 0
    row = lambda t: t.reshape(1, -1).astype(F32)

    w_in_l = w_in[l]
    wa = w_in_l[:, RWKV_COLS:]
    w_att = jnp.concatenate([wa[:, :D_ATT], wa[:, D_ATT + 2 * D_KV:], wa[:, D_ATT:D_ATT + 2 * D_KV]], axis=1)
    zeros_l = jnp.zeros((LORA, D_RWKV), F32)
    w2p = jnp.concatenate([w2[l], zeros_l], axis=0).astype(BF16)
    a2p = jnp.concatenate([zeros_l, a2[l]], axis=0).astype(BF16)
    w_out_r = w_out[l][:D_RWKV].astype(BF16)
    w_out_a = w_out[l][D_RWKV:].astype(BF16)
    qnw = jnp.tile(q_norm_w[l] * (HEAD_DIM ** -0.5 * LOG2E), N_Q_HEADS).reshape(1, D_ATT)
    knw = jnp.tile(k_norm_w[l], N_KV_HEADS).reshape(1, D_KV)
    bucket = jnp.asarray(_t5_bucket_table())

    z_rwkv, h, rs = _proj_in(x2, row(norm_w[l]), w_in_l, RWKV_COLS, row(mu_rwkv[l]), 512, 1408)
    z_att, *prep = _att_prep(h, rs, w_att, z_rwkv, row(w0[l]), w2p, row(a0[l]), a2p, row(k_k[l]), row(k_a[l]),
                             row(r_k[l]), 512, 1152, 256)
    y_rwkv = _rwkv(prep, z_rwkv, row(lnx_w[l]), row(lnx_b[l]))
    out = _swa_out(z_att, bucket, sinks[l].astype(F32), rel_bias.astype(F32), qnw, knw,
                   x2, y_rwkv, w_out_r, w_out_a, 512)
    return out.reshape(b, s, d)
```

```python
import math

import jax
import jax.numpy as jnp
import numpy as np
from jax import lax
from jax.experimental import pallas as pl
from jax.experimental.pallas import tpu as pltpu

F32 = jnp.float32
BF16 = jnp.bfloat16

D_MODEL = 2048
SEQ = 8192
HEAD_DIM = 64
D_RWKV = 1024
D_ATT = 1024
LORA = 64
N_Q_HEADS = 16
N_KV_HEADS = 2
D_KV = N_KV_HEADS * HEAD_DIM
WINDOW = 128
BLOCK = 128
N_BUCKETS = 32
MAX_EXACT = N_BUCKETS // 2
MAX_DISTANCE = 128
NORM_EPS = 1e-6
LNX_EPS = 64e-5
RWKV_COLS = 4 * D_RWKV + 2 * LORA
ATT_COLS = 2 * D_ATT + 2 * D_KV

LANES = 128
MXU_WIDTH = 256
N_PAIRS = D_RWKV // LANES
CHUNK = 64
RWKV_CHUNKS_PER_GROUP = 4
RWKV_CHUNKS_PER_STEP = 16
NEG_BIG = -1e30
LOG2E = math.log2(math.e)

VMEM_LIMIT = 56 * 1024 * 1024


def _dot(a, b):
    return jnp.dot(a, b, preferred_element_type=F32)


def _dot_nt(a, b):
    return lax.dot_general(a, b, (((1,), (1,)), ((), ())), preferred_element_type=F32)


def _split3(x):
    hi = x.astype(BF16)
    r1 = x - hi.astype(F32)
    mid = r1.astype(BF16)
    lo = (r1 - mid.astype(F32)).astype(BF16)
    return hi, mid, lo


def _seg_sum(x, seg_ones):
    hi = x.astype(BF16)
    lo = (x - hi.astype(F32)).astype(BF16)
    return _dot(hi, seg_ones) + _dot(lo, seg_ones)


def _dot_exact_lhs(m, x):
    hi, mid, lo = _split3(x)
    return _dot(m, hi) + _dot(m, mid) + _dot(m, lo)


def _sigmoid(x):
    return 1.0 / (1.0 + jnp.exp(-x))


def _proj_in_kernel(x_ref, nw_ref, w_ref, mu_ref, o_ref, h_ref, rs_ref, wb_ref, prev_ref):
    first_col_block = pl.program_id(0) == 0

    @pl.when(pl.program_id(1) == 0)
    def _():
        wb_ref[...] = w_ref[...].astype(BF16)
        prev_ref[...] = jnp.zeros_like(prev_ref)

    tm, tn = o_ref.shape
    x = x_ref[...]
    h = (x * nw_ref[...]).astype(BF16)
    rs = lax.rsqrt(jnp.mean(x * x, axis=-1, keepdims=True) + NORM_EPS)
    for c0 in range(0, tn, MXU_WIDTH):
        cs = slice(c0, min(c0 + MXU_WIDTH, tn))
        z = _dot(h, wb_ref[:, cs]) * rs
        row = lax.broadcasted_iota(jnp.int32, z.shape, 0)
        zprev = jnp.where(row == 0, prev_ref[:, cs], pltpu.roll(z, 1, axis=0))
        prev_ref[:, cs] = z[tm - 1:tm, :]
        o_ref[:, cs] = z + (zprev - z) * mu_ref[:, cs]

    @pl.when(first_col_block)
    def _():
        h_ref[...] = h
        rs_ref[...] = rs


def _proj_in(x2, norm_w, w, n, mu, tm, tn):
    s, d = x2.shape
    ni = s // tm
    once = lambda j, i: (jnp.where(j == 0, i, ni - 1), 0)
    return pl.pallas_call(
        _proj_in_kernel,
        grid=(n // tn, ni),
        in_specs=[
            pl.BlockSpec((tm, d), lambda j, i: (i, 0)),
            pl.BlockSpec((1, d), lambda j, i: (0, 0)),
            pl.BlockSpec((d, tn), lambda j, i: (0, j)),
            pl.BlockSpec((1, tn), lambda j, i: (0, j)),
        ],
        out_specs=[
            pl.BlockSpec((tm, tn), lambda j, i: (i, j)),
            pl.BlockSpec((tm, d), once),
            pl.BlockSpec((tm, 1), once),
        ],
        out_shape=[
            jax.ShapeDtypeStruct((s, n), F32),
            jax.ShapeDtypeStruct((s, d), BF16),
            jax.ShapeDtypeStruct((s, 1), F32),
        ],
        scratch_shapes=[pltpu.VMEM((d, tn), BF16), pltpu.VMEM((1, tn), F32)],
        compiler_params=pltpu.CompilerParams(
            dimension_semantics=("arbitrary", "arbitrary"), vmem_limit_bytes=VMEM_LIMIT),
        name="proj_in_rwkv",
    )(x2, norm_w, w, mu)


def _att_prep_kernel(h_ref, rs_ref, w_ref, r_ref, k_ref, v_ref, lora_ref, w0_ref, w2_ref, a0_ref, a2_ref,
                     kk_ref, ka_ref, rk_ref,
                     o_ref, rt_ref, kt_ref, at_ref, bt_ref, vb_ref, bonus_ref, el_ref, wb_ref):
    c = CHUNK

    @pl.when(pl.program_id(1) == 0)
    def _():
        wb_ref[...] = w_ref[...].astype(BF16)

    tm, tn = o_ref.shape
    h = h_ref[...]
    rs = rs_ref[...]

    def project():
        for c0 in range(0, tn, MXU_WIDTH):
            cs = slice(c0, min(c0 + MXU_WIDTH, tn))
            o_ref[:, cs] = _dot(h, wb_ref[:, cs]) * rs
            yield

    r128 = lax.broadcasted_iota(jnp.int32, (LANES, LANES), 0)
    c128 = lax.broadcasted_iota(jnp.int32, (LANES, LANES), 1)
    seg_ones = jnp.where((r128 // HEAD_DIM) == (c128 // HEAD_DIM), 1.0, 0.0).astype(BF16)
    ti = lax.broadcasted_iota(jnp.int32, (c, c), 0)
    si = lax.broadcasted_iota(jnp.int32, (c, c), 1)
    tril_ones = jnp.where(si <= ti, 1.0, 0.0).astype(BF16)
    pairs = range(N_PAIRS)
    psl = [slice(p * LANES, (p + 1) * LANES) for p in pairs]

    def seg_sum_pairs(y):
        z = _seg_sum(jnp.concatenate([y[:, s] for s in psl], axis=0), seg_ones)
        return jnp.concatenate([z[p * c:(p + 1) * c] for p in pairs], axis=1)

    def prepare():
        lora_in = lora_ref[...]
        u_lin = _dot(jnp.tanh(lora_in).astype(BF16), w2_ref[...])
        a_lin = _dot(lora_in.astype(BF16), a2_ref[...])
        yield
        for ci in range(r_ref.shape[0] // c):
            rows = slice(ci * c, (ci + 1) * c)
            logw = -math.exp(-0.5) * _sigmoid(w0_ref[...] + u_lin[rows, :])
            av = _sigmoid(a0_ref[...] + a_lin[rows, :])
            logp = _dot_exact_lhs(tril_ones, logw)
            e_p = jnp.exp(logp)
            e_n = 1.0 / e_p
            el_ref[0, ci:ci + 1, :] = e_p[c - 1:c, :]
            yield
            r = r_ref[rows, :]
            k = k_ref[rows, :]
            v = v_ref[rows, :]
            kmod = k * (1.0 + (av - 1.0) * ka_ref[...])
            rt_ref[rows, :] = (r * e_p).astype(rt_ref.dtype)
            kt_ref[rows, :] = (kmod * e_n).astype(kt_ref.dtype)
            vb_ref[rows, :] = v.astype(vb_ref.dtype)
            bonus_ref[rows, :] = seg_sum_pairs(r * kmod * rk_ref[...]) * v
            yield
            kk = k * kk_ref[...]
            kkn = kk * lax.rsqrt(jnp.maximum(seg_sum_pairs(kk * kk), 1e-24))
            at_ref[rows, :] = (-kkn * jnp.exp(logp - logw)).astype(at_ref.dtype)
            bt_ref[rows, :] = (kkn * av * e_n).astype(bt_ref.dtype)
            yield

    proj = project()
    n_slices = 1 + 3 * (r_ref.shape[0] // c)
    every = max(1, n_slices // (-(-tn // MXU_WIDTH)))
    for n, _ in enumerate(prepare()):
        if n % every == every - 1:
            next(proj, None)
    for _ in proj:
        pass


def _att_prep(h, rs, w_att, z_rwkv, w0, w2p, a0, a2p, k_k, k_a, r_k, tm, tn, tp):
    s, d = h.shape
    n = w_att.shape[1]
    nj, ni = n // tn, s // tm
    assert nj * ni * tp == s and tp % CHUNK == 0
    blk = lambda j, i: j * ni + i
    row = pl.BlockSpec((1, D_RWKV), lambda j, i: (0, 0))
    lora_w = pl.BlockSpec((2 * LORA, D_RWKV), lambda j, i: (0, 0))
    zcol = lambda cb: pl.BlockSpec((tp, D_RWKV), lambda j, i: (blk(j, i), cb))
    tok = pl.BlockSpec((tp, D_RWKV), lambda j, i: (blk(j, i), 0))
    sds = lambda dt: jax.ShapeDtypeStruct((s, D_RWKV), dt)
    return pl.pallas_call(
        _att_prep_kernel,
        grid=(nj, ni),
        in_specs=[
            pl.BlockSpec((tm, d), lambda j, i: (i, 0)),
            pl.BlockSpec((tm, 1), lambda j, i: (i, 0)),
            pl.BlockSpec((d, tn), lambda j, i: (0, j)),
            zcol(0), zcol(1), zcol(2),
            pl.BlockSpec((tp, 2 * LORA), lambda j, i: (blk(j, i), 4 * D_RWKV // (2 * LORA))),
            row, lora_w, row, lora_w, row, row, row,
        ],
        out_specs=[
            pl.BlockSpec((tm, tn), lambda j, i: (i, j)),
            tok, tok, tok, tok, tok, tok,
            pl.BlockSpec((1, tp // CHUNK, D_RWKV), lambda j, i: (blk(j, i), 0, 0)),
        ],
        out_shape=[
            jax.ShapeDtypeStruct((s, n), F32),
            sds(BF16), sds(BF16), sds(BF16), sds(BF16), sds(BF16), sds(F32),
            jax.ShapeDtypeStruct((s // tp, tp // CHUNK, D_RWKV), F32),
        ],
        scratch_shapes=[pltpu.VMEM((d, tn), BF16)],
        compiler_params=pltpu.CompilerParams(
            dimension_semantics=("arbitrary", "arbitrary"), vmem_limit_bytes=VMEM_LIMIT),
        name="proj_att_rwkv_prep",
    )(h, rs, w_att, z_rwkv, z_rwkv, z_rwkv, z_rwkv, w0, w2p, a0, a2p, k_k, k_a, r_k)


def _rwkv_kernel(rt_ref, kt_ref, at_ref, bt_ref, v_ref, bonus_ref, el_ref, g_ref,
                 lw_ref, lb_ref, o_ref, gt_ref):
    c = CHUNK
    nch = RWKV_CHUNKS_PER_STEP
    i = pl.program_id(0)

    @pl.when(i == 0)
    def _():
        gt_ref[...] = jnp.zeros_like(gt_ref)

    lane = lax.broadcasted_iota(jnp.int32, (c, LANES), 1)
    head0 = lane < HEAD_DIM
    trow = lax.broadcasted_iota(jnp.int32, (c, LANES), 0)
    eye_cat = jnp.where((lane % HEAD_DIM) == trow, 1.0, 0.0)
    t2 = lax.broadcasted_iota(jnp.int32, (c, 2 * LANES), 0)
    s2 = lax.broadcasted_iota(jnp.int32, (c, 2 * LANES), 1) % HEAD_DIM
    strict2 = s2 < t2
    incl2 = s2 <= t2
    r128 = lax.broadcasted_iota(jnp.int32, (LANES, LANES), 0)
    c128 = lax.broadcasted_iota(jnp.int32, (LANES, LANES), 1)
    same_head = (r128 // HEAD_DIM) == (c128 // HEAD_DIM)
    zeros_c = jnp.zeros((c, LANES), F32)

    pairs = range(N_PAIRS)
    psl = [slice(p * LANES, (p + 1) * LANES) for p in pairs]

    def stack2(y):
        return jnp.concatenate([jnp.where(head0, y, 0.0), jnp.where(head0, 0.0, y)], axis=0)

    def catmul(a_cat, y):
        return _dot(a_cat.astype(BF16), stack2(y).astype(BF16))

    seg_mean = jnp.where(same_head, 1.0 / HEAD_DIM, 0.0).astype(BF16)

    def seg_sum_pairs(x, seg):
        y = _seg_sum(jnp.concatenate([x[:, s] for s in psl], axis=0), seg)
        return jnp.concatenate([y[p * c:(p + 1) * c] for p in pairs], axis=1)

    el_chunks = el_ref.shape[1]

    def intra(chunks, out):
        per_item = lambda ref: [ref[ci * c:(ci + 1) * c, s].astype(F32) for ci in chunks for s in psl]
        a_t, b_t, r_t, k_t, v_i = (per_item(ref) for ref in (at_ref, bt_ref, rt_ref, kt_ref, v_ref))
        e_last = [el_ref[ci // el_chunks, ci % el_chunks:ci % el_chunks + 1, s]
                  for ci in chunks for s in psl]
        sc = [_dot_nt(jnp.concatenate([a, r], axis=0).astype(BF16),
                      jnp.concatenate([stack2(b), stack2(k)], axis=0).astype(BF16))
              for a, r, b, k in zip(a_t, r_t, b_t, k_t)]
        yield
        l_all = [jnp.where(strict2, x[0:c, :], 0.0) for x in sc]
        a_rbk = [jnp.where(incl2, x[c:, :], 0.0) for x in sc]
        x = [l[:, 0:LANES] for l in l_all]
        tinv = [eye_cat + xi for xi in x]
        x = [catmul(xi, xi) for xi in x]
        yield
        for _ in range(4):
            tx = [catmul(jnp.concatenate([ti_, xi], axis=0), xi) for ti_, xi in zip(tinv, x)]
            tinv = [ti_ + y[0:c] for ti_, y in zip(tinv, tx)]
            x = [y[c:] for y in tx]
            yield
        tinv = [ti_ + catmul(ti_, xi) for ti_, xi in zip(tinv, x)]
        yield
        a_eff = [catmul(ti_, a) for ti_, a in zip(tinv, a_t)]
        yield
        lakv = [catmul(l[:, LANES:], v) for l, v in zip(l_all, v_i)]
        yield
        w_loc = [catmul(ti_, y) for ti_, y in zip(tinv, lakv)]
        yield
        q_eff = [r + catmul(a[:, 0:LANES], ae) for r, a, ae in zip(r_t, a_rbk, a_eff)]
        yield
        o_loc = [_dot(a.astype(BF16), jnp.concatenate([stack2(w), stack2(v)], axis=0).astype(BF16))
                 for a, w, v in zip(a_rbk, w_loc, v_i)]
        yield
        bh = [b * el for b, el in zip(b_t, e_last)]
        kh = [k * el for k, el in zip(k_t, e_last)]
        m_mat = [jnp.where(same_head, _dot(jnp.concatenate([ae, zeros_c], axis=0).T.astype(BF16),
                                           jnp.concatenate([b, zeros_c], axis=0).astype(BF16)), 0.0)
                 for ae, b in zip(a_eff, bh)]
        yield
        n_mat = [jnp.where(same_head, _dot(jnp.concatenate([w, v], axis=0).T.astype(BF16),
                                           jnp.concatenate([b, k], axis=0).astype(BF16)), 0.0)
                 for w, v, b, k in zip(w_loc, v_i, bh, kh)]
        out.update(q_eff=q_eff, o_loc=o_loc, m_mat=m_mat, n_mat=n_mat, e_last=e_last)

    def advance(gts, ic, j):
        sel = lambda name: ic[name][j * N_PAIRS:(j + 1) * N_PAIRS]
        gtb = [g.astype(BF16) for g in gts]
        o = [_dot_nt(q.astype(BF16), g) + ol for q, g, ol in zip(sel("q_eff"), gtb, sel("o_loc"))]
        gts = [g * el + _dot(gb, m.astype(BF16)) + n
               for g, gb, el, m, n in zip(gts, gtb, sel("e_last"), sel("m_mat"), sel("n_mat"))]
        return gts, jnp.concatenate(o, axis=1)

    def finish(ci, o):
        rs = slice(ci * c, (ci + 1) * c)
        d = o - seg_sum_pairs(o, seg_mean)
        var = seg_sum_pairs(d * d, seg_mean)
        on = d * lax.rsqrt(var + LNX_EPS) * lw_ref[...] + lb_ref[...]
        g = g_ref[rs, :]
        o_ref[rs, :] = ((on + bonus_ref[rs, :]) * (g * _sigmoid(g))).astype(o_ref.dtype)

    gsz = RWKV_CHUNKS_PER_GROUP
    ngroups = nch // gsz
    state = dict(gts=[gt_ref[p] for p in pairs])

    def tail_group(g, ic):
        for j in range(gsz):
            state["gts"], o = advance(state["gts"], ic, j)
            yield
            finish(g * gsz + j, o)
            yield

    def drive(main, sides):
        for _ in main:
            for s_ in sides:
                next(s_, None)
        for s_ in sides:
            for _ in s_:
                pass

    results = {}
    for g in range(ngroups):
        results[g] = {}
        sides = [tail_group(g - 1, results[g - 1])] if g >= 1 else []
        drive(intra(range(g * gsz, (g + 1) * gsz), results[g]), sides)
    drive(tail_group(ngroups - 1, results[ngroups - 1]), [])
    for p in pairs:
        gt_ref[p] = state["gts"][p]


def _rwkv(prep, z_rwkv, lnx_w, lnx_b):
    s = z_rwkv.shape[0]
    t = CHUNK * RWKV_CHUNKS_PER_STEP
    el = prep[-1]
    blocks_per_step = RWKV_CHUNKS_PER_STEP // el.shape[1]
    tok = pl.BlockSpec((t, D_RWKV), lambda i: (i, 0))
    row = pl.BlockSpec((1, D_RWKV), lambda i: (0, 0))
    return pl.pallas_call(
        _rwkv_kernel,
        grid=(s // t,),
        in_specs=[
            tok, tok, tok, tok, tok, tok,
            pl.BlockSpec((blocks_per_step, el.shape[1], D_RWKV), lambda i: (i, 0, 0)),
            pl.BlockSpec((t, D_RWKV), lambda i: (i, 3)),
            row, row,
        ],
        out_specs=tok,
        out_shape=jax.ShapeDtypeStruct((s, D_RWKV), BF16),
        scratch_shapes=[
            pltpu.VMEM((N_PAIRS, LANES, LANES), F32),
        ],
        compiler_params=pltpu.CompilerParams(
            dimension_semantics=("arbitrary",), vmem_limit_bytes=VMEM_LIMIT),
        name="rwkv7",
    )(*prep, z_rwkv, lnx_w, lnx_b)


def _swa_out_kernel(sinks_ref, relb_ref, bucket_ref, q_ref, g_ref, k_ref, v_ref, qnw_ref, knw_ref,
                    x_ref, yr_ref, wr_ref, wa_ref, o_ref, bias_ref, kprev_ref, vprev_ref, yatt_ref):
    i = pl.program_id(0)
    bq = BLOCK
    tm = q_ref.shape[0]

    @pl.when(i == 0)
    def _():
        kprev_ref[...] = jnp.zeros_like(kprev_ref)
        vprev_ref[...] = jnp.zeros_like(vprev_ref)
        yatt_ref[...] = jnp.zeros_like(yatt_ref)
        bucket = bucket_ref[...]
        qi = lax.broadcasted_iota(jnp.int32, (bq, 2 * bq), 0)
        kj = lax.broadcasted_iota(jnp.int32, (bq, 2 * bq), 1)
        dist = bq + qi - kj
        inwin = (dist >= 0) & (dist < WINDOW)
        for h in range(N_Q_HEADS):
            acc = jnp.zeros((bq, 2 * bq), F32)
            for b in range(N_BUCKETS):
                acc = jnp.where(bucket == b, relb_ref[b, h], acc)
            acc = jnp.where(inwin, acc * LOG2E, NEG_BIG)
            bias_ref[0, h] = acc
            bias_ref[1, h] = jnp.where(kj >= bq, acc, NEG_BIG)

    first_step = jnp.where(i == 0, 1, 0)
    lane = lax.broadcasted_iota(jnp.int32, (bq, LANES), 1)
    head0 = lane < HEAD_DIM
    lane2 = lax.broadcasted_iota(jnp.int32, (2 * bq, LANES), 1)
    head0_2 = lane2 < HEAD_DIM
    r128 = lax.broadcasted_iota(jnp.int32, (LANES, LANES), 0)
    c128 = lax.broadcasted_iota(jnp.int32, (LANES, LANES), 1)
    seg_mean = jnp.where((r128 // HEAD_DIM) == (c128 // HEAD_DIM), 1.0 / HEAD_DIM, 0.0).astype(BF16)
    sink2 = [sinks_ref[h] * LOG2E for h in range(N_Q_HEADS)]
    pairs = range(N_PAIRS)
    heads = range(N_Q_HEADS)
    psl = [slice(p * LANES, (p + 1) * LANES) for p in pairs]
    kvh = [(2 * p) // (N_Q_HEADS // N_KV_HEADS) for p in pairs]

    def rms_heads(t, w):
        return t * lax.rsqrt(_seg_sum(t * t, seg_mean) + NORM_EPS) * w

    ya_prev = yatt_ref[...]
    yr_prev = yr_ref[...]

    def project_previous():
        for c0 in range(0, D_MODEL, MXU_WIDTH):
            cs = slice(c0, c0 + MXU_WIDTH)
            o_ref[:, cs] = x_ref[:, cs] + _dot(yr_prev, wr_ref[:, cs]) + _dot(ya_prev, wa_ref[:, cs])
            yield

    def attend():
        kp, vp = kprev_ref[...], vprev_ref[...]
        for b in range(tm // bq):
            rb = slice(b * bq, (b + 1) * bq)
            first = first_step if b == 0 else 0
            kn = rms_heads(k_ref[rb, :], knw_ref[...])
            vc = v_ref[rb, :]
            kcat = jnp.concatenate([kp, kn], axis=0)
            vcat = jnp.concatenate([vp, vc], axis=0)
            kp, vp = kn, vc
            krol = pltpu.roll(kcat, HEAD_DIM, axis=1)
            vrol = pltpu.roll(vcat, HEAD_DIM, axis=1)
            kdup = [jnp.where(head0_2, kcat, krol).astype(BF16), jnp.where(head0_2, krol, kcat).astype(BF16)]
            vdup = [jnp.where(head0_2, vcat, vrol).astype(BF16), jnp.where(head0_2, vrol, vcat).astype(BF16)]
            yield
            qn = [rms_heads(q_ref[rb, s], qnw_ref[:, s]) for s in psl]
            yield
            q2 = [jnp.concatenate([jnp.where(head0, q, 0.0), jnp.where(head0, 0.0, q)], axis=0).astype(BF16)
                  for q in qn]
            lg2 = [_dot_nt(q, kdup[kv]) for q, kv in zip(q2, kvh)]
            yield
            lg = [lg2[h // 2][(h % 2) * bq:(h % 2 + 1) * bq, :] + bias_ref[first, h] for h in heads]
            yield
            m = [jnp.maximum(jnp.max(l, axis=-1, keepdims=True), sink2[h]) for h, l in zip(heads, lg)]
            yield
            e = [jnp.exp2(l - mm) for l, mm in zip(lg, m)]
            yield
            denom = [jnp.sum(ee, axis=-1, keepdims=True) + jnp.exp2(sink2[h] - mm)
                     for h, ee, mm in zip(heads, e, m)]
            yield
            pv = [_dot(ee.astype(BF16), vdup[kvh[h // 2]]) for h, ee in zip(heads, e)]
            yield
            outs = [x / d for x, d in zip(pv, denom)]
            for p in pairs:
                o = jnp.where(head0, outs[2 * p], outs[2 * p + 1])
                g = g_ref[rb, psl[p]]
                yatt_ref[rb, psl[p]] = (o * (g * _sigmoid(g))).astype(yatt_ref.dtype)
            yield
        kprev_ref[...] = kp
        vprev_ref[...] = vp

    proj = project_previous()
    n_levels = 9 * (tm // bq)
    every = n_levels // (D_MODEL // MXU_WIDTH)
    for n, _ in enumerate(attend()):
        if n % every == every - 1:
            next(proj, None)
    for _ in proj:
        pass


def _swa_out(z_att, bucket, sinks, rel_bias, qnw, knw, x2, y_rwkv, w_r, w_a, tm):
    s, d = x2.shape
    bq = BLOCK
    nb = s // tm
    smem = pl.BlockSpec(memory_space=pltpu.SMEM)
    cur = lambda i: jnp.minimum(i, nb - 1)
    prev = lambda i: jnp.maximum(i - 1, 0)
    kv_col = 2 * D_ATT // D_KV
    return pl.pallas_call(
        _swa_out_kernel,
        grid=(nb + 1,),
        in_specs=[
            smem, smem,
            pl.BlockSpec((bq, 2 * bq), lambda i: (0, 0)),
            pl.BlockSpec((tm, D_ATT), lambda i: (cur(i), 0)),
            pl.BlockSpec((tm, D_ATT), lambda i: (cur(i), 1)),
            pl.BlockSpec((tm, D_KV), lambda i: (cur(i), kv_col)),
            pl.BlockSpec((tm, D_KV), lambda i: (cur(i), kv_col + 1)),
            pl.BlockSpec((1, D_ATT), lambda i: (0, 0)),
            pl.BlockSpec((1, D_KV), lambda i: (0, 0)),
            pl.BlockSpec((tm, d), lambda i: (prev(i), 0)),
            pl.BlockSpec((tm, D_RWKV), lambda i: (prev(i), 0)),
            pl.BlockSpec((D_RWKV, d), lambda i: (0, 0)),
            pl.BlockSpec((D_ATT, d), lambda i: (0, 0)),
        ],
        out_specs=pl.BlockSpec((tm, d), lambda i: (prev(i), 0)),
        out_shape=jax.ShapeDtypeStruct((s, d), F32),
        scratch_shapes=[
            pltpu.VMEM((2, N_Q_HEADS, bq, 2 * bq), F32),
            pltpu.VMEM((bq, D_KV), F32),
            pltpu.VMEM((bq, D_KV), F32),
            pltpu.VMEM((tm, D_ATT), BF16),
        ],
        compiler_params=pltpu.CompilerParams(
            dimension_semantics=("arbitrary",), vmem_limit_bytes=VMEM_LIMIT),
        name="swa_out",
    )(sinks, rel_bias, bucket, z_att, z_att, z_att, z_att, qnw, knw, x2, y_rwkv, w_r, w_a)


def _t5_bucket_table():
    dist = BLOCK + np.arange(BLOCK)[:, None] - np.arange(2 * BLOCK)[None, :]
    n = np.maximum(dist, 0)
    nf = np.maximum(n, 1).astype(np.float64)
    large = MAX_EXACT + (np.log(nf / MAX_EXACT) / math.log(MAX_DISTANCE / MAX_EXACT)
                         * (N_BUCKETS - MAX_EXACT)).astype(np.int32)
    large = np.minimum(large, N_BUCKETS - 1)
    return np.where(n < MAX_EXACT, n, large).astype(np.int32)


def kernel(x, norm_w, w_in, w_out, mu_rwkv, w0, w2, a0, a2, k_k, k_a, r_k, lnx_w, lnx_b,
           q_norm_w, k_norm_w, sinks, rel_bias):
    b, s, d = x.shape
    assert (b, s, d) == (1, SEQ, D_MODEL) and norm_w.shape[0] == 1
    x2 = x.reshape(s, d)
    l = 0
    row = lambda t: t.reshape(1, -1).astype(F32)

    w_in_l = w_in[l]
    wa = w_in_l[:, RWKV_COLS:]
    w_att = jnp.concatenate([wa[:, :D_ATT], wa[:, D_ATT + 2 * D_KV:], wa[:, D_ATT:D_ATT + 2 * D_KV]], axis=1)
    zeros_l = jnp.zeros((LORA, D_RWKV), F32)
    w2p = jnp.concatenate([w2[l], zeros_l], axis=0).astype(BF16)
    a2p = jnp.concatenate([zeros_l, a2[l]], axis=0).astype(BF16)
    w_out_r = w_out[l][:D_RWKV].astype(BF16)
    w_out_a = w_out[l][D_RWKV:].astype(BF16)
    qnw = jnp.tile(q_norm_w[l] * (HEAD_DIM ** -0.5 * LOG2E), N_Q_HEADS).reshape(1, D_ATT)
    knw = jnp.tile(k_norm_w[l], N_KV_HEADS).reshape(1, D_KV)
    bucket = jnp.asarray(_t5_bucket_table())

    z_rwkv, h, rs = _proj_in(x2, row(norm_w[l]), w_in_l, RWKV_COLS, row(mu_rwkv[l]), 512, 1408)
    z_att, *prep = _att_prep(h, rs, w_att, z_rwkv, row(w0[l]), w2p, row(a0[l]), a2p, row(k_k[l]), row(k_a[l]),
                             row(r_k[l]), 512, 1152, 256)
    y_rwkv = _rwkv(prep, z_rwkv, row(lnx_w[l]), row(lnx_b[l]))
    out = _swa_out(z_att, bucket, sinks[l].astype(F32), rel_bias.astype(F32), qnw, knw,
                   x2, y_rwkv, w_out_r, w_out_a, 512)
    return out.reshape(b, s, d)
```

```python
import math

import jax
import jax.numpy as jnp
import numpy as np
from jax import lax
from jax.experimental import pallas as pl
from jax.experimental.pallas import tpu as pltpu

F32 = jnp.float32
BF16 = jnp.bfloat16

D_MODEL = 2048
SEQ = 8192
HEAD_DIM = 64
D_RWKV = 1024
D_ATT = 1024
LORA = 64
N_Q_HEADS = 16
N_KV_HEADS = 2
D_KV = N_KV_HEADS * HEAD_DIM
WINDOW = 128
BLOCK = 128
N_BUCKETS = 32
MAX_EXACT = N_BUCKETS // 2
MAX_DISTANCE = 128
NORM_EPS = 1e-6
LNX_EPS = 64e-5
RWKV_COLS = 4 * D_RWKV + 2 * LORA
ATT_COLS = 2 * D_ATT + 2 * D_KV

LANES = 128
MXU_WIDTH = 256
N_PAIRS = D_RWKV // LANES
CHUNK = 64
RWKV_CHUNKS_PER_GROUP = 4
RWKV_CHUNKS_PER_STEP = 8
NEG_BIG = -1e30
LOG2E = math.log2(math.e)

VMEM_LIMIT = 56 * 1024 * 1024


def _dot(a, b):
    return jnp.dot(a, b, preferred_element_type=F32)


def _dot_nt(a, b):
    return lax.dot_general(a, b, (((1,), (1,)), ((), ())), preferred_element_type=F32)


def _split3(x):
    hi = x.astype(BF16)
    r1 = x - hi.astype(F32)
    mid = r1.astype(BF16)
    lo = (r1 - mid.astype(F32)).astype(BF16)
    return hi, mid, lo


def _seg_sum(x, seg_ones):
    hi = x.astype(BF16)
    lo = (x - hi.astype(F32)).astype(BF16)
    return _dot(hi, seg_ones) + _dot(lo, seg_ones)


def _dot_exact_lhs(m, x):
    hi, mid, lo = _split3(x)
    return _dot(m, hi) + _dot(m, mid) + _dot(m, lo)


def _sigmoid(x):
    return 1.0 / (1.0 + jnp.exp(-x))


def _proj_in_kernel(x_ref, nw_ref, w_ref, mu_ref, o_ref, h_ref, rs_ref, wb_ref, prev_ref):
    first_col_block = pl.program_id(0) == 0

    @pl.when(pl.program_id(1) == 0)
    def _():
        wb_ref[...] = w_ref[...].astype(BF16)
        prev_ref[...] = jnp.zeros_like(prev_ref)

    tm, tn = o_ref.shape
    x = x_ref[...]
    h = (x * nw_ref[...]).astype(BF16)
    rs = lax.rsqrt(jnp.mean(x * x, axis=-1, keepdims=True) + NORM_EPS)
    for c0 in range(0, tn, MXU_WIDTH):
        cs = slice(c0, min(c0 + MXU_WIDTH, tn))
        z = _dot(h, wb_ref[:, cs]) * rs
        row = lax.broadcasted_iota(jnp.int32, z.shape, 0)
        zprev = jnp.where(row == 0, prev_ref[:, cs], pltpu.roll(z, 1, axis=0))
        prev_ref[:, cs] = z[tm - 1:tm, :]
        o_ref[:, cs] = z + (zprev - z) * mu_ref[:, cs]

    @pl.when(first_col_block)
    def _():
        h_ref[...] = h
        rs_ref[...] = rs


def _proj_in(x2, norm_w, w, n, mu, tm, tn):
    s, d = x2.shape
    ni = s // tm
    once = lambda j, i: (jnp.where(j == 0, i, ni - 1), 0)
    return pl.pallas_call(
        _proj_in_kernel,
        grid=(n // tn, ni),
        in_specs=[
            pl.BlockSpec((tm, d), lambda j, i: (i, 0)),
            pl.BlockSpec((1, d), lambda j, i: (0, 0)),
            pl.BlockSpec((d, tn), lambda j, i: (0, j)),
            pl.BlockSpec((1, tn), lambda j, i: (0, j)),
        ],
        out_specs=[
            pl.BlockSpec((tm, tn), lambda j, i: (i, j)),
            pl.BlockSpec((tm, d), once),
            pl.BlockSpec((tm, 1), once),
        ],
        out_shape=[
            jax.ShapeDtypeStruct((s, n), F32),
            jax.ShapeDtypeStruct((s, d), BF16),
            jax.ShapeDtypeStruct((s, 1), F32),
        ],
        scratch_shapes=[pltpu.VMEM((d, tn), BF16), pltpu.VMEM((1, tn), F32)],
        compiler_params=pltpu.CompilerParams(
            dimension_semantics=("arbitrary", "arbitrary"), vmem_limit_bytes=VMEM_LIMIT),
        name="proj_in_rwkv",
    )(x2, norm_w, w, mu)


def _att_prep_kernel(h_ref, rs_ref, wa_ref, wb_in_ref, wc_ref, r_ref, k_ref, v_ref, lora_ref,
                     w0_ref, w2_ref, a0_ref, a2_ref, kk_ref, ka_ref, rk_ref,
                     o_ref, rt_ref, kt_ref, at_ref, bt_ref, vb_ref, bonus_ref, el_ref, wb_ref):
    c = CHUNK

    @pl.when(pl.program_id(1) == 0)
    def _():
        wp = wa_ref.shape[1]
        for n, piece in enumerate((wa_ref, wb_in_ref, wc_ref)):
            wb_ref[:, n * wp:(n + 1) * wp] = piece[...].astype(BF16)

    tm, tn = o_ref.shape
    h = h_ref[...]
    rs = rs_ref[...]

    def project():
        for c0 in range(0, tn, MXU_WIDTH):
            cs = slice(c0, min(c0 + MXU_WIDTH, tn))
            o_ref[:, cs] = _dot(h, wb_ref[:, cs]) * rs
            yield

    r128 = lax.broadcasted_iota(jnp.int32, (LANES, LANES), 0)
    c128 = lax.broadcasted_iota(jnp.int32, (LANES, LANES), 1)
    seg_ones = jnp.where((r128 // HEAD_DIM) == (c128 // HEAD_DIM), 1.0, 0.0).astype(BF16)
    ti = lax.broadcasted_iota(jnp.int32, (c, c), 0)
    si = lax.broadcasted_iota(jnp.int32, (c, c), 1)
    tril_ones = jnp.where(si <= ti, 1.0, 0.0).astype(BF16)
    pairs = range(N_PAIRS)
    psl = [slice(p * LANES, (p + 1) * LANES) for p in pairs]

    def seg_sum_pairs(y):
        z = _seg_sum(jnp.concatenate([y[:, s] for s in psl], axis=0), seg_ones)
        return jnp.concatenate([z[p * c:(p + 1) * c] for p in pairs], axis=1)

    def prepare():
        lora_in = lora_ref[...]
        u_lin = _dot(jnp.tanh(lora_in).astype(BF16), w2_ref[...])
        a_lin = _dot(lora_in.astype(BF16), a2_ref[...])
        yield
        for ci in range(r_ref.shape[0] // c):
            rows = slice(ci * c, (ci + 1) * c)
            logw = -math.exp(-0.5) * _sigmoid(w0_ref[...] + u_lin[rows, :])
            av = _sigmoid(a0_ref[...] + a_lin[rows, :])
            logp = _dot_exact_lhs(tril_ones, logw)
            e_p = jnp.exp(logp)
            e_n = 1.0 / e_p
            el_ref[0, ci:ci + 1, :] = e_p[c - 1:c, :]
            yield
            r = r_ref[rows, :]
            k = k_ref[rows, :]
            v = v_ref[rows, :]
            kmod = k * (1.0 + (av - 1.0) * ka_ref[...])
            rt_ref[rows, :] = (r * e_p).astype(rt_ref.dtype)
            kt_ref[rows, :] = (kmod * e_n).astype(kt_ref.dtype)
            vb_ref[rows, :] = v.astype(vb_ref.dtype)
            bonus_ref[rows, :] = seg_sum_pairs(r * kmod * rk_ref[...]) * v
            yield
            kk = k * kk_ref[...]
            kkn = kk * lax.rsqrt(jnp.maximum(seg_sum_pairs(kk * kk), 1e-24))
            at_ref[rows, :] = (-kkn * jnp.exp(logp - logw)).astype(at_ref.dtype)
            bt_ref[rows, :] = (kkn * av * e_n).astype(bt_ref.dtype)
            yield

    proj = project()
    n_slices = 1 + 3 * (r_ref.shape[0] // c)
    every = max(1, n_slices // (-(-tn // MXU_WIDTH)))
    for n, _ in enumerate(prepare()):
        if n % every == every - 1:
            next(proj, None)
    for _ in proj:
        pass


def _att_prep(h, rs, w_in, z_rwkv, w0, w2p, a0, a2p, k_k, k_a, r_k, tm, tn, tp):
    s, d = h.shape
    n = ATT_COLS
    nj, ni = n // tn, s // tm
    wp = tn // 3
    assert nj * ni * tp == s and tp % CHUNK == 0 and RWKV_COLS % wp == 0 and wp % LANES == 0
    piece = lambda p: pl.BlockSpec((d, wp), lambda j, i: (0, RWKV_COLS // wp + 3 * j + p))
    blk = lambda j, i: j * ni + i
    row = pl.BlockSpec((1, D_RWKV), lambda j, i: (0, 0))
    lora_w = pl.BlockSpec((2 * LORA, D_RWKV), lambda j, i: (0, 0))
    zcol = lambda cb: pl.BlockSpec((tp, D_RWKV), lambda j, i: (blk(j, i), cb))
    tok = pl.BlockSpec((tp, D_RWKV), lambda j, i: (blk(j, i), 0))
    sds = lambda dt: jax.ShapeDtypeStruct((s, D_RWKV), dt)
    return pl.pallas_call(
        _att_prep_kernel,
        grid=(nj, ni),
        in_specs=[
            pl.BlockSpec((tm, d), lambda j, i: (i, 0)),
            pl.BlockSpec((tm, 1), lambda j, i: (i, 0)),
            piece(0), piece(1), piece(2),
            zcol(0), zcol(1), zcol(2),
            pl.BlockSpec((tp, 2 * LORA), lambda j, i: (blk(j, i), 4 * D_RWKV // (2 * LORA))),
            row, lora_w, row, lora_w, row, row, row,
        ],
        out_specs=[
            pl.BlockSpec((tm, tn), lambda j, i: (i, j)),
            tok, tok, tok, tok, tok, tok,
            pl.BlockSpec((1, tp // CHUNK, D_RWKV), lambda j, i: (blk(j, i), 0, 0)),
        ],
        out_shape=[
            jax.ShapeDtypeStruct((s, n), F32),
            sds(BF16), sds(BF16), sds(BF16), sds(BF16), sds(BF16), sds(F32),
            jax.ShapeDtypeStruct((s // tp, tp // CHUNK, D_RWKV), F32),
        ],
        scratch_shapes=[pltpu.VMEM((d, tn), BF16)],
        compiler_params=pltpu.CompilerParams(
            dimension_semantics=("arbitrary", "arbitrary"), vmem_limit_bytes=VMEM_LIMIT),
        name="proj_att_rwkv_prep",
    )(h, rs, w_in, w_in, w_in, z_rwkv, z_rwkv, z_rwkv, z_rwkv, w0, w2p, a0, a2p, k_k, k_a, r_k)


def _rwkv_kernel(rt_ref, kt_ref, at_ref, bt_ref, v_ref, bonus_ref, el_ref, g_ref,
                 lw_ref, lb_ref, o_ref, gt_ref):
    c = CHUNK
    nch = RWKV_CHUNKS_PER_STEP
    i = pl.program_id(0)

    @pl.when(i == 0)
    def _():
        gt_ref[...] = jnp.zeros_like(gt_ref)

    lane = lax.broadcasted_iota(jnp.int32, (c, LANES), 1)
    head0 = lane < HEAD_DIM
    trow = lax.broadcasted_iota(jnp.int32, (c, LANES), 0)
    eye_cat = jnp.where((lane % HEAD_DIM) == trow, 1.0, 0.0)
    t2 = lax.broadcasted_iota(jnp.int32, (c, 2 * LANES), 0)
    s2 = lax.broadcasted_iota(jnp.int32, (c, 2 * LANES), 1) % HEAD_DIM
    strict2 = s2 < t2
    incl2 = s2 <= t2
    r128 = lax.broadcasted_iota(jnp.int32, (LANES, LANES), 0)
    c128 = lax.broadcasted_iota(jnp.int32, (LANES, LANES), 1)
    same_head = (r128 // HEAD_DIM) == (c128 // HEAD_DIM)
    zeros_c = jnp.zeros((c, LANES), F32)

    pairs = range(N_PAIRS)
    psl = [slice(p * LANES, (p + 1) * LANES) for p in pairs]

    def stack2(y):
        return jnp.concatenate([jnp.where(head0, y, 0.0), jnp.where(head0, 0.0, y)], axis=0)

    def catmul(a_cat, y):
        return _dot(a_cat.astype(BF16), stack2(y).astype(BF16))

    seg_mean = jnp.where(same_head, 1.0 / HEAD_DIM, 0.0).astype(BF16)

    def seg_sum_pairs(x, seg):
        y = _seg_sum(jnp.concatenate([x[:, s] for s in psl], axis=0), seg)
        return jnp.concatenate([y[p * c:(p + 1) * c] for p in pairs], axis=1)

    el_chunks = el_ref.shape[1]

    def intra(chunks, out):
        per_item = lambda ref: [ref[ci * c:(ci + 1) * c, s].astype(F32) for ci in chunks for s in psl]
        a_t, b_t, r_t, k_t, v_i = (per_item(ref) for ref in (at_ref, bt_ref, rt_ref, kt_ref, v_ref))
        e_last = [el_ref[ci // el_chunks, ci % el_chunks:ci % el_chunks + 1, s]
                  for ci in chunks for s in psl]
        sc = [_dot_nt(jnp.concatenate([a, r], axis=0).astype(BF16),
                      jnp.concatenate([stack2(b), stack2(k)], axis=0).astype(BF16))
              for a, r, b, k in zip(a_t, r_t, b_t, k_t)]
        yield
        l_all = [jnp.where(strict2, x[0:c, :], 0.0) for x in sc]
        a_rbk = [jnp.where(incl2, x[c:, :], 0.0) for x in sc]
        x = [l[:, 0:LANES] for l in l_all]
        tinv = [eye_cat + xi for xi in x]
        x = [catmul(xi, xi) for xi in x]
        yield
        for _ in range(4):
            tx = [catmul(jnp.concatenate([ti_, xi], axis=0), xi) for ti_, xi in zip(tinv, x)]
            tinv = [ti_ + y[0:c] for ti_, y in zip(tinv, tx)]
            x = [y[c:] for y in tx]
            yield
        tinv = [ti_ + catmul(ti_, xi) for ti_, xi in zip(tinv, x)]
        yield
        a_eff = [catmul(ti_, a) for ti_, a in zip(tinv, a_t)]
        yield
        lakv = [catmul(l[:, LANES:], v) for l, v in zip(l_all, v_i)]
        yield
        w_loc = [catmul(ti_, y) for ti_, y in zip(tinv, lakv)]
        yield
        q_eff = [r + catmul(a[:, 0:LANES], ae) for r, a, ae in zip(r_t, a_rbk, a_eff)]
        yield
        o_loc = [_dot(a.astype(BF16), jnp.concatenate([stack2(w), stack2(v)], axis=0).astype(BF16))
                 for a, w, v in zip(a_rbk, w_loc, v_i)]
        yield
        bh = [b * el for b, el in zip(b_t, e_last)]
        kh = [k * el for k, el in zip(k_t, e_last)]
        m_mat = [jnp.where(same_head, _dot(jnp.concatenate([ae, zeros_c], axis=0).T.astype(BF16),
                                           jnp.concatenate([b, zeros_c], axis=0).astype(BF16)), 0.0)
                 for ae, b in zip(a_eff, bh)]
        yield
        n_mat = [jnp.where(same_head, _dot(jnp.concatenate([w, v], axis=0).T.astype(BF16),
                                           jnp.concatenate([b, k], axis=0).astype(BF16)), 0.0)
                 for w, v, b, k in zip(w_loc, v_i, bh, kh)]
        out.update(q_eff=q_eff, o_loc=o_loc, m_mat=m_mat, n_mat=n_mat, e_last=e_last)

    def advance(gts, ic, j):
        sel = lambda name: ic[name][j * N_PAIRS:(j + 1) * N_PAIRS]
        gtb = [g.astype(BF16) for g in gts]
        o = [_dot_nt(q.astype(BF16), g) + ol for q, g, ol in zip(sel("q_eff"), gtb, sel("o_loc"))]
        gts = [g * el + _dot(gb, m.astype(BF16)) + n
               for g, gb, el, m, n in zip(gts, gtb, sel("e_last"), sel("m_mat"), sel("n_mat"))]
        return gts, jnp.concatenate(o, axis=1)

    def finish(ci, o):
        rs = slice(ci * c, (ci + 1) * c)
        d = o - seg_sum_pairs(o, seg_mean)
        var = seg_sum_pairs(d * d, seg_mean)
        on = d * lax.rsqrt(var + LNX_EPS) * lw_ref[...] + lb_ref[...]
        g = g_ref[rs, :]
        o_ref[rs, :] = ((on + bonus_ref[rs, :]) * (g * _sigmoid(g))).astype(o_ref.dtype)

    gsz = RWKV_CHUNKS_PER_GROUP
    ngroups = nch // gsz
    state = dict(gts=[gt_ref[p] for p in pairs])

    def tail_group(g, ic):
        for j in range(gsz):
            state["gts"], o = advance(state["gts"], ic, j)
            yield
            finish(g * gsz + j, o)
            yield

    def drive(main, sides):
        for _ in main:
            for s_ in sides:
                next(s_, None)
        for s_ in sides:
            for _ in s_:
                pass

    results = {}
    for g in range(ngroups):
        results[g] = {}
        sides = [tail_group(g - 1, results[g - 1])] if g >= 1 else []
        drive(intra(range(g * gsz, (g + 1) * gsz), results[g]), sides)
    drive(tail_group(ngroups - 1, results[ngroups - 1]), [])
    for p in pairs:
        gt_ref[p] = state["gts"][p]


def _rwkv(prep, z_rwkv, lnx_w, lnx_b):
    s = z_rwkv.shape[0]
    t = CHUNK * RWKV_CHUNKS_PER_STEP
    el = prep[-1]
    blocks_per_step = RWKV_CHUNKS_PER_STEP // el.shape[1]
    tok = pl.BlockSpec((t, D_RWKV), lambda i: (i, 0))
    row = pl.BlockSpec((1, D_RWKV), lambda i: (0, 0))
    return pl.pallas_call(
        _rwkv_kernel,
        grid=(s // t,),
        in_specs=[
            tok, tok, tok, tok, tok, tok,
            pl.BlockSpec((blocks_per_step, el.shape[1], D_RWKV), lambda i: (i, 0, 0)),
            pl.BlockSpec((t, D_RWKV), lambda i: (i, 3)),
            row, row,
        ],
        out_specs=tok,
        out_shape=jax.ShapeDtypeStruct((s, D_RWKV), BF16),
        scratch_shapes=[
            pltpu.VMEM((N_PAIRS, LANES, LANES), F32),
        ],
        compiler_params=pltpu.CompilerParams(
            dimension_semantics=("arbitrary",), vmem_limit_bytes=VMEM_LIMIT),
        name="rwkv7",
    )(*prep, z_rwkv, lnx_w, lnx_b)


def _swa_out_kernel(sinks_ref, relb_ref, bucket_ref, q_ref, k_ref, v_ref, g0_ref, g1_ref, g2_ref, g3_ref,
                    qnw_ref, knw_ref, x_ref, yr_ref, wr_ref, wa_ref,
                    o_ref, bias_ref, kprev_ref, vprev_ref, yatt_ref):
    g_refs = (g0_ref, g1_ref, g2_ref, g3_ref)
    i = pl.program_id(0)
    bq = BLOCK
    tm = q_ref.shape[0]

    @pl.when(i == 0)
    def _():
        kprev_ref[...] = jnp.zeros_like(kprev_ref)
        vprev_ref[...] = jnp.zeros_like(vprev_ref)
        yatt_ref[...] = jnp.zeros_like(yatt_ref)
        bucket = bucket_ref[...]
        qi = lax.broadcasted_iota(jnp.int32, (bq, 2 * bq), 0)
        kj = lax.broadcasted_iota(jnp.int32, (bq, 2 * bq), 1)
        dist = bq + qi - kj
        inwin = (dist >= 0) & (dist < WINDOW)
        for h in range(N_Q_HEADS):
            acc = jnp.zeros((bq, 2 * bq), F32)
            for b in range(N_BUCKETS):
                acc = jnp.where(bucket == b, relb_ref[b, h], acc)
            acc = jnp.where(inwin, acc * LOG2E, NEG_BIG)
            bias_ref[0, h] = acc
            bias_ref[1, h] = jnp.where(kj >= bq, acc, NEG_BIG)

    first_step = jnp.where(i == 0, 1, 0)
    lane = lax.broadcasted_iota(jnp.int32, (bq, LANES), 1)
    head0 = lane < HEAD_DIM
    lane2 = lax.broadcasted_iota(jnp.int32, (2 * bq, LANES), 1)
    head0_2 = lane2 < HEAD_DIM
    r128 = lax.broadcasted_iota(jnp.int32, (LANES, LANES), 0)
    c128 = lax.broadcasted_iota(jnp.int32, (LANES, LANES), 1)
    seg_mean = jnp.where((r128 // HEAD_DIM) == (c128 // HEAD_DIM), 1.0 / HEAD_DIM, 0.0).astype(BF16)
    sink2 = [sinks_ref[h] * LOG2E for h in range(N_Q_HEADS)]
    pairs = range(N_PAIRS)
    heads = range(N_Q_HEADS)
    psl = [slice(p * LANES, (p + 1) * LANES) for p in pairs]
    kvh = [(2 * p) // (N_Q_HEADS // N_KV_HEADS) for p in pairs]

    def rms_heads(t, w):
        return t * lax.rsqrt(_seg_sum(t * t, seg_mean) + NORM_EPS) * w

    ya_prev = yatt_ref[...]
    yr_prev = yr_ref[...]

    def project_previous():
        for c0 in range(0, D_MODEL, MXU_WIDTH):
            cs = slice(c0, c0 + MXU_WIDTH)
            o_ref[:, cs] = x_ref[:, cs] + _dot(yr_prev, wr_ref[:, cs]) + _dot(ya_prev, wa_ref[:, cs])
            yield

    def attend():
        kp, vp = kprev_ref[...], vprev_ref[...]
        for b in range(tm // bq):
            rb = slice(b * bq, (b + 1) * bq)
            first = first_step if b == 0 else 0
            kn = rms_heads(k_ref[rb, :], knw_ref[...])
            vc = v_ref[rb, :]
            kcat = jnp.concatenate([kp, kn], axis=0)
            vcat = jnp.concatenate([vp, vc], axis=0)
            kp, vp = kn, vc
            krol = pltpu.roll(kcat, HEAD_DIM, axis=1)
            vrol = pltpu.roll(vcat, HEAD_DIM, axis=1)
            kdup = [jnp.where(head0_2, kcat, krol).astype(BF16), jnp.where(head0_2, krol, kcat).astype(BF16)]
            vdup = [jnp.where(head0_2, vcat, vrol).astype(BF16), jnp.where(head0_2, vrol, vcat).astype(BF16)]
            yield
            qn = [rms_heads(q_ref[rb, s], qnw_ref[:, s]) for s in psl]
            yield
            q2 = [jnp.concatenate([jnp.where(head0, q, 0.0), jnp.where(head0, 0.0, q)], axis=0).astype(BF16)
                  for q in qn]
            lg2 = [_dot_nt(q, kdup[kv]) for q, kv in zip(q2, kvh)]
            yield
            lg = [lg2[h // 2][(h % 2) * bq:(h % 2 + 1) * bq, :] + bias_ref[first, h] for h in heads]
            yield
            m = [jnp.maximum(jnp.max(l, axis=-1, keepdims=True), sink2[h]) for h, l in zip(heads, lg)]
            yield
            e = [jnp.exp2(l - mm) for l, mm in zip(lg, m)]
            yield
            denom = [jnp.sum(ee, axis=-1, keepdims=True) + jnp.exp2(sink2[h] - mm)
                     for h, ee, mm in zip(heads, e, m)]
            yield
            pv = [_dot(ee.astype(BF16), vdup[kvh[h // 2]]) for h, ee in zip(heads, e)]
            yield
            outs = [x / d for x, d in zip(pv, denom)]
            for p in pairs:
                o = jnp.where(head0, outs[2 * p], outs[2 * p + 1])
                g = g_refs[p // 2][rb, (p % 2) * LANES:(p % 2 + 1) * LANES]
                yatt_ref[rb, psl[p]] = (o * (g * _sigmoid(g))).astype(yatt_ref.dtype)
            yield
        kprev_ref[...] = kp
        vprev_ref[...] = vp

    proj = project_previous()
    n_levels = 9 * (tm // bq)
    every = n_levels // (D_MODEL // MXU_WIDTH)
    for n, _ in enumerate(attend()):
        if n % every == every - 1:
            next(proj, None)
    for _ in proj:
        pass


def _swa_out(z_att, bucket, sinks, rel_bias, qnw, knw, x2, y_rwkv, w_r, w_a, tm):
    s, d = x2.shape
    bq = BLOCK
    nb = s // tm
    smem = pl.BlockSpec(memory_space=pltpu.SMEM)
    cur = lambda i: jnp.minimum(i, nb - 1)
    prev = lambda i: jnp.maximum(i - 1, 0)
    kv_col = D_ATT // D_KV
    gw = 2 * D_KV
    gate = lambda n: pl.BlockSpec((tm, gw), lambda i: (cur(i), (D_ATT + 2 * D_KV) // gw + n))
    return pl.pallas_call(
        _swa_out_kernel,
        grid=(nb + 1,),
        in_specs=[
            smem, smem,
            pl.BlockSpec((bq, 2 * bq), lambda i: (0, 0)),
            pl.BlockSpec((tm, D_ATT), lambda i: (cur(i), 0)),
            pl.BlockSpec((tm, D_KV), lambda i: (cur(i), kv_col)),
            pl.BlockSpec((tm, D_KV), lambda i: (cur(i), kv_col + 1)),
            gate(0), gate(1), gate(2), gate(3),
            pl.BlockSpec((1, D_ATT), lambda i: (0, 0)),
            pl.BlockSpec((1, D_KV), lambda i: (0, 0)),
            pl.BlockSpec((tm, d), lambda i: (prev(i), 0)),
            pl.BlockSpec((tm, D_RWKV), lambda i: (prev(i), 0)),
            pl.BlockSpec((D_RWKV, d), lambda i: (0, 0)),
            pl.BlockSpec((D_ATT, d), lambda i: (0, 0)),
        ],
        out_specs=pl.BlockSpec((tm, d), lambda i: (prev(i), 0)),
        out_shape=jax.ShapeDtypeStruct((s, d), F32),
        scratch_shapes=[
            pltpu.VMEM((2, N_Q_HEADS, bq, 2 * bq), F32),
            pltpu.VMEM((bq, D_KV), F32),
            pltpu.VMEM((bq, D_KV), F32),
            pltpu.VMEM((tm, D_ATT), BF16),
        ],
        compiler_params=pltpu.CompilerParams(
            dimension_semantics=("arbitrary",), vmem_limit_bytes=VMEM_LIMIT),
        name="swa_out",
    )(sinks, rel_bias, bucket, z_att, z_att, z_att, z_att, z_att, z_att, z_att, qnw, knw, x2, y_rwkv, w_r, w_a)


def _t5_bucket_table():
    dist = BLOCK + np.arange(BLOCK)[:, None] - np.arange(2 * BLOCK)[None, :]
    n = np.maximum(dist, 0)
    nf = np.maximum(n, 1).astype(np.float64)
    large = MAX_EXACT + (np.log(nf / MAX_EXACT) / math.log(MAX_DISTANCE / MAX_EXACT)
                         * (N_BUCKETS - MAX_EXACT)).astype(np.int32)
    large = np.minimum(large, N_BUCKETS - 1)
    return np.where(n < MAX_EXACT, n, large).astype(np.int32)


def kernel(x, norm_w, w_in, w_out, mu_rwkv, w0, w2, a0, a2, k_k, k_a, r_k, lnx_w, lnx_b,
           q_norm_w, k_norm_w, sinks, rel_bias):
    b, s, d = x.shape
    assert (b, s, d) == (1, SEQ, D_MODEL) and norm_w.shape[0] == 1
    x2 = x.reshape(s, d)
    l = 0
    row = lambda t: t.reshape(1, -1).astype(F32)

    w_in_l = w_in[l]
    zeros_l = jnp.zeros((LORA, D_RWKV), F32)
    w2p = jnp.concatenate([w2[l], zeros_l], axis=0).astype(BF16)
    a2p = jnp.concatenate([zeros_l, a2[l]], axis=0).astype(BF16)
    w_out_r = w_out[l][:D_RWKV].astype(BF16)
    w_out_a = w_out[l][D_RWKV:].astype(BF16)
    qnw = jnp.tile(q_norm_w[l] * (HEAD_DIM ** -0.5 * LOG2E), N_Q_HEADS).reshape(1, D_ATT)
    knw = jnp.tile(k_norm_w[l], N_KV_HEADS).reshape(1, D_KV)
    bucket = jnp.asarray(_t5_bucket_table())

    z_rwkv, h, rs = _proj_in(x2, row(norm_w[l]), w_in_l, RWKV_COLS, row(mu_rwkv[l]), 512, 1408)
    z_att, *prep = _att_prep(h, rs, w_in_l, z_rwkv, row(w0[l]), w2p, row(a0[l]), a2p, row(k_k[l]), row(k_a[l]),
                             row(r_k[l]), 512, 1152, 256)
    y_rwkv = _rwkv(prep, z_rwkv, row(lnx_w[l]), row(lnx_b[l]))
    out = _swa_out(z_att, bucket, sinks[l].astype(F32), rel_bias.astype(F32), qnw, knw,
                   x2, y_rwkv, w_out_r, w_out_a, 512)
    return out.reshape(b, s, d)
```

```python
import math

import jax
import jax.numpy as jnp
import numpy as np
from jax import lax
from jax.experimental import pallas as pl
from jax.experimental.pallas import tpu as pltpu

F32 = jnp.float32
BF16 = jnp.bfloat16

D_MODEL = 2048
SEQ = 8192
HEAD_DIM = 64
D_RWKV = 1024
D_ATT = 1024
LORA = 64
N_Q_HEADS = 16
N_KV_HEADS = 2
D_KV = N_KV_HEADS * HEAD_DIM
WINDOW = 128
BLOCK = 128
N_BUCKETS = 32
MAX_EXACT = N_BUCKETS // 2
MAX_DISTANCE = 128
NORM_EPS = 1e-6
LNX_EPS = 64e-5
RWKV_COLS = 4 * D_RWKV + 2 * LORA
ATT_COLS = 2 * D_ATT + 2 * D_KV

LANES = 128
MXU_WIDTH = 256
N_PAIRS = D_RWKV // LANES
CHUNK = 64
RWKV_CHUNKS_PER_GROUP = 4
RWKV_CHUNKS_PER_STEP = 8
NEG_BIG = -1e30
LOG2E = math.log2(math.e)

VMEM_LIMIT = 56 * 1024 * 1024


def _dot(a, b):
    return jnp.dot(a, b, preferred_element_type=F32)


def _dot_nt(a, b):
    return lax.dot_general(a, b, (((1,), (1,)), ((), ())), preferred_element_type=F32)


def _split3(x):
    hi = x.astype(BF16)
    r1 = x - hi.astype(F32)
    mid = r1.astype(BF16)
    lo = (r1 - mid.astype(F32)).astype(BF16)
    return hi, mid, lo


def _seg_sum(x, seg_ones):
    hi = x.astype(BF16)
    lo = (x - hi.astype(F32)).astype(BF16)
    return _dot(hi, seg_ones) + _dot(lo, seg_ones)


def _dot_exact_lhs(m, x):
    hi, mid, lo = _split3(x)
    return _dot(m, hi) + _dot(m, mid) + _dot(m, lo)


def _sigmoid(x):
    return 1.0 / (1.0 + jnp.exp(-x))


def _proj_in_kernel(x_ref, nw_ref, w_ref, mu_ref, o_ref, wb_ref, prev_ref):
    @pl.when(pl.program_id(1) == 0)
    def _():
        wb_ref[...] = w_ref[...].astype(BF16)
        prev_ref[...] = jnp.zeros_like(prev_ref)

    tm, tn = o_ref.shape
    x = x_ref[...]
    h = (x * nw_ref[...]).astype(BF16)
    rs = lax.rsqrt(jnp.mean(x * x, axis=-1, keepdims=True) + NORM_EPS)
    for c0 in range(0, tn, MXU_WIDTH):
        cs = slice(c0, min(c0 + MXU_WIDTH, tn))
        z = _dot(h, wb_ref[:, cs]) * rs
        row = lax.broadcasted_iota(jnp.int32, z.shape, 0)
        zprev = jnp.where(row == 0, prev_ref[:, cs], pltpu.roll(z, 1, axis=0))
        prev_ref[:, cs] = z[tm - 1:tm, :]
        o_ref[:, cs] = z + (zprev - z) * mu_ref[:, cs]


def _proj_in(x2, norm_w, w, n, mu, tm, tn):
    s, d = x2.shape
    return pl.pallas_call(
        _proj_in_kernel,
        grid=(n // tn, s // tm),
        in_specs=[
            pl.BlockSpec((tm, d), lambda j, i: (i, 0)),
            pl.BlockSpec((1, d), lambda j, i: (0, 0)),
            pl.BlockSpec((d, tn), lambda j, i: (0, j)),
            pl.BlockSpec((1, tn), lambda j, i: (0, j)),
        ],
        out_specs=pl.BlockSpec((tm, tn), lambda j, i: (i, j)),
        out_shape=jax.ShapeDtypeStruct((s, n), F32),
        scratch_shapes=[pltpu.VMEM((d, tn), BF16), pltpu.VMEM((1, tn), F32)],
        compiler_params=pltpu.CompilerParams(
            dimension_semantics=("arbitrary", "arbitrary"), vmem_limit_bytes=VMEM_LIMIT),
        name="proj_in_rwkv",
    )(x2, norm_w, w, mu)


def _att_prep_kernel(x_ref, nw_ref, wa_ref, wb_in_ref, wc_ref, r_ref, k_ref, v_ref, lora_ref,
                     w0_ref, w2_ref, a0_ref, a2_ref, kk_ref, ka_ref, rk_ref,
                     o_ref, rt_ref, kt_ref, at_ref, bt_ref, vb_ref, bonus_ref, el_ref, wb_ref):
    c = CHUNK

    @pl.when(pl.program_id(1) == 0)
    def _():
        wp = wa_ref.shape[1]
        for n, piece in enumerate((wa_ref, wb_in_ref, wc_ref)):
            wb_ref[:, n * wp:(n + 1) * wp] = piece[...].astype(BF16)

    tm, tn = o_ref.shape
    x = x_ref[...]
    h = (x * nw_ref[...]).astype(BF16)
    rs = lax.rsqrt(jnp.mean(x * x, axis=-1, keepdims=True) + NORM_EPS)

    def project():
        for c0 in range(0, tn, MXU_WIDTH):
            cs = slice(c0, min(c0 + MXU_WIDTH, tn))
            o_ref[:, cs] = _dot(h, wb_ref[:, cs]) * rs
            yield

    r128 = lax.broadcasted_iota(jnp.int32, (LANES, LANES), 0)
    c128 = lax.broadcasted_iota(jnp.int32, (LANES, LANES), 1)
    seg_ones = jnp.where((r128 // HEAD_DIM) == (c128 // HEAD_DIM), 1.0, 0.0).astype(BF16)
    ti = lax.broadcasted_iota(jnp.int32, (c, c), 0)
    si = lax.broadcasted_iota(jnp.int32, (c, c), 1)
    tril_ones = jnp.where(si <= ti, 1.0, 0.0).astype(BF16)
    pairs = range(N_PAIRS)
    psl = [slice(p * LANES, (p + 1) * LANES) for p in pairs]

    def seg_sum_pairs(y):
        z = _seg_sum(jnp.concatenate([y[:, s] for s in psl], axis=0), seg_ones)
        return jnp.concatenate([z[p * c:(p + 1) * c] for p in pairs], axis=1)

    def prepare():
        lora_in = lora_ref[...]
        u_lin = _dot(jnp.tanh(lora_in).astype(BF16), w2_ref[...])
        a_lin = _dot(lora_in.astype(BF16), a2_ref[...])
        yield
        for ci in range(r_ref.shape[0] // c):
            rows = slice(ci * c, (ci + 1) * c)
            logw = -math.exp(-0.5) * _sigmoid(w0_ref[...] + u_lin[rows, :])
            av = _sigmoid(a0_ref[...] + a_lin[rows, :])
            logp = _dot_exact_lhs(tril_ones, logw)
            e_p = jnp.exp(logp)
            e_n = 1.0 / e_p
            el_ref[0, ci:ci + 1, :] = e_p[c - 1:c, :]
            yield
            r = r_ref[rows, :]
            k = k_ref[rows, :]
            v = v_ref[rows, :]
            kmod = k * (1.0 + (av - 1.0) * ka_ref[...])
            rt_ref[rows, :] = (r * e_p).astype(rt_ref.dtype)
            kt_ref[rows, :] = (kmod * e_n).astype(kt_ref.dtype)
            vb_ref[rows, :] = v.astype(vb_ref.dtype)
            bonus_ref[rows, :] = seg_sum_pairs(r * kmod * rk_ref[...]) * v
            yield
            kk = k * kk_ref[...]
            kkn = kk * lax.rsqrt(jnp.maximum(seg_sum_pairs(kk * kk), 1e-24))
            at_ref[rows, :] = (-kkn * jnp.exp(logp - logw)).astype(at_ref.dtype)
            bt_ref[rows, :] = (kkn * av * e_n).astype(bt_ref.dtype)
            yield

    proj = project()
    n_slices = 1 + 3 * (r_ref.shape[0] // c)
    every = max(1, n_slices // (-(-tn // MXU_WIDTH)))
    for n, _ in enumerate(prepare()):
        if n % every == every - 1:
            next(proj, None)
    for _ in proj:
        pass


def _att_prep(x2, norm_w, w_in, z_rwkv, w0, w2p, a0, a2p, k_k, k_a, r_k, tm, tn, tp):
    s, d = x2.shape
    n = ATT_COLS
    nj, ni = n // tn, s // tm
    wp = tn // 3
    assert nj * ni * tp == s and tp % CHUNK == 0 and RWKV_COLS % wp == 0 and wp % LANES == 0
    piece = lambda p: pl.BlockSpec((d, wp), lambda j, i: (0, RWKV_COLS // wp + 3 * j + p))
    blk = lambda j, i: j * ni + i
    row = pl.BlockSpec((1, D_RWKV), lambda j, i: (0, 0))
    lora_w = pl.BlockSpec((2 * LORA, D_RWKV), lambda j, i: (0, 0))
    zcol = lambda cb: pl.BlockSpec((tp, D_RWKV), lambda j, i: (blk(j, i), cb))
    tok = pl.BlockSpec((tp, D_RWKV), lambda j, i: (blk(j, i), 0))
    sds = lambda dt: jax.ShapeDtypeStruct((s, D_RWKV), dt)
    return pl.pallas_call(
        _att_prep_kernel,
        grid=(nj, ni),
        in_specs=[
            pl.BlockSpec((tm, d), lambda j, i: (i, 0)),
            pl.BlockSpec((1, d), lambda j, i: (0, 0)),
            piece(0), piece(1), piece(2),
            zcol(0), zcol(1), zcol(2),
            pl.BlockSpec((tp, 2 * LORA), lambda j, i: (blk(j, i), 4 * D_RWKV // (2 * LORA))),
            row, lora_w, row, lora_w, row, row, row,
        ],
        out_specs=[
            pl.BlockSpec((tm, tn), lambda j, i: (i, j)),
            tok, tok, tok, tok, tok, tok,
            pl.BlockSpec((1, tp // CHUNK, D_RWKV), lambda j, i: (blk(j, i), 0, 0)),
        ],
        out_shape=[
            jax.ShapeDtypeStruct((s, n), F32),
            sds(BF16), sds(BF16), sds(BF16), sds(BF16), sds(BF16), sds(F32),
            jax.ShapeDtypeStruct((s // tp, tp // CHUNK, D_RWKV), F32),
        ],
        scratch_shapes=[pltpu.VMEM((d, tn), BF16)],
        compiler_params=pltpu.CompilerParams(
            dimension_semantics=("arbitrary", "arbitrary"), vmem_limit_bytes=VMEM_LIMIT),
        name="proj_att_rwkv_prep",
    )(x2, norm_w, w_in, w_in, w_in, z_rwkv, z_rwkv, z_rwkv, z_rwkv, w0, w2p, a0, a2p, k_k, k_a, r_k)


def _rwkv_kernel(rt_ref, kt_ref, at_ref, bt_ref, v_ref, bonus_ref, el_ref, g_ref,
                 lw_ref, lb_ref, o_ref, gt_ref):
    c = CHUNK
    nch = RWKV_CHUNKS_PER_STEP
    i = pl.program_id(0)

    @pl.when(i == 0)
    def _():
        gt_ref[...] = jnp.zeros_like(gt_ref)

    lane = lax.broadcasted_iota(jnp.int32, (c, LANES), 1)
    head0 = lane < HEAD_DIM
    trow = lax.broadcasted_iota(jnp.int32, (c, LANES), 0)
    eye_cat = jnp.where((lane % HEAD_DIM) == trow, 1.0, 0.0)
    t2 = lax.broadcasted_iota(jnp.int32, (c, 2 * LANES), 0)
    s2 = lax.broadcasted_iota(jnp.int32, (c, 2 * LANES), 1) % HEAD_DIM
    strict2 = s2 < t2
    incl2 = s2 <= t2
    r128 = lax.broadcasted_iota(jnp.int32, (LANES, LANES), 0)
    c128 = lax.broadcasted_iota(jnp.int32, (LANES, LANES), 1)
    same_head = (r128 // HEAD_DIM) == (c128 // HEAD_DIM)
    zeros_c = jnp.zeros((c, LANES), F32)

    pairs = range(N_PAIRS)
    psl = [slice(p * LANES, (p + 1) * LANES) for p in pairs]

    def stack2(y):
        return jnp.concatenate([jnp.where(head0, y, 0.0), jnp.where(head0, 0.0, y)], axis=0)

    def catmul(a_cat, y):
        return _dot(a_cat.astype(BF16), stack2(y).astype(BF16))

    seg_mean = jnp.where(same_head, 1.0 / HEAD_DIM, 0.0).astype(BF16)

    def seg_sum_pairs(x, seg):
        y = _seg_sum(jnp.concatenate([x[:, s] for s in psl], axis=0), seg)
        return jnp.concatenate([y[p * c:(p + 1) * c] for p in pairs], axis=1)

    el_chunks = el_ref.shape[1]

    def intra(chunks, out):
        per_item = lambda ref: [ref[ci * c:(ci + 1) * c, s].astype(F32) for ci in chunks for s in psl]
        a_t, b_t, r_t, k_t, v_i = (per_item(ref) for ref in (at_ref, bt_ref, rt_ref, kt_ref, v_ref))
        e_last = [el_ref[ci // el_chunks, ci % el_chunks:ci % el_chunks + 1, s]
                  for ci in chunks for s in psl]
        sc = [_dot_nt(jnp.concatenate([a, r], axis=0).astype(BF16),
                      jnp.concatenate([stack2(b), stack2(k)], axis=0).astype(BF16))
              for a, r, b, k in zip(a_t, r_t, b_t, k_t)]
        yield
        l_all = [jnp.where(strict2, x[0:c, :], 0.0) for x in sc]
        a_rbk = [jnp.where(incl2, x[c:, :], 0.0) for x in sc]
        x = [l[:, 0:LANES] for l in l_all]
        tinv = [eye_cat + xi for xi in x]
        x = [catmul(xi, xi) for xi in x]
        yield
        for _ in range(4):
            tx = [catmul(jnp.concatenate([ti_, xi], axis=0), xi) for ti_, xi in zip(tinv, x)]
            tinv = [ti_ + y[0:c] for ti_, y in zip(tinv, tx)]
            x = [y[c:] for y in tx]
            yield
        tinv = [ti_ + catmul(ti_, xi) for ti_, xi in zip(tinv, x)]
        yield
        a_eff = [catmul(ti_, a) for ti_, a in zip(tinv, a_t)]
        yield
        lakv = [catmul(l[:, LANES:], v) for l, v in zip(l_all, v_i)]
        yield
        w_loc = [catmul(ti_, y) for ti_, y in zip(tinv, lakv)]
        yield
        q_eff = [r + catmul(a[:, 0:LANES], ae) for r, a, ae in zip(r_t, a_rbk, a_eff)]
        yield
        o_loc = [_dot(a.astype(BF16), jnp.concatenate([stack2(w), stack2(v)], axis=0).astype(BF16))
                 for a, w, v in zip(a_rbk, w_loc, v_i)]
        yield
        bh = [b * el for b, el in zip(b_t, e_last)]
        kh = [k * el for k, el in zip(k_t, e_last)]
        m_mat = [jnp.where(same_head, _dot(jnp.concatenate([ae, zeros_c], axis=0).T.astype(BF16),
                                           jnp.concatenate([b, zeros_c], axis=0).astype(BF16)), 0.0)
                 for ae, b in zip(a_eff, bh)]
        yield
        n_mat = [jnp.where(same_head, _dot(jnp.concatenate([w, v], axis=0).T.astype(BF16),
                                           jnp.concatenate([b, k], axis=0).astype(BF16)), 0.0)
                 for w, v, b, k in zip(w_loc, v_i, bh, kh)]
        out.update(q_eff=q_eff, o_loc=o_loc, m_mat=m_mat, n_mat=n_mat, e_last=e_last)

    def advance(gts, ic, j):
        sel = lambda name: ic[name][j * N_PAIRS:(j + 1) * N_PAIRS]
        gtb = [g.astype(BF16) for g in gts]
        o = [_dot_nt(q.astype(BF16), g) + ol for q, g, ol in zip(sel("q_eff"), gtb, sel("o_loc"))]
        gts = [g * el + _dot(gb, m.astype(BF16)) + n
               for g, gb, el, m, n in zip(gts, gtb, sel("e_last"), sel("m_mat"), sel("n_mat"))]
        return gts, jnp.concatenate(o, axis=1)

    def finish(ci, o):
        rs = slice(ci * c, (ci + 1) * c)
        d = o - seg_sum_pairs(o, seg_mean)
        var = seg_sum_pairs(d * d, seg_mean)
        on = d * lax.rsqrt(var + LNX_EPS) * lw_ref[...] + lb_ref[...]
        g = g_ref[rs, :]
        o_ref[rs, :] = ((on + bonus_ref[rs, :]) * (g * _sigmoid(g))).astype(o_ref.dtype)

    gsz = RWKV_CHUNKS_PER_GROUP
    ngroups = nch // gsz
    state = dict(gts=[gt_ref[p] for p in pairs])

    def tail_group(g, ic):
        for j in range(gsz):
            state["gts"], o = advance(state["gts"], ic, j)
            yield
            finish(g * gsz + j, o)
            yield

    def drive(main, sides):
        for _ in main:
            for s_ in sides:
                next(s_, None)
        for s_ in sides:
            for _ in s_:
                pass

    results = {}
    for g in range(ngroups):
        results[g] = {}
        sides = [tail_group(g - 1, results[g - 1])] if g >= 1 else []
        drive(intra(range(g * gsz, (g + 1) * gsz), results[g]), sides)
    drive(tail_group(ngroups - 1, results[ngroups - 1]), [])
    for p in pairs:
        gt_ref[p] = state["gts"][p]


def _rwkv(prep, z_rwkv, lnx_w, lnx_b):
    s = z_rwkv.shape[0]
    t = CHUNK * RWKV_CHUNKS_PER_STEP
    el = prep[-1]
    blocks_per_step = RWKV_CHUNKS_PER_STEP // el.shape[1]
    tok = pl.BlockSpec((t, D_RWKV), lambda i: (i, 0))
    row = pl.BlockSpec((1, D_RWKV), lambda i: (0, 0))
    return pl.pallas_call(
        _rwkv_kernel,
        grid=(s // t,),
        in_specs=[
            tok, tok, tok, tok, tok, tok,
            pl.BlockSpec((blocks_per_step, el.shape[1], D_RWKV), lambda i: (i, 0, 0)),
            pl.BlockSpec((t, D_RWKV), lambda i: (i, 3)),
            row, row,
        ],
        out_specs=tok,
        out_shape=jax.ShapeDtypeStruct((s, D_RWKV), BF16),
        scratch_shapes=[
            pltpu.VMEM((N_PAIRS, LANES, LANES), F32),
        ],
        compiler_params=pltpu.CompilerParams(
            dimension_semantics=("arbitrary",), vmem_limit_bytes=VMEM_LIMIT),
        name="rwkv7",
    )(*prep, z_rwkv, lnx_w, lnx_b)


def _swa_out_kernel(sinks_ref, relb_ref, bucket_ref, q_ref, k_ref, v_ref, g0_ref, g1_ref, g2_ref, g3_ref,
                    qnw_ref, knw_ref, x_ref, yr_ref, wr_ref, wa_ref,
                    o_ref, bias_ref, kprev_ref, vprev_ref, yatt_ref):
    g_refs = (g0_ref, g1_ref, g2_ref, g3_ref)
    i = pl.program_id(0)
    bq = BLOCK
    tm = q_ref.shape[0]

    @pl.when(i == 0)
    def _():
        kprev_ref[...] = jnp.zeros_like(kprev_ref)
        vprev_ref[...] = jnp.zeros_like(vprev_ref)
        yatt_ref[...] = jnp.zeros_like(yatt_ref)
        bucket = bucket_ref[...]
        qi = lax.broadcasted_iota(jnp.int32, (bq, 2 * bq), 0)
        kj = lax.broadcasted_iota(jnp.int32, (bq, 2 * bq), 1)
        dist = bq + qi - kj
        inwin = (dist >= 0) & (dist < WINDOW)
        for h in range(N_Q_HEADS):
            acc = jnp.zeros((bq, 2 * bq), F32)
            for b in range(N_BUCKETS):
                acc = jnp.where(bucket == b, relb_ref[b, h], acc)
            acc = jnp.where(inwin, acc * LOG2E, NEG_BIG)
            bias_ref[0, h] = acc
            bias_ref[1, h] = jnp.where(kj >= bq, acc, NEG_BIG)

    first_step = jnp.where(i == 0, 1, 0)
    lane = lax.broadcasted_iota(jnp.int32, (bq, LANES), 1)
    head0 = lane < HEAD_DIM
    lane2 = lax.broadcasted_iota(jnp.int32, (2 * bq, LANES), 1)
    head0_2 = lane2 < HEAD_DIM
    r128 = lax.broadcasted_iota(jnp.int32, (LANES, LANES), 0)
    c128 = lax.broadcasted_iota(jnp.int32, (LANES, LANES), 1)
    seg_mean = jnp.where((r128 // HEAD_DIM) == (c128 // HEAD_DIM), 1.0 / HEAD_DIM, 0.0).astype(BF16)
    sink2 = [sinks_ref[h] * LOG2E for h in range(N_Q_HEADS)]
    pairs = range(N_PAIRS)
    heads = range(N_Q_HEADS)
    psl = [slice(p * LANES, (p + 1) * LANES) for p in pairs]
    kvh = [(2 * p) // (N_Q_HEADS // N_KV_HEADS) for p in pairs]

    def rms_heads(t, w):
        return t * lax.rsqrt(_seg_sum(t * t, seg_mean) + NORM_EPS) * w

    ya_prev = yatt_ref[...]
    yr_prev = yr_ref[...]

    def project_previous():
        for c0 in range(0, D_MODEL, MXU_WIDTH):
            cs = slice(c0, c0 + MXU_WIDTH)
            o_ref[:, cs] = x_ref[:, cs] + _dot(yr_prev, wr_ref[:, cs]) + _dot(ya_prev, wa_ref[:, cs])
            yield

    def attend():
        kp, vp = kprev_ref[...], vprev_ref[...]
        for b in range(tm // bq):
            rb = slice(b * bq, (b + 1) * bq)
            first = first_step if b == 0 else 0
            kn = rms_heads(k_ref[rb, :], knw_ref[...])
            vc = v_ref[rb, :]
            kcat = jnp.concatenate([kp, kn], axis=0)
            vcat = jnp.concatenate([vp, vc], axis=0)
            kp, vp = kn, vc
            krol = pltpu.roll(kcat, HEAD_DIM, axis=1)
            vrol = pltpu.roll(vcat, HEAD_DIM, axis=1)
            kdup = [jnp.where(head0_2, kcat, krol).astype(BF16), jnp.where(head0_2, krol, kcat).astype(BF16)]
            vdup = [jnp.where(head0_2, vcat, vrol).astype(BF16), jnp.where(head0_2, vrol, vcat).astype(BF16)]
            yield
            qn = [rms_heads(q_ref[rb, s], qnw_ref[:, s]) for s in psl]
            yield
            q2 = [jnp.concatenate([jnp.where(head0, q, 0.0), jnp.where(head0, 0.0, q)], axis=0).astype(BF16)
                  for q in qn]
            lg2 = [_dot_nt(q, kdup[kv]) for q, kv in zip(q2, kvh)]
            yield
            lg = [lg2[h // 2][(h % 2) * bq:(h % 2 + 1) * bq, :] + bias_ref[first, h] for h in heads]
            yield
            m = [jnp.maximum(jnp.max(l, axis=-1, keepdims=True), sink2[h]) for h, l in zip(heads, lg)]
            yield
            e = [jnp.exp2(l - mm) for l, mm in zip(lg, m)]
            yield
            denom = [jnp.sum(ee, axis=-1, keepdims=True) + jnp.exp2(sink2[h] - mm)
                     for h, ee, mm in zip(heads, e, m)]
            yield
            pv = [_dot(ee.astype(BF16), vdup[kvh[h // 2]]) for h, ee in zip(heads, e)]
            yield
            outs = [x / d for x, d in zip(pv, denom)]
            for p in pairs:
                o = jnp.where(head0, outs[2 * p], outs[2 * p + 1])
                g = g_refs[p // 2][rb, (p % 2) * LANES:(p % 2 + 1) * LANES]
                yatt_ref[rb, psl[p]] = (o * (g * _sigmoid(g))).astype(yatt_ref.dtype)
            yield
        kprev_ref[...] = kp
        vprev_ref[...] = vp

    proj = project_previous()
    n_levels = 9 * (tm // bq)
    every = n_levels // (D_MODEL // MXU_WIDTH)
    for n, _ in enumerate(attend()):
        if n % every == every - 1:
            next(proj, None)
    for _ in proj:
        pass


def _swa_out(z_att, bucket, sinks, rel_bias, qnw, knw, x2, y_rwkv, w_out, tm):
    s, d = x2.shape
    bq = BLOCK
    nb = s // tm
    smem = pl.BlockSpec(memory_space=pltpu.SMEM)
    cur = lambda i: jnp.minimum(i, nb - 1)
    prev = lambda i: jnp.maximum(i - 1, 0)
    kv_col = D_ATT // D_KV
    gw = 2 * D_KV
    gate = lambda n: pl.BlockSpec((tm, gw), lambda i: (cur(i), (D_ATT + 2 * D_KV) // gw + n))
    return pl.pallas_call(
        _swa_out_kernel,
        grid=(nb + 1,),
        in_specs=[
            smem, smem,
            pl.BlockSpec((bq, 2 * bq), lambda i: (0, 0)),
            pl.BlockSpec((tm, D_ATT), lambda i: (cur(i), 0)),
            pl.BlockSpec((tm, D_KV), lambda i: (cur(i), kv_col)),
            pl.BlockSpec((tm, D_KV), lambda i: (cur(i), kv_col + 1)),
            gate(0), gate(1), gate(2), gate(3),
            pl.BlockSpec((1, D_ATT), lambda i: (0, 0)),
            pl.BlockSpec((1, D_KV), lambda i: (0, 0)),
            pl.BlockSpec((tm, d), lambda i: (prev(i), 0)),
            pl.BlockSpec((tm, D_RWKV), lambda i: (prev(i), 0)),
            pl.BlockSpec((D_RWKV, d), lambda i: (0, 0)),
            pl.BlockSpec((D_ATT, d), lambda i: (D_RWKV // D_ATT, 0)),
        ],
        out_specs=pl.BlockSpec((tm, d), lambda i: (prev(i), 0)),
        out_shape=jax.ShapeDtypeStruct((s, d), F32),
        scratch_shapes=[
            pltpu.VMEM((2, N_Q_HEADS, bq, 2 * bq), F32),
            pltpu.VMEM((bq, D_KV), F32),
            pltpu.VMEM((bq, D_KV), F32),
            pltpu.VMEM((tm, D_ATT), BF16),
        ],
        compiler_params=pltpu.CompilerParams(
            dimension_semantics=("arbitrary",), vmem_limit_bytes=VMEM_LIMIT),
        name="swa_out",
    )(sinks, rel_bias, bucket, z_att, z_att, z_att, z_att, z_att, z_att, z_att, qnw, knw, x2, y_rwkv, w_out, w_out)


def _t5_bucket_table():
    dist = BLOCK + np.arange(BLOCK)[:, None] - np.arange(2 * BLOCK)[None, :]
    n = np.maximum(dist, 0)
    nf = np.maximum(n, 1).astype(np.float64)
    large = MAX_EXACT + (np.log(nf / MAX_EXACT) / math.log(MAX_DISTANCE / MAX_EXACT)
                         * (N_BUCKETS - MAX_EXACT)).astype(np.int32)
    large = np.minimum(large, N_BUCKETS - 1)
    return np.where(n < MAX_EXACT, n, large).astype(np.int32)


def kernel(x, norm_w, w_in, w_out, mu_rwkv, w0, w2, a0, a2, k_k, k_a, r_k, lnx_w, lnx_b,
           q_norm_w, k_norm_w, sinks, rel_bias):
    b, s, d = x.shape
    assert (b, s, d) == (1, SEQ, D_MODEL) and norm_w.shape[0] == 1
    x2 = x.reshape(s, d)
    l = 0
    row = lambda t: t.reshape(1, -1).astype(F32)

    w_in_l = w_in[l]
    zeros_l = jnp.zeros((LORA, D_RWKV), F32)
    w2p = jnp.concatenate([w2[l], zeros_l], axis=0).astype(BF16)
    a2p = jnp.concatenate([zeros_l, a2[l]], axis=0).astype(BF16)
    w_out_b = w_out[l].astype(BF16)
    qnw = jnp.tile(q_norm_w[l] * (HEAD_DIM ** -0.5 * LOG2E), N_Q_HEADS).reshape(1, D_ATT)
    knw = jnp.tile(k_norm_w[l], N_KV_HEADS).reshape(1, D_KV)
    bucket = jnp.asarray(_t5_bucket_table())

    nw = row(norm_w[l])
    z_rwkv = _proj_in(x2, nw, w_in_l, RWKV_COLS, row(mu_rwkv[l]), 512, 1408)
    z_att, *prep = _att_prep(x2, nw, w_in_l, z_rwkv, row(w0[l]), w2p, row(a0[l]), a2p, row(k_k[l]), row(k_a[l]),
                             row(r_k[l]), 512, 1152, 256)
    y_rwkv = _rwkv(prep, z_rwkv, row(lnx_w[l]), row(lnx_b[l]))
    out = _swa_out(z_att, bucket, sinks[l].astype(F32), rel_bias.astype(F32), qnw, knw,
                   x2, y_rwkv, w_out_b, 512)
    return out.reshape(b, s, d)
```

```python
import math

import jax
import jax.numpy as jnp
import numpy as np
from jax import lax
from jax.experimental import pallas as pl
from jax.experimental.pallas import tpu as pltpu

F32 = jnp.float32
BF16 = jnp.bfloat16

D_MODEL = 2048
SEQ = 8192
HEAD_DIM = 64
D_RWKV = 1024
D_ATT = 1024
LORA = 64
N_Q_HEADS = 16
N_KV_HEADS = 2
D_KV = N_KV_HEADS * HEAD_DIM
WINDOW = 128
BLOCK = 128
N_BUCKETS = 32
MAX_EXACT = N_BUCKETS // 2
MAX_DISTANCE = 128
NORM_EPS = 1e-6
LNX_EPS = 64e-5
RWKV_COLS = 4 * D_RWKV + 2 * LORA
ATT_COLS = 2 * D_ATT + 2 * D_KV

LANES = 128
MXU_WIDTH = 256
N_PAIRS = D_RWKV // LANES
CHUNK = 64
RWKV_CHUNKS_PER_GROUP = 4
RWKV_CHUNKS_PER_STEP = 8
NEG_BIG = -1e30
LOG2E = math.log2(math.e)

ROW_BLOCK = 512
RWKV_COL_BLOCK = RWKV_COLS // 3
ATT_COL_BLOCK = ATT_COLS // 2
PREP_TOKENS = SEQ // ((ATT_COLS // ATT_COL_BLOCK) * (SEQ // ROW_BLOCK))
VMEM_LIMIT = 56 * 1024 * 1024


def _dot(a, b):
    return jnp.dot(a, b, preferred_element_type=F32)


def _dot_nt(a, b):
    return lax.dot_general(a, b, (((1,), (1,)), ((), ())), preferred_element_type=F32)


def _split3(x):
    hi = x.astype(BF16)
    r1 = x - hi.astype(F32)
    mid = r1.astype(BF16)
    lo = (r1 - mid.astype(F32)).astype(BF16)
    return hi, mid, lo


def _seg_sum(x, seg_ones):
    hi = x.astype(BF16)
    lo = (x - hi.astype(F32)).astype(BF16)
    return _dot(hi, seg_ones) + _dot(lo, seg_ones)


def _dot_exact_lhs(m, x):
    hi, mid, lo = _split3(x)
    return _dot(m, hi) + _dot(m, mid) + _dot(m, lo)


def _sigmoid(x):
    return 1.0 / (1.0 + jnp.exp(-x))


def _proj_in_kernel(x_ref, nw_ref, w_ref, mu_ref, o_ref, wb_ref, prev_ref):
    @pl.when(pl.program_id(1) == 0)
    def _():
        wb_ref[...] = w_ref[...].astype(BF16)
        prev_ref[...] = jnp.zeros_like(prev_ref)

    tm, tn = o_ref.shape
    x = x_ref[...]
    h = (x * nw_ref[...]).astype(BF16)
    rs = lax.rsqrt(jnp.mean(x * x, axis=-1, keepdims=True) + NORM_EPS)
    for c0 in range(0, tn, MXU_WIDTH):
        cs = slice(c0, min(c0 + MXU_WIDTH, tn))
        z = _dot(h, wb_ref[:, cs]) * rs
        row = lax.broadcasted_iota(jnp.int32, z.shape, 0)
        zprev = jnp.where(row == 0, prev_ref[:, cs], pltpu.roll(z, 1, axis=0))
        prev_ref[:, cs] = z[tm - 1:tm, :]
        o_ref[:, cs] = z + (zprev - z) * mu_ref[:, cs]


def _proj_in(x2, norm_w, w, n, mu, tm, tn):
    s, d = x2.shape
    return pl.pallas_call(
        _proj_in_kernel,
        grid=(n // tn, s // tm),
        in_specs=[
            pl.BlockSpec((tm, d), lambda j, i: (i, 0)),
            pl.BlockSpec((1, d), lambda j, i: (0, 0)),
            pl.BlockSpec((d, tn), lambda j, i: (0, j)),
            pl.BlockSpec((1, tn), lambda j, i: (0, j)),
        ],
        out_specs=pl.BlockSpec((tm, tn), lambda j, i: (i, j)),
        out_shape=jax.ShapeDtypeStruct((s, n), F32),
        scratch_shapes=[pltpu.VMEM((d, tn), BF16), pltpu.VMEM((1, tn), F32)],
        compiler_params=pltpu.CompilerParams(
            dimension_semantics=("arbitrary", "arbitrary"), vmem_limit_bytes=VMEM_LIMIT),
        name="proj_in_rwkv",
    )(x2, norm_w, w, mu)


def _att_prep_kernel(x_ref, nw_ref, wa_ref, wb_in_ref, wc_ref, r_ref, k_ref, v_ref, lora_ref,
                     w0_ref, w2_ref, a0_ref, a2_ref, kk_ref, ka_ref, rk_ref,
                     o_ref, rt_ref, kt_ref, at_ref, bt_ref, bonus_ref, el_ref, wb_ref):
    c = CHUNK

    @pl.when(pl.program_id(1) == 0)
    def _():
        wp = wa_ref.shape[1]
        for n, piece in enumerate((wa_ref, wb_in_ref, wc_ref)):
            wb_ref[:, n * wp:(n + 1) * wp] = piece[...].astype(BF16)

    tm, tn = o_ref.shape
    x = x_ref[...]
    h = (x * nw_ref[...]).astype(BF16)
    rs = lax.rsqrt(jnp.mean(x * x, axis=-1, keepdims=True) + NORM_EPS)

    def project():
        for c0 in range(0, tn, MXU_WIDTH):
            cs = slice(c0, min(c0 + MXU_WIDTH, tn))
            o_ref[:, cs] = _dot(h, wb_ref[:, cs]) * rs
            yield

    r128 = lax.broadcasted_iota(jnp.int32, (LANES, LANES), 0)
    c128 = lax.broadcasted_iota(jnp.int32, (LANES, LANES), 1)
    seg_ones = jnp.where((r128 // HEAD_DIM) == (c128 // HEAD_DIM), 1.0, 0.0).astype(BF16)
    ti = lax.broadcasted_iota(jnp.int32, (c, c), 0)
    si = lax.broadcasted_iota(jnp.int32, (c, c), 1)
    tril_ones = jnp.where(si <= ti, 1.0, 0.0).astype(BF16)
    pairs = range(N_PAIRS)
    psl = [slice(p * LANES, (p + 1) * LANES) for p in pairs]

    def seg_sum_pairs(y):
        z = _seg_sum(jnp.concatenate([y[:, s] for s in psl], axis=0), seg_ones)
        return jnp.concatenate([z[p * c:(p + 1) * c] for p in pairs], axis=1)

    def prepare():
        lora_in = lora_ref[...]
        u_lin = _dot(jnp.tanh(lora_in).astype(BF16), w2_ref[...])
        a_lin = _dot(lora_in.astype(BF16), a2_ref[...])
        yield
        for ci in range(r_ref.shape[0] // c):
            rows = slice(ci * c, (ci + 1) * c)
            logw = -math.exp(-0.5) * _sigmoid(w0_ref[...] + u_lin[rows, :])
            av = _sigmoid(a0_ref[...] + a_lin[rows, :])
            logp = _dot_exact_lhs(tril_ones, logw)
            e_p = jnp.exp(logp)
            e_n = 1.0 / e_p
            el_ref[0, ci:ci + 1, :] = e_p[c - 1:c, :]
            yield
            r = r_ref[rows, :]
            k = k_ref[rows, :]
            v = v_ref[rows, :]
            kmod = k * (1.0 + (av - 1.0) * ka_ref[...])
            rt_ref[rows, :] = (r * e_p).astype(rt_ref.dtype)
            kt_ref[rows, :] = (kmod * e_n).astype(kt_ref.dtype)
            bonus_ref[rows, :] = seg_sum_pairs(r * kmod * rk_ref[...]) * v
            yield
            kk = k * kk_ref[...]
            kkn = kk * lax.rsqrt(jnp.maximum(seg_sum_pairs(kk * kk), 1e-24))
            at_ref[rows, :] = (-kkn * jnp.exp(logp - logw)).astype(at_ref.dtype)
            bt_ref[rows, :] = (kkn * av * e_n).astype(bt_ref.dtype)
            yield

    proj = project()
    n_slices = 1 + 3 * (r_ref.shape[0] // c)
    every = max(1, n_slices // (-(-tn // MXU_WIDTH)))
    for n, _ in enumerate(prepare()):
        if n % every == every - 1:
            next(proj, None)
    for _ in proj:
        pass


def _att_prep(x2, norm_w, w_in, z_rwkv, w0, w2p, a0, a2p, k_k, k_a, r_k, tm, tn, tp):
    s, d = x2.shape
    n = ATT_COLS
    nj, ni = n // tn, s // tm
    wp = tn // 3
    assert nj * ni * tp == s and tp % CHUNK == 0 and RWKV_COLS % wp == 0 and wp % LANES == 0
    piece = lambda p: pl.BlockSpec((d, wp), lambda j, i: (0, RWKV_COLS // wp + 3 * j + p))
    blk = lambda j, i: j * ni + i
    row = pl.BlockSpec((1, D_RWKV), lambda j, i: (0, 0))
    lora_w = pl.BlockSpec((2 * LORA, D_RWKV), lambda j, i: (0, 0))
    zcol = lambda cb: pl.BlockSpec((tp, D_RWKV), lambda j, i: (blk(j, i), cb))
    tok = pl.BlockSpec((tp, D_RWKV), lambda j, i: (blk(j, i), 0))
    sds = lambda dt: jax.ShapeDtypeStruct((s, D_RWKV), dt)
    return pl.pallas_call(
        _att_prep_kernel,
        grid=(nj, ni),
        in_specs=[
            pl.BlockSpec((tm, d), lambda j, i: (i, 0)),
            pl.BlockSpec((1, d), lambda j, i: (0, 0)),
            piece(0), piece(1), piece(2),
            zcol(0), zcol(1), zcol(2),
            pl.BlockSpec((tp, 2 * LORA), lambda j, i: (blk(j, i), 4 * D_RWKV // (2 * LORA))),
            row, lora_w, row, lora_w, row, row, row,
        ],
        out_specs=[
            pl.BlockSpec((tm, tn), lambda j, i: (i, j)),
            tok, tok, tok, tok, tok,
            pl.BlockSpec((1, tp // CHUNK, D_RWKV), lambda j, i: (blk(j, i), 0, 0)),
        ],
        out_shape=[
            jax.ShapeDtypeStruct((s, n), F32),
            sds(BF16), sds(BF16), sds(BF16), sds(BF16), sds(F32),
            jax.ShapeDtypeStruct((s // tp, tp // CHUNK, D_RWKV), F32),
        ],
        scratch_shapes=[pltpu.VMEM((d, tn), BF16)],
        compiler_params=pltpu.CompilerParams(
            dimension_semantics=("arbitrary", "arbitrary"), vmem_limit_bytes=VMEM_LIMIT),
        name="proj_att_rwkv_prep",
    )(x2, norm_w, w_in, w_in, w_in, z_rwkv, z_rwkv, z_rwkv, z_rwkv, w0, w2p, a0, a2p, k_k, k_a, r_k)


def _rwkv_kernel(rt_ref, kt_ref, at_ref, bt_ref, bonus_ref, el_ref, v_ref, g_ref,
                 lw_ref, lb_ref, o_ref, gt_ref):
    c = CHUNK
    nch = RWKV_CHUNKS_PER_STEP
    i = pl.program_id(0)

    @pl.when(i == 0)
    def _():
        gt_ref[...] = jnp.zeros_like(gt_ref)

    lane = lax.broadcasted_iota(jnp.int32, (c, LANES), 1)
    head0 = lane < HEAD_DIM
    trow = lax.broadcasted_iota(jnp.int32, (c, LANES), 0)
    eye_cat = jnp.where((lane % HEAD_DIM) == trow, 1.0, 0.0)
    t2 = lax.broadcasted_iota(jnp.int32, (c, 2 * LANES), 0)
    s2 = lax.broadcasted_iota(jnp.int32, (c, 2 * LANES), 1) % HEAD_DIM
    strict2 = s2 < t2
    incl2 = s2 <= t2
    r128 = lax.broadcasted_iota(jnp.int32, (LANES, LANES), 0)
    c128 = lax.broadcasted_iota(jnp.int32, (LANES, LANES), 1)
    same_head = (r128 // HEAD_DIM) == (c128 // HEAD_DIM)
    zeros_c = jnp.zeros((c, LANES), F32)

    pairs = range(N_PAIRS)
    psl = [slice(p * LANES, (p + 1) * LANES) for p in pairs]

    def stack2(y):
        return jnp.concatenate([jnp.where(head0, y, 0.0), jnp.where(head0, 0.0, y)], axis=0)

    def catmul(a_cat, y):
        return _dot(a_cat.astype(BF16), stack2(y).astype(BF16))

    seg_mean = jnp.where(same_head, 1.0 / HEAD_DIM, 0.0).astype(BF16)

    def seg_sum_pairs(x, seg):
        y = _seg_sum(jnp.concatenate([x[:, s] for s in psl], axis=0), seg)
        return jnp.concatenate([y[p * c:(p + 1) * c] for p in pairs], axis=1)

    el_chunks = el_ref.shape[1]

    def intra(chunks, out):
        per_item = lambda ref: [ref[ci * c:(ci + 1) * c, s].astype(F32) for ci in chunks for s in psl]
        a_t, b_t, r_t, k_t, v_i = (per_item(ref) for ref in (at_ref, bt_ref, rt_ref, kt_ref, v_ref))
        e_last = [el_ref[ci // el_chunks, ci % el_chunks:ci % el_chunks + 1, s]
                  for ci in chunks for s in psl]
        sc = [_dot_nt(jnp.concatenate([a, r], axis=0).astype(BF16),
                      jnp.concatenate([stack2(b), stack2(k)], axis=0).astype(BF16))
              for a, r, b, k in zip(a_t, r_t, b_t, k_t)]
        yield
        l_all = [jnp.where(strict2, x[0:c, :], 0.0) for x in sc]
        a_rbk = [jnp.where(incl2, x[c:, :], 0.0) for x in sc]
        x = [l[:, 0:LANES] for l in l_all]
        tinv = [eye_cat + xi for xi in x]
        x = [catmul(xi, xi) for xi in x]
        yield
        for _ in range(4):
            tx = [catmul(jnp.concatenate([ti_, xi], axis=0), xi) for ti_, xi in zip(tinv, x)]
            tinv = [ti_ + y[0:c] for ti_, y in zip(tinv, tx)]
            x = [y[c:] for y in tx]
            yield
        tinv = [ti_ + catmul(ti_, xi) for ti_, xi in zip(tinv, x)]
        yield
        a_eff = [catmul(ti_, a) for ti_, a in zip(tinv, a_t)]
        yield
        lakv = [catmul(l[:, LANES:], v) for l, v in zip(l_all, v_i)]
        yield
        w_loc = [catmul(ti_, y) for ti_, y in zip(tinv, lakv)]
        yield
        q_eff = [r + catmul(a[:, 0:LANES], ae) for r, a, ae in zip(r_t, a_rbk, a_eff)]
        yield
        o_loc = [_dot(a.astype(BF16), jnp.concatenate([stack2(w), stack2(v)], axis=0).astype(BF16))
                 for a, w, v in zip(a_rbk, w_loc, v_i)]
        yield
        bh = [b * el for b, el in zip(b_t, e_last)]
        kh = [k * el for k, el in zip(k_t, e_last)]
        m_mat = [jnp.where(same_head, _dot(jnp.concatenate([ae, zeros_c], axis=0).T.astype(BF16),
                                           jnp.concatenate([b, zeros_c], axis=0).astype(BF16)), 0.0)
                 for ae, b in zip(a_eff, bh)]
        yield
        n_mat = [jnp.where(same_head, _dot(jnp.concatenate([w, v], axis=0).T.astype(BF16),
                                           jnp.concatenate([b, k], axis=0).astype(BF16)), 0.0)
                 for w, v, b, k in zip(w_loc, v_i, bh, kh)]
        out.update(q_eff=q_eff, o_loc=o_loc, m_mat=m_mat, n_mat=n_mat, e_last=e_last)

    def advance(gts, ic, j):
        sel = lambda name: ic[name][j * N_PAIRS:(j + 1) * N_PAIRS]
        gtb = [g.astype(BF16) for g in gts]
        o = [_dot_nt(q.astype(BF16), g) + ol for q, g, ol in zip(sel("q_eff"), gtb, sel("o_loc"))]
        gts = [g * el + _dot(gb, m.astype(BF16)) + n
               for g, gb, el, m, n in zip(gts, gtb, sel("e_last"), sel("m_mat"), sel("n_mat"))]
        return gts, jnp.concatenate(o, axis=1)

    def finish(ci, o):
        rs = slice(ci * c, (ci + 1) * c)
        d = o - seg_sum_pairs(o, seg_mean)
        var = seg_sum_pairs(d * d, seg_mean)
        on = d * lax.rsqrt(var + LNX_EPS) * lw_ref[...] + lb_ref[...]
        g = g_ref[rs, :]
        o_ref[rs, :] = ((on + bonus_ref[rs, :]) * (g * _sigmoid(g))).astype(o_ref.dtype)

    gsz = RWKV_CHUNKS_PER_GROUP
    ngroups = nch // gsz
    state = dict(gts=[gt_ref[p] for p in pairs])

    def tail_group(g, ic):
        for j in range(gsz):
            state["gts"], o = advance(state["gts"], ic, j)
            yield
            finish(g * gsz + j, o)
            yield

    def drive(main, sides):
        for _ in main:
            for s_ in sides:
                next(s_, None)
        for s_ in sides:
            for _ in s_:
                pass

    results = {}
    for g in range(ngroups):
        results[g] = {}
        sides = [tail_group(g - 1, results[g - 1])] if g >= 1 else []
        drive(intra(range(g * gsz, (g + 1) * gsz), results[g]), sides)
    drive(tail_group(ngroups - 1, results[ngroups - 1]), [])
    for p in pairs:
        gt_ref[p] = state["gts"][p]


def _rwkv(prep, z_rwkv, lnx_w, lnx_b):
    s = z_rwkv.shape[0]
    t = CHUNK * RWKV_CHUNKS_PER_STEP
    el = prep[-1]
    blocks_per_step = RWKV_CHUNKS_PER_STEP // el.shape[1]
    tok = pl.BlockSpec((t, D_RWKV), lambda i: (i, 0))
    row = pl.BlockSpec((1, D_RWKV), lambda i: (0, 0))
    return pl.pallas_call(
        _rwkv_kernel,
        grid=(s // t,),
        in_specs=[
            tok, tok, tok, tok, tok,
            pl.BlockSpec((blocks_per_step, el.shape[1], D_RWKV), lambda i: (i, 0, 0)),
            pl.BlockSpec((t, D_RWKV), lambda i: (i, 2)),
            pl.BlockSpec((t, D_RWKV), lambda i: (i, 3)),
            row, row,
        ],
        out_specs=tok,
        out_shape=jax.ShapeDtypeStruct((s, D_RWKV), BF16),
        scratch_shapes=[
            pltpu.VMEM((N_PAIRS, LANES, LANES), F32),
        ],
        compiler_params=pltpu.CompilerParams(
            dimension_semantics=("arbitrary",), vmem_limit_bytes=VMEM_LIMIT),
        name="rwkv7",
    )(*prep, z_rwkv, z_rwkv, lnx_w, lnx_b)


def _swa_out_kernel(sinks_ref, relb_ref, bucket_ref, q_ref, k_ref, v_ref, g0_ref, g1_ref, g2_ref, g3_ref,
                    qnw_ref, knw_ref, x_ref, yr_ref, wr_ref, wa_ref,
                    o_ref, bias_ref, kprev_ref, vprev_ref, yatt_ref):
    g_refs = (g0_ref, g1_ref, g2_ref, g3_ref)
    i = pl.program_id(0)
    bq = BLOCK
    tm = q_ref.shape[0]

    @pl.when(i == 0)
    def _():
        kprev_ref[...] = jnp.zeros_like(kprev_ref)
        vprev_ref[...] = jnp.zeros_like(vprev_ref)
        yatt_ref[...] = jnp.zeros_like(yatt_ref)
        bucket = bucket_ref[...]
        qi = lax.broadcasted_iota(jnp.int32, (bq, 2 * bq), 0)
        kj = lax.broadcasted_iota(jnp.int32, (bq, 2 * bq), 1)
        dist = bq + qi - kj
        inwin = (dist >= 0) & (dist < WINDOW)
        for h in range(N_Q_HEADS):
            acc = jnp.zeros((bq, 2 * bq), F32)
            for b in range(N_BUCKETS):
                acc = jnp.where(bucket == b, relb_ref[b, h], acc)
            acc = jnp.where(inwin, acc * LOG2E, NEG_BIG)
            bias_ref[0, h] = acc
            bias_ref[1, h] = jnp.where(kj >= bq, acc, NEG_BIG)

    first_step = jnp.where(i == 0, 1, 0)
    lane = lax.broadcasted_iota(jnp.int32, (bq, LANES), 1)
    head0 = lane < HEAD_DIM
    lane2 = lax.broadcasted_iota(jnp.int32, (2 * bq, LANES), 1)
    head0_2 = lane2 < HEAD_DIM
    r128 = lax.broadcasted_iota(jnp.int32, (LANES, LANES), 0)
    c128 = lax.broadcasted_iota(jnp.int32, (LANES, LANES), 1)
    seg_mean = jnp.where((r128 // HEAD_DIM) == (c128 // HEAD_DIM), 1.0 / HEAD_DIM, 0.0).astype(BF16)
    sink2 = [sinks_ref[h] * LOG2E for h in range(N_Q_HEADS)]
    pairs = range(N_PAIRS)
    heads = range(N_Q_HEADS)
    psl = [slice(p * LANES, (p + 1) * LANES) for p in pairs]
    kvh = [(2 * p) // (N_Q_HEADS // N_KV_HEADS) for p in pairs]

    def rms_heads(t, w):
        return t * lax.rsqrt(_seg_sum(t * t, seg_mean) + NORM_EPS) * w

    ya_prev = yatt_ref[...]
    yr_prev = yr_ref[...]

    def project_previous():
        for c0 in range(0, D_MODEL, MXU_WIDTH):
            cs = slice(c0, c0 + MXU_WIDTH)
            o_ref[:, cs] = x_ref[:, cs] + _dot(yr_prev, wr_ref[:, cs]) + _dot(ya_prev, wa_ref[:, cs])
            yield

    def attend():
        kp, vp = kprev_ref[...], vprev_ref[...]
        for b in range(tm // bq):
            rb = slice(b * bq, (b + 1) * bq)
            first = first_step if b == 0 else 0
            kn = rms_heads(k_ref[rb, :], knw_ref[...])
            vc = v_ref[rb, :]
            kcat = jnp.concatenate([kp, kn], axis=0)
            vcat = jnp.concatenate([vp, vc], axis=0)
            kp, vp = kn, vc
            krol = pltpu.roll(kcat, HEAD_DIM, axis=1)
            vrol = pltpu.roll(vcat, HEAD_DIM, axis=1)
            kdup = [jnp.where(head0_2, kcat, krol).astype(BF16), jnp.where(head0_2, krol, kcat).astype(BF16)]
            vdup = [jnp.where(head0_2, vcat, vrol).astype(BF16), jnp.where(head0_2, vrol, vcat).astype(BF16)]
            yield
            qn = [rms_heads(q_ref[rb, s], qnw_ref[:, s]) for s in psl]
            yield
            q2 = [jnp.concatenate([jnp.where(head0, q, 0.0), jnp.where(head0, 0.0, q)], axis=0).astype(BF16)
                  for q in qn]
            lg2 = [_dot_nt(q, kdup[kv]) for q, kv in zip(q2, kvh)]
            yield
            lg = [lg2[h // 2][(h % 2) * bq:(h % 2 + 1) * bq, :] + bias_ref[first, h] for h in heads]
            yield
            m = [jnp.maximum(jnp.max(l, axis=-1, keepdims=True), sink2[h]) for h, l in zip(heads, lg)]
            yield
            e = [jnp.exp2(l - mm) for l, mm in zip(lg, m)]
            yield
            denom = [jnp.sum(ee, axis=-1, keepdims=True) + jnp.exp2(sink2[h] - mm)
                     for h, ee, mm in zip(heads, e, m)]
            yield
            pv = [_dot(ee.astype(BF16), vdup[kvh[h // 2]]) for h, ee in zip(heads, e)]
            yield
            outs = [x / d for x, d in zip(pv, denom)]
            for p in pairs:
                o = jnp.where(head0, outs[2 * p], outs[2 * p + 1])
                g = g_refs[p // 2][rb, (p % 2) * LANES:(p % 2 + 1) * LANES]
                yatt_ref[rb, psl[p]] = (o * (g * _sigmoid(g))).astype(yatt_ref.dtype)
            yield
        kprev_ref[...] = kp
        vprev_ref[...] = vp

    proj = project_previous()
    n_levels = 9 * (tm // bq)
    every = n_levels // (D_MODEL // MXU_WIDTH)
    for n, _ in enumerate(attend()):
        if n % every == every - 1:
            next(proj, None)
    for _ in proj:
        pass


def _swa_out(z_att, bucket, sinks, rel_bias, qnw, knw, x2, y_rwkv, w_out, tm):
    s, d = x2.shape
    bq = BLOCK
    nb = s // tm
    smem = pl.BlockSpec(memory_space=pltpu.SMEM)
    cur = lambda i: jnp.minimum(i, nb - 1)
    prev = lambda i: jnp.maximum(i - 1, 0)
    kv_col = D_ATT // D_KV
    gw = 2 * D_KV
    gate = lambda n: pl.BlockSpec((tm, gw), lambda i: (cur(i), (D_ATT + 2 * D_KV) // gw + n))
    return pl.pallas_call(
        _swa_out_kernel,
        grid=(nb + 1,),
        in_specs=[
            smem, smem,
            pl.BlockSpec((bq, 2 * bq), lambda i: (0, 0)),
            pl.BlockSpec((tm, D_ATT), lambda i: (cur(i), 0)),
            pl.BlockSpec((tm, D_KV), lambda i: (cur(i), kv_col)),
            pl.BlockSpec((tm, D_KV), lambda i: (cur(i), kv_col + 1)),
            gate(0), gate(1), gate(2), gate(3),
            pl.BlockSpec((1, D_ATT), lambda i: (0, 0)),
            pl.BlockSpec((1, D_KV), lambda i: (0, 0)),
            pl.BlockSpec((tm, d), lambda i: (prev(i), 0)),
            pl.BlockSpec((tm, D_RWKV), lambda i: (prev(i), 0)),
            pl.BlockSpec((D_RWKV, d), lambda i: (0, 0)),
            pl.BlockSpec((D_ATT, d), lambda i: (D_RWKV // D_ATT, 0)),
        ],
        out_specs=pl.BlockSpec((tm, d), lambda i: (prev(i), 0)),
        out_shape=jax.ShapeDtypeStruct((s, d), F32),
        scratch_shapes=[
            pltpu.VMEM((2, N_Q_HEADS, bq, 2 * bq), F32),
            pltpu.VMEM((bq, D_KV), F32),
            pltpu.VMEM((bq, D_KV), F32),
            pltpu.VMEM((tm, D_ATT), BF16),
        ],
        compiler_params=pltpu.CompilerParams(
            dimension_semantics=("arbitrary",), vmem_limit_bytes=VMEM_LIMIT),
        name="swa_out",
    )(sinks, rel_bias, bucket, z_att, z_att, z_att, z_att, z_att, z_att, z_att, qnw, knw, x2, y_rwkv, w_out, w_out)


def _t5_bucket_table():
    dist = BLOCK + np.arange(BLOCK)[:, None] - np.arange(2 * BLOCK)[None, :]
    n = np.maximum(dist, 0)
    nf = np.maximum(n, 1).astype(np.float64)
    large = MAX_EXACT + (np.log(nf / MAX_EXACT) / math.log(MAX_DISTANCE / MAX_EXACT)
                         * (N_BUCKETS - MAX_EXACT)).astype(np.int32)
    large = np.minimum(large, N_BUCKETS - 1)
    return np.where(n < MAX_EXACT, n, large).astype(np.int32)


def kernel(x, norm_w, w_in, w_out, mu_rwkv, w0, w2, a0, a2, k_k, k_a, r_k, lnx_w, lnx_b,
           q_norm_w, k_norm_w, sinks, rel_bias):
    b, s, d = x.shape
    assert (b, s, d) == (1, SEQ, D_MODEL) and norm_w.shape[0] == 1
    x2 = x.reshape(s, d)
    l = 0
    row = lambda t: t.reshape(1, -1).astype(F32)

    w_in_l = w_in[l]
    zeros_l = jnp.zeros((LORA, D_RWKV), F32)
    w2p = jnp.concatenate([w2[l], zeros_l], axis=0).astype(BF16)
    a2p = jnp.concatenate([zeros_l, a2[l]], axis=0).astype(BF16)
    w_out_b = w_out[l].astype(BF16)
    qnw = jnp.tile(q_norm_w[l] * (HEAD_DIM ** -0.5 * LOG2E), N_Q_HEADS).reshape(1, D_ATT)
    knw = jnp.tile(k_norm_w[l], N_KV_HEADS).reshape(1, D_KV)
    bucket = jnp.asarray(_t5_bucket_table())

    nw = row(norm_w[l])
    z_rwkv = _proj_in(x2, nw, w_in_l, RWKV_COLS, row(mu_rwkv[l]), ROW_BLOCK, RWKV_COL_BLOCK)
    z_att, *prep = _att_prep(x2, nw, w_in_l, z_rwkv, row(w0[l]), w2p, row(a0[l]), a2p, row(k_k[l]), row(k_a[l]),
                             row(r_k[l]), ROW_BLOCK, ATT_COL_BLOCK, PREP_TOKENS)
    y_rwkv = _rwkv(prep, z_rwkv, row(lnx_w[l]), row(lnx_b[l]))
    out = _swa_out(z_att, bucket, sinks[l].astype(F32), rel_bias.astype(F32), qnw, knw,
                   x2, y_rwkv, w_out_b, ROW_BLOCK)
    return out.reshape(b, s, d)
```

```python
import math

import jax
import jax.numpy as jnp
import numpy as np
from jax import lax
from jax.experimental import pallas as pl
from jax.experimental.pallas import tpu as pltpu

F32 = jnp.float32
BF16 = jnp.bfloat16

D_MODEL = 2048
SEQ = 8192
HEAD_DIM = 64
D_RWKV = 1024
D_ATT = 1024
LORA = 64
N_Q_HEADS = 16
N_KV_HEADS = 2
D_KV = N_KV_HEADS * HEAD_DIM
WINDOW = 128
BLOCK = 128
N_BUCKETS = 32
MAX_EXACT = N_BUCKETS // 2
MAX_DISTANCE = 128
NORM_EPS = 1e-6
LNX_EPS = 64e-5
RWKV_COLS = 4 * D_RWKV + 2 * LORA
ATT_COLS = 2 * D_ATT + 2 * D_KV

LANES = 128
MXU_WIDTH = 256
N_PAIRS = D_RWKV // LANES
CHUNK = 64
RWKV_CHUNKS_PER_GROUP = 4
RWKV_CHUNKS_PER_STEP = 8
NEG_BIG = -1e30
LOG2E = math.log2(math.e)

ROW_BLOCK = 512
RWKV_COL_BLOCK = RWKV_COLS // 3
ATT_COL_BLOCK = ATT_COLS // 2
PREP_TOKENS = SEQ // ((ATT_COLS // ATT_COL_BLOCK) * (SEQ // ROW_BLOCK))
VMEM_LIMIT = 56 * 1024 * 1024


def _dot(a, b):
    return jnp.dot(a, b, preferred_element_type=F32)


def _dot_nt(a, b):
    return lax.dot_general(a, b, (((1,), (1,)), ((), ())), preferred_element_type=F32)


def _split3(x):
    hi = x.astype(BF16)
    r1 = x - hi.astype(F32)
    mid = r1.astype(BF16)
    lo = (r1 - mid.astype(F32)).astype(BF16)
    return hi, mid, lo


def _seg_sum(x, seg_ones):
    hi = x.astype(BF16)
    lo = (x - hi.astype(F32)).astype(BF16)
    return _dot(hi, seg_ones) + _dot(lo, seg_ones)


def _dot_exact_lhs(m, x):
    hi, mid, lo = _split3(x)
    return _dot(m, hi) + _dot(m, mid) + _dot(m, lo)


def _sigmoid(x):
    return 1.0 / (1.0 + jnp.exp(-x))


def _proj_in_kernel(x_ref, nw_ref, w_ref, mu_ref, o_ref, wb_ref, prev_ref):
    @pl.when(pl.program_id(1) == 0)
    def _():
        wb_ref[...] = w_ref[...].astype(BF16)
        prev_ref[...] = jnp.zeros_like(prev_ref)

    tm, tn = o_ref.shape
    x = x_ref[...]
    h = (x * nw_ref[...]).astype(BF16)
    rs = lax.rsqrt(jnp.mean(x * x, axis=-1, keepdims=True) + NORM_EPS)
    for c0 in range(0, tn, MXU_WIDTH):
        cs = slice(c0, min(c0 + MXU_WIDTH, tn))
        z = _dot(h, wb_ref[:, cs]) * rs
        row = lax.broadcasted_iota(jnp.int32, z.shape, 0)
        zprev = jnp.where(row == 0, prev_ref[:, cs], pltpu.roll(z, 1, axis=0))
        prev_ref[:, cs] = z[tm - 1:tm, :]
        o_ref[:, cs] = z + (zprev - z) * mu_ref[:, cs]


def _proj_in(x2, norm_w, w, n, mu, tm, tn):
    s, d = x2.shape
    return pl.pallas_call(
        _proj_in_kernel,
        grid=(n // tn, s // tm),
        in_specs=[
            pl.BlockSpec((tm, d), lambda j, i: (i, 0)),
            pl.BlockSpec((1, d), lambda j, i: (0, 0)),
            pl.BlockSpec((d, tn), lambda j, i: (0, j)),
            pl.BlockSpec((1, tn), lambda j, i: (0, j)),
        ],
        out_specs=pl.BlockSpec((tm, tn), lambda j, i: (i, j)),
        out_shape=jax.ShapeDtypeStruct((s, n), F32),
        scratch_shapes=[pltpu.VMEM((d, tn), BF16), pltpu.VMEM((1, tn), F32)],
        compiler_params=pltpu.CompilerParams(
            dimension_semantics=("arbitrary", "arbitrary"), vmem_limit_bytes=VMEM_LIMIT),
        name="proj_in_rwkv",
    )(x2, norm_w, w, mu)


def _att_prep_kernel(x_ref, nw_ref, wa_ref, wb_in_ref, wc_ref, r_ref, k_ref, v_ref, lora_ref,
                     w0_ref, w2_ref, a0_ref, a2_ref, kk_ref, ka_ref, rk_ref,
                     o_ref, rt_ref, kt_ref, at_ref, bt_ref, bonus_ref, el_ref, wb_ref):
    c = CHUNK

    @pl.when(pl.program_id(1) == 0)
    def _():
        wp = wa_ref.shape[1]
        for n, piece in enumerate((wa_ref, wb_in_ref, wc_ref)):
            wb_ref[:, n * wp:(n + 1) * wp] = piece[...].astype(BF16)

    tm, tn = o_ref.shape
    x = x_ref[...]
    h = (x * nw_ref[...]).astype(BF16)
    rs = lax.rsqrt(jnp.mean(x * x, axis=-1, keepdims=True) + NORM_EPS)

    def project():
        for c0 in range(0, tn, MXU_WIDTH):
            cs = slice(c0, min(c0 + MXU_WIDTH, tn))
            o_ref[:, cs] = _dot(h, wb_ref[:, cs]) * rs
            yield

    r128 = lax.broadcasted_iota(jnp.int32, (LANES, LANES), 0)
    c128 = lax.broadcasted_iota(jnp.int32, (LANES, LANES), 1)
    seg_ones = jnp.where((r128 // HEAD_DIM) == (c128 // HEAD_DIM), 1.0, 0.0).astype(BF16)
    ti = lax.broadcasted_iota(jnp.int32, (c, c), 0)
    si = lax.broadcasted_iota(jnp.int32, (c, c), 1)
    tril_ones = jnp.where(si <= ti, 1.0, 0.0).astype(BF16)
    pairs = range(N_PAIRS)
    psl = [slice(p * LANES, (p + 1) * LANES) for p in pairs]

    def seg_sum_pairs(y):
        z = _seg_sum(jnp.concatenate([y[:, s] for s in psl], axis=0), seg_ones)
        return jnp.concatenate([z[p * c:(p + 1) * c] for p in pairs], axis=1)

    def prepare():
        lora_in = lora_ref[...]
        u_lin = _dot(jnp.tanh(lora_in).astype(BF16), w2_ref[...])
        a_lin = _dot(lora_in.astype(BF16), a2_ref[...])
        yield
        for ci in range(r_ref.shape[0] // c):
            rows = slice(ci * c, (ci + 1) * c)
            logw = -math.exp(-0.5) * _sigmoid(w0_ref[...] + u_lin[rows, :])
            av = _sigmoid(a0_ref[...] + a_lin[rows, :])
            logp = _dot_exact_lhs(tril_ones, logw)
            e_p = jnp.exp(logp)
            e_n = 1.0 / e_p
            el_ref[0, ci:ci + 1, :] = e_p[c - 1:c, :]
            yield
            r = r_ref[rows, :]
            k = k_ref[rows, :]
            v = v_ref[rows, :]
            kmod = k * (1.0 + (av - 1.0) * ka_ref[...])
            rt_ref[rows, :] = (r * e_p).astype(rt_ref.dtype)
            kt_ref[rows, :] = (kmod * e_n).astype(kt_ref.dtype)
            bonus_ref[rows, :] = seg_sum_pairs(r * kmod * rk_ref[...]) * v
            yield
            kk = k * kk_ref[...]
            kkn = kk * lax.rsqrt(jnp.maximum(seg_sum_pairs(kk * kk), 1e-24))
            at_ref[rows, :] = (-kkn * jnp.exp(logp - logw)).astype(at_ref.dtype)
            bt_ref[rows, :] = (kkn * av * e_n).astype(bt_ref.dtype)
            yield

    proj = project()
    n_slices = 1 + 3 * (r_ref.shape[0] // c)
    every = max(1, n_slices // (-(-tn // MXU_WIDTH)))
    for n, _ in enumerate(prepare()):
        if n % every == every - 1:
            next(proj, None)
    for _ in proj:
        pass


def _att_prep(x2, norm_w, w_in, z_rwkv, w0, w2p, a0, a2p, k_k, k_a, r_k, tm, tn, tp):
    s, d = x2.shape
    n = ATT_COLS
    nj, ni = n // tn, s // tm
    wp = tn // 3
    assert nj * ni * tp == s and tp % CHUNK == 0 and RWKV_COLS % wp == 0 and wp % LANES == 0
    piece = lambda p: pl.BlockSpec((d, wp), lambda j, i: (0, RWKV_COLS // wp + 3 * j + p))
    blk = lambda j, i: j * ni + i
    row = pl.BlockSpec((1, D_RWKV), lambda j, i: (0, 0))
    lora_w = pl.BlockSpec((2 * LORA, D_RWKV), lambda j, i: (0, 0))
    zcol = lambda cb: pl.BlockSpec((tp, D_RWKV), lambda j, i: (blk(j, i), cb))
    tok = pl.BlockSpec((tp, D_RWKV), lambda j, i: (blk(j, i), 0))
    sds = lambda dt: jax.ShapeDtypeStruct((s, D_RWKV), dt)
    return pl.pallas_call(
        _att_prep_kernel,
        grid=(nj, ni),
        in_specs=[
            pl.BlockSpec((tm, d), lambda j, i: (i, 0)),
            pl.BlockSpec((1, d), lambda j, i: (0, 0)),
            piece(0), piece(1), piece(2),
            zcol(0), zcol(1), zcol(2),
            pl.BlockSpec((tp, 2 * LORA), lambda j, i: (blk(j, i), 4 * D_RWKV // (2 * LORA))),
            row, lora_w, row, lora_w, row, row, row,
        ],
        out_specs=[
            pl.BlockSpec((tm, tn), lambda j, i: (i, j)),
            tok, tok, tok, tok, tok,
            pl.BlockSpec((1, tp // CHUNK, D_RWKV), lambda j, i: (blk(j, i), 0, 0)),
        ],
        out_shape=[
            jax.ShapeDtypeStruct((s, n), F32),
            sds(BF16), sds(BF16), sds(BF16), sds(BF16), sds(F32),
            jax.ShapeDtypeStruct((s // tp, tp // CHUNK, D_RWKV), F32),
        ],
        scratch_shapes=[pltpu.VMEM((d, tn), BF16)],
        compiler_params=pltpu.CompilerParams(
            dimension_semantics=("arbitrary", "arbitrary"), vmem_limit_bytes=VMEM_LIMIT),
        name="proj_att_rwkv_prep",
    )(x2, norm_w, w_in, w_in, w_in, z_rwkv, z_rwkv, z_rwkv, z_rwkv, w0, w2p, a0, a2p, k_k, k_a, r_k)


def _rwkv_kernel(rt_ref, kt_ref, at_ref, bt_ref, bonus_ref, el_ref, v_ref, g_ref,
                 lw_ref, lb_ref, o_ref, gt_ref):
    c = CHUNK
    nch = RWKV_CHUNKS_PER_STEP
    i = pl.program_id(0)

    @pl.when(i == 0)
    def _():
        gt_ref[...] = jnp.zeros_like(gt_ref)

    lane = lax.broadcasted_iota(jnp.int32, (c, LANES), 1)
    head0 = lane < HEAD_DIM
    trow = lax.broadcasted_iota(jnp.int32, (c, LANES), 0)
    eye_cat = jnp.where((lane % HEAD_DIM) == trow, 1.0, 0.0)
    t2 = lax.broadcasted_iota(jnp.int32, (c, 2 * LANES), 0)
    s2 = lax.broadcasted_iota(jnp.int32, (c, 2 * LANES), 1) % HEAD_DIM
    strict2 = s2 < t2
    incl2 = s2 <= t2
    r128 = lax.broadcasted_iota(jnp.int32, (LANES, LANES), 0)
    c128 = lax.broadcasted_iota(jnp.int32, (LANES, LANES), 1)
    same_head = (r128 // HEAD_DIM) == (c128 // HEAD_DIM)
    zeros_c = jnp.zeros((c, LANES), F32)

    pairs = range(N_PAIRS)
    psl = [slice(p * LANES, (p + 1) * LANES) for p in pairs]

    def stack2(y):
        return jnp.concatenate([jnp.where(head0, y, 0.0), jnp.where(head0, 0.0, y)], axis=0)

    def catmul(a_cat, y):
        return _dot(a_cat.astype(BF16), stack2(y).astype(BF16))

    seg_mean = jnp.where(same_head, 1.0 / HEAD_DIM, 0.0).astype(BF16)

    def seg_sum_pairs(x, seg):
        y = _seg_sum(jnp.concatenate([x[:, s] for s in psl], axis=0), seg)
        return jnp.concatenate([y[p * c:(p + 1) * c] for p in pairs], axis=1)

    el_chunks = el_ref.shape[1]

    def intra(chunks, out):
        per_item = lambda ref: [ref[ci * c:(ci + 1) * c, s].astype(F32) for ci in chunks for s in psl]
        a_t, b_t, r_t, k_t, v_i = (per_item(ref) for ref in (at_ref, bt_ref, rt_ref, kt_ref, v_ref))
        e_last = [el_ref[ci // el_chunks, ci % el_chunks:ci % el_chunks + 1, s]
                  for ci in chunks for s in psl]
        sc = [_dot_nt(jnp.concatenate([a, r], axis=0).astype(BF16),
                      jnp.concatenate([stack2(b), stack2(k)], axis=0).astype(BF16))
              for a, r, b, k in zip(a_t, r_t, b_t, k_t)]
        yield
        l_all = [jnp.where(strict2, x[0:c, :], 0.0) for x in sc]
        a_rbk = [jnp.where(incl2, x[c:, :], 0.0) for x in sc]
        x = [l[:, 0:LANES] for l in l_all]
        tinv = [eye_cat + xi for xi in x]
        x = [catmul(xi, xi) for xi in x]
        yield
        for _ in range(4):
            tx = [catmul(jnp.concatenate([ti_, xi], axis=0), xi) for ti_, xi in zip(tinv, x)]
            tinv = [ti_ + y[0:c] for ti_, y in zip(tinv, tx)]
            x = [y[c:] for y in tx]
            yield
        tinv = [ti_ + catmul(ti_, xi) for ti_, xi in zip(tinv, x)]
        yield
        lakv = [catmul(l[:, LANES:], v) for l, v in zip(l_all, v_i)]
        yield
        aw = [_dot(ti_.astype(BF16), jnp.concatenate([stack2(a), stack2(y)], axis=1).astype(BF16))
              for ti_, a, y in zip(tinv, a_t, lakv)]
        a_eff = [y[:, 0:LANES] for y in aw]
        w_loc = [y[:, LANES:] for y in aw]
        yield
        qo = [_dot(a[:, 0:LANES].astype(BF16), jnp.concatenate([stack2(ae), stack2(w)], axis=1).astype(BF16))
              for a, ae, w in zip(a_rbk, a_eff, w_loc)]
        q_eff = [r + y[:, 0:LANES] for r, y in zip(r_t, qo)]
        yield
        o_loc = [y[:, LANES:] + catmul(a[:, LANES:], v) for y, a, v in zip(qo, a_rbk, v_i)]
        yield
        bh = [b * el for b, el in zip(b_t, e_last)]
        kh = [k * el for k, el in zip(k_t, e_last)]
        m_mat = [_dot(jnp.concatenate([ae, zeros_c], axis=0).T.astype(BF16),
                      jnp.concatenate([b, zeros_c], axis=0).astype(BF16)) for ae, b in zip(a_eff, bh)]
        yield
        n_mat = [_dot(jnp.concatenate([w, v], axis=0).T.astype(BF16),
                      jnp.concatenate([b, k], axis=0).astype(BF16)) for w, v, b, k in zip(w_loc, v_i, bh, kh)]
        out.update(q_eff=q_eff, o_loc=o_loc, m_mat=m_mat, n_mat=n_mat, e_last=e_last)

    def advance(gts, ic, j):
        sel = lambda name: ic[name][j * N_PAIRS:(j + 1) * N_PAIRS]
        gtb = [g.astype(BF16) for g in gts]
        o = [_dot_nt(q.astype(BF16), g) + ol for q, g, ol in zip(sel("q_eff"), gtb, sel("o_loc"))]
        gts = [jnp.where(same_head, g * el + _dot(gb, m.astype(BF16)) + n, 0.0)
               for g, gb, el, m, n in zip(gts, gtb, sel("e_last"), sel("m_mat"), sel("n_mat"))]
        return gts, jnp.concatenate(o, axis=1)

    def finish(ci, o):
        rs = slice(ci * c, (ci + 1) * c)
        d = o - seg_sum_pairs(o, seg_mean)
        var = seg_sum_pairs(d * d, seg_mean)
        on = d * lax.rsqrt(var + LNX_EPS) * lw_ref[...] + lb_ref[...]
        g = g_ref[rs, :]
        o_ref[rs, :] = ((on + bonus_ref[rs, :]) * (g * _sigmoid(g))).astype(o_ref.dtype)

    gsz = RWKV_CHUNKS_PER_GROUP
    ngroups = nch // gsz
    state = dict(gts=[gt_ref[p] for p in pairs])

    def tail_group(g, ic):
        for j in range(gsz):
            state["gts"], o = advance(state["gts"], ic, j)
            yield
            finish(g * gsz + j, o)
            yield

    def drive(main, sides):
        for _ in main:
            for s_ in sides:
                next(s_, None)
        for s_ in sides:
            for _ in s_:
                pass

    results = {}
    for g in range(ngroups):
        results[g] = {}
        sides = [tail_group(g - 1, results[g - 1])] if g >= 1 else []
        drive(intra(range(g * gsz, (g + 1) * gsz), results[g]), sides)
    drive(tail_group(ngroups - 1, results[ngroups - 1]), [])
    for p in pairs:
        gt_ref[p] = state["gts"][p]


def _rwkv(prep, z_rwkv, lnx_w, lnx_b):
    s = z_rwkv.shape[0]
    t = CHUNK * RWKV_CHUNKS_PER_STEP
    el = prep[-1]
    blocks_per_step = RWKV_CHUNKS_PER_STEP // el.shape[1]
    tok = pl.BlockSpec((t, D_RWKV), lambda i: (i, 0))
    row = pl.BlockSpec((1, D_RWKV), lambda i: (0, 0))
    return pl.pallas_call(
        _rwkv_kernel,
        grid=(s // t,),
        in_specs=[
            tok, tok, tok, tok, tok,
            pl.BlockSpec((blocks_per_step, el.shape[1], D_RWKV), lambda i: (i, 0, 0)),
            pl.BlockSpec((t, D_RWKV), lambda i: (i, 2)),
            pl.BlockSpec((t, D_RWKV), lambda i: (i, 3)),
            row, row,
        ],
        out_specs=tok,
        out_shape=jax.ShapeDtypeStruct((s, D_RWKV), BF16),
        scratch_shapes=[
            pltpu.VMEM((N_PAIRS, LANES, LANES), F32),
        ],
        compiler_params=pltpu.CompilerParams(
            dimension_semantics=("arbitrary",), vmem_limit_bytes=VMEM_LIMIT),
        name="rwkv7",
    )(*prep, z_rwkv, z_rwkv, lnx_w, lnx_b)


def _swa_out_kernel(sinks_ref, relb_ref, bucket_ref, q_ref, k_ref, v_ref, g0_ref, g1_ref, g2_ref, g3_ref,
                    qnw_ref, knw_ref, x_ref, yr_ref, wr_ref, wa_ref,
                    o_ref, bias_ref, kprev_ref, vprev_ref, yatt_ref):
    g_refs = (g0_ref, g1_ref, g2_ref, g3_ref)
    i = pl.program_id(0)
    bq = BLOCK
    tm = q_ref.shape[0]

    @pl.when(i == 0)
    def _():
        kprev_ref[...] = jnp.zeros_like(kprev_ref)
        vprev_ref[...] = jnp.zeros_like(vprev_ref)
        yatt_ref[...] = jnp.zeros_like(yatt_ref)
        bucket = bucket_ref[...]
        qi = lax.broadcasted_iota(jnp.int32, (bq, 2 * bq), 0)
        kj = lax.broadcasted_iota(jnp.int32, (bq, 2 * bq), 1)
        dist = bq + qi - kj
        inwin = (dist >= 0) & (dist < WINDOW)
        for h in range(N_Q_HEADS):
            acc = jnp.zeros((bq, 2 * bq), F32)
            for b in range(N_BUCKETS):
                acc = jnp.where(bucket == b, relb_ref[b, h], acc)
            acc = jnp.where(inwin, acc * LOG2E, NEG_BIG)
            bias_ref[0, h] = acc
            bias_ref[1, h] = jnp.where(kj >= bq, acc, NEG_BIG)

    first_step = jnp.where(i == 0, 1, 0)
    lane = lax.broadcasted_iota(jnp.int32, (bq, LANES), 1)
    head0 = lane < HEAD_DIM
    lane2 = lax.broadcasted_iota(jnp.int32, (2 * bq, LANES), 1)
    head0_2 = lane2 < HEAD_DIM
    r128 = lax.broadcasted_iota(jnp.int32, (LANES, LANES), 0)
    c128 = lax.broadcasted_iota(jnp.int32, (LANES, LANES), 1)
    seg_mean = jnp.where((r128 // HEAD_DIM) == (c128 // HEAD_DIM), 1.0 / HEAD_DIM, 0.0).astype(BF16)
    sink2 = [sinks_ref[h] * LOG2E for h in range(N_Q_HEADS)]
    pairs = range(N_PAIRS)
    heads = range(N_Q_HEADS)
    psl = [slice(p * LANES, (p + 1) * LANES) for p in pairs]
    kvh = [(2 * p) // (N_Q_HEADS // N_KV_HEADS) for p in pairs]

    def rms_heads(t, w):
        return t * lax.rsqrt(_seg_sum(t * t, seg_mean) + NORM_EPS) * w

    ya_prev = yatt_ref[...]
    yr_prev = yr_ref[...]

    def project_previous():
        for c0 in range(0, D_MODEL, MXU_WIDTH):
            cs = slice(c0, c0 + MXU_WIDTH)
            o_ref[:, cs] = x_ref[:, cs] + _dot(yr_prev, wr_ref[:, cs]) + _dot(ya_prev, wa_ref[:, cs])
            yield

    def attend():
        kp, vp = kprev_ref[...], vprev_ref[...]
        for b in range(tm // bq):
            rb = slice(b * bq, (b + 1) * bq)
            first = first_step if b == 0 else 0
            kn = rms_heads(k_ref[rb, :], knw_ref[...])
            vc = v_ref[rb, :]
            kcat = jnp.concatenate([kp, kn], axis=0)
            vcat = jnp.concatenate([vp, vc], axis=0)
            kp, vp = kn, vc
            krol = pltpu.roll(kcat, HEAD_DIM, axis=1)
            vrol = pltpu.roll(vcat, HEAD_DIM, axis=1)
            kdup = [jnp.where(head0_2, kcat, krol).astype(BF16), jnp.where(head0_2, krol, kcat).astype(BF16)]
            vdup = [jnp.where(head0_2, vcat, vrol).astype(BF16), jnp.where(head0_2, vrol, vcat).astype(BF16)]
            yield
            qn = [rms_heads(q_ref[rb, s], qnw_ref[:, s]) for s in psl]
            yield
            q2 = [jnp.concatenate([jnp.where(head0, q, 0.0), jnp.where(head0, 0.0, q)], axis=0).astype(BF16)
                  for q in qn]
            lg2 = [_dot_nt(q, kdup[kv]) for q, kv in zip(q2, kvh)]
            yield
            lg = [lg2[h // 2][(h % 2) * bq:(h % 2 + 1) * bq, :] + bias_ref[first, h] for h in heads]
            yield
            m = [jnp.maximum(jnp.max(l, axis=-1, keepdims=True), sink2[h]) for h, l in zip(heads, lg)]
            yield
            e = [jnp.exp2(l - mm) for l, mm in zip(lg, m)]
            yield
            denom = [jnp.sum(ee, axis=-1, keepdims=True) + jnp.exp2(sink2[h] - mm)
                     for h, ee, mm in zip(heads, e, m)]
            yield
            pv = [_dot(ee.astype(BF16), vdup[kvh[h // 2]]) for h, ee in zip(heads, e)]
            yield
            outs = [x / d for x, d in zip(pv, denom)]
            for p in pairs:
                o = jnp.where(head0, outs[2 * p], outs[2 * p + 1])
                g = g_refs[p // 2][rb, (p % 2) * LANES:(p % 2 + 1) * LANES]
                yatt_ref[rb, psl[p]] = (o * (g * _sigmoid(g))).astype(yatt_ref.dtype)
            yield
        kprev_ref[...] = kp
        vprev_ref[...] = vp

    proj = project_previous()
    n_levels = 9 * (tm // bq)
    every = n_levels // (D_MODEL // MXU_WIDTH)
    for n, _ in enumerate(attend()):
        if n % every == every - 1:
            next(proj, None)
    for _ in proj:
        pass


def _swa_out(z_att, bucket, sinks, rel_bias, qnw, knw, x2, y_rwkv, w_out, tm):
    s, d = x2.shape
    bq = BLOCK
    nb = s // tm
    smem = pl.BlockSpec(memory_space=pltpu.SMEM)
    cur = lambda i: jnp.minimum(i, nb - 1)
    prev = lambda i: jnp.maximum(i - 1, 0)
    kv_col = D_ATT // D_KV
    gw = 2 * D_KV
    gate = lambda n: pl.BlockSpec((tm, gw), lambda i: (cur(i), (D_ATT + 2 * D_KV) // gw + n))
    return pl.pallas_call(
        _swa_out_kernel,
        grid=(nb + 1,),
        in_specs=[
            smem, smem,
            pl.BlockSpec((bq, 2 * bq), lambda i: (0, 0)),
            pl.BlockSpec((tm, D_ATT), lambda i: (cur(i), 0)),
            pl.BlockSpec((tm, D_KV), lambda i: (cur(i), kv_col)),
            pl.BlockSpec((tm, D_KV), lambda i: (cur(i), kv_col + 1)),
            gate(0), gate(1), gate(2), gate(3),
            pl.BlockSpec((1, D_ATT), lambda i: (0, 0)),
            pl.BlockSpec((1, D_KV), lambda i: (0, 0)),
            pl.BlockSpec((tm, d), lambda i: (prev(i), 0)),
            pl.BlockSpec((tm, D_RWKV), lambda i: (prev(i), 0)),
            pl.BlockSpec((D_RWKV, d), lambda i: (0, 0)),
            pl.BlockSpec((D_ATT, d), lambda i: (D_RWKV // D_ATT, 0)),
        ],
        out_specs=pl.BlockSpec((tm, d), lambda i: (prev(i), 0)),
        out_shape=jax.ShapeDtypeStruct((s, d), F32),
        scratch_shapes=[
            pltpu.VMEM((2, N_Q_HEADS, bq, 2 * bq), F32),
            pltpu.VMEM((bq, D_KV), F32),
            pltpu.VMEM((bq, D_KV), F32),
            pltpu.VMEM((tm, D_ATT), BF16),
        ],
        compiler_params=pltpu.CompilerParams(
            dimension_semantics=("arbitrary",), vmem_limit_bytes=VMEM_LIMIT),
        name="swa_out",
    )(sinks, rel_bias, bucket, z_att, z_att, z_att, z_att, z_att, z_att, z_att, qnw, knw, x2, y_rwkv, w_out, w_out)


def _t5_bucket_table():
    dist = BLOCK + np.arange(BLOCK)[:, None] - np.arange(2 * BLOCK)[None, :]
    n = np.maximum(dist, 0)
    nf = np.maximum(n, 1).astype(np.float64)
    large = MAX_EXACT + (np.log(nf / MAX_EXACT) / math.log(MAX_DISTANCE / MAX_EXACT)
                         * (N_BUCKETS - MAX_EXACT)).astype(np.int32)
    large = np.minimum(large, N_BUCKETS - 1)
    return np.where(n < MAX_EXACT, n, large).astype(np.int32)


def kernel(x, norm_w, w_in, w_out, mu_rwkv, w0, w2, a0, a2, k_k, k_a, r_k, lnx_w, lnx_b,
           q_norm_w, k_norm_w, sinks, rel_bias):
    b, s, d = x.shape
    assert (b, s, d) == (1, SEQ, D_MODEL) and norm_w.shape[0] == 1
    x2 = x.reshape(s, d)
    l = 0
    row = lambda t: t.reshape(1, -1).astype(F32)

    w_in_l = w_in[l]
    zeros_l = jnp.zeros((LORA, D_RWKV), F32)
    w2p = jnp.concatenate([w2[l], zeros_l], axis=0).astype(BF16)
    a2p = jnp.concatenate([zeros_l, a2[l]], axis=0).astype(BF16)
    w_out_b = w_out[l].astype(BF16)
    qnw = jnp.tile(q_norm_w[l] * (HEAD_DIM ** -0.5 * LOG2E), N_Q_HEADS).reshape(1, D_ATT)
    knw = jnp.tile(k_norm_w[l], N_KV_HEADS).reshape(1, D_KV)
    bucket = jnp.asarray(_t5_bucket_table())

    nw = row(norm_w[l])
    z_rwkv = _proj_in(x2, nw, w_in_l, RWKV_COLS, row(mu_rwkv[l]), ROW_BLOCK, RWKV_COL_BLOCK)
    z_att, *prep = _att_prep(x2, nw, w_in_l, z_rwkv, row(w0[l]), w2p, row(a0[l]), a2p, row(k_k[l]), row(k_a[l]),
                             row(r_k[l]), ROW_BLOCK, ATT_COL_BLOCK, PREP_TOKENS)
    y_rwkv = _rwkv(prep, z_rwkv, row(lnx_w[l]), row(lnx_b[l]))
    out = _swa_out(z_att, bucket, sinks[l].astype(F32), rel_bias.astype(F32), qnw, knw,
                   x2, y_rwkv, w_out_b, ROW_BLOCK)
    return out.reshape(b, s, d)
```

```python
import math

import jax
import jax.numpy as jnp
import numpy as np
from jax import lax
from jax.experimental import pallas as pl
from jax.experimental.pallas import tpu as pltpu

F32 = jnp.float32
BF16 = jnp.bfloat16

D_MODEL = 2048
SEQ = 8192
HEAD_DIM = 64
D_RWKV = 1024
D_ATT = 1024
LORA = 64
N_Q_HEADS = 16
N_KV_HEADS = 2
D_KV = N_KV_HEADS * HEAD_DIM
WINDOW = 128
BLOCK = 128
N_BUCKETS = 32
MAX_EXACT = N_BUCKETS // 2
MAX_DISTANCE = 128
NORM_EPS = 1e-6
LNX_EPS = 64e-5
RWKV_COLS = 4 * D_RWKV + 2 * LORA
ATT_COLS = 2 * D_ATT + 2 * D_KV

LANES = 128
MXU_WIDTH = 256
N_PAIRS = D_RWKV // LANES
CHUNK = 64
RWKV_CHUNKS_PER_GROUP = 4
RWKV_CHUNKS_PER_STEP = 8
NEG_BIG = -1e30
LOG2E = math.log2(math.e)

ROW_BLOCK = 512
RWKV_COL_BLOCK = RWKV_COLS // 3
ATT_COL_BLOCK = ATT_COLS // 2
PREP_TOKENS = SEQ // ((ATT_COLS // ATT_COL_BLOCK) * (SEQ // ROW_BLOCK))
VMEM_LIMIT = 56 * 1024 * 1024


def _dot(a, b):
    return jnp.dot(a, b, preferred_element_type=F32)


def _dot_nt(a, b):
    return lax.dot_general(a, b, (((1,), (1,)), ((), ())), preferred_element_type=F32)


def _split3(x):
    hi = x.astype(BF16)
    r1 = x - hi.astype(F32)
    mid = r1.astype(BF16)
    lo = (r1 - mid.astype(F32)).astype(BF16)
    return hi, mid, lo


def _seg_sum(x, seg_ones):
    hi = x.astype(BF16)
    lo = (x - hi.astype(F32)).astype(BF16)
    return _dot(hi, seg_ones) + _dot(lo, seg_ones)


def _dot_exact_lhs(m, x):
    hi, mid, lo = _split3(x)
    return _dot(m, hi) + _dot(m, mid) + _dot(m, lo)


def _sigmoid(x):
    return 1.0 / (1.0 + jnp.exp(-x))


def _proj_in_kernel(x_ref, nw_ref, w_ref, mu_ref, o_ref, wb_ref, prev_ref):
    @pl.when(pl.program_id(1) == 0)
    def _():
        wb_ref[...] = w_ref[...].astype(BF16)
        prev_ref[...] = jnp.zeros_like(prev_ref)

    tm, tn = o_ref.shape
    x = x_ref[...]
    h = (x * nw_ref[...]).astype(BF16)
    rs = lax.rsqrt(jnp.mean(x * x, axis=-1, keepdims=True) + NORM_EPS)
    for c0 in range(0, tn, MXU_WIDTH):
        cs = slice(c0, min(c0 + MXU_WIDTH, tn))
        z = _dot(h, wb_ref[:, cs]) * rs
        row = lax.broadcasted_iota(jnp.int32, z.shape, 0)
        zprev = jnp.where(row == 0, prev_ref[:, cs], pltpu.roll(z, 1, axis=0))
        prev_ref[:, cs] = z[tm - 1:tm, :]
        o_ref[:, cs] = z + (zprev - z) * mu_ref[:, cs]


def _proj_in(x2, norm_w, w, n, mu, tm, tn):
    s, d = x2.shape
    return pl.pallas_call(
        _proj_in_kernel,
        grid=(n // tn, s // tm),
        in_specs=[
            pl.BlockSpec((tm, d), lambda j, i: (i, 0)),
            pl.BlockSpec((1, d), lambda j, i: (0, 0)),
            pl.BlockSpec((d, tn), lambda j, i: (0, j)),
            pl.BlockSpec((1, tn), lambda j, i: (0, j)),
        ],
        out_specs=pl.BlockSpec((tm, tn), lambda j, i: (i, j)),
        out_shape=jax.ShapeDtypeStruct((s, n), F32),
        scratch_shapes=[pltpu.VMEM((d, tn), BF16), pltpu.VMEM((1, tn), F32)],
        compiler_params=pltpu.CompilerParams(
            dimension_semantics=("arbitrary", "arbitrary"), vmem_limit_bytes=VMEM_LIMIT),
        name="proj_in_rwkv",
    )(x2, norm_w, w, mu)


def _att_prep_kernel(x_ref, nw_ref, wa_ref, wb_in_ref, wc_ref, r_ref, k_ref, v_ref, lora_ref,
                     w0_ref, w2_ref, a0_ref, a2_ref, kk_ref, ka_ref, rk_ref,
                     o_ref, rt_ref, kt_ref, at_ref, bt_ref, bonus_ref, el_ref, wb_ref):
    c = CHUNK

    @pl.when(pl.program_id(1) == 0)
    def _():
        wp = wa_ref.shape[1]
        for n, piece in enumerate((wa_ref, wb_in_ref, wc_ref)):
            wb_ref[:, n * wp:(n + 1) * wp] = piece[...].astype(BF16)

    tm, tn = o_ref.shape
    x = x_ref[...]
    h = (x * nw_ref[...]).astype(BF16)
    rs = lax.rsqrt(jnp.mean(x * x, axis=-1, keepdims=True) + NORM_EPS)

    def project():
        for c0 in range(0, tn, MXU_WIDTH):
            cs = slice(c0, min(c0 + MXU_WIDTH, tn))
            o_ref[:, cs] = _dot(h, wb_ref[:, cs]) * rs
            yield

    r128 = lax.broadcasted_iota(jnp.int32, (LANES, LANES), 0)
    c128 = lax.broadcasted_iota(jnp.int32, (LANES, LANES), 1)
    seg_ones = jnp.where((r128 // HEAD_DIM) == (c128 // HEAD_DIM), 1.0, 0.0).astype(BF16)
    ti = lax.broadcasted_iota(jnp.int32, (c, c), 0)
    si = lax.broadcasted_iota(jnp.int32, (c, c), 1)
    tril_ones = jnp.where(si <= ti, 1.0, 0.0).astype(BF16)
    pairs = range(N_PAIRS)
    psl = [slice(p * LANES, (p + 1) * LANES) for p in pairs]

    def seg_sum_pairs(y):
        z = _seg_sum(jnp.concatenate([y[:, s] for s in psl], axis=0), seg_ones)
        return jnp.concatenate([z[p * c:(p + 1) * c] for p in pairs], axis=1)

    def prepare():
        lora_in = lora_ref[...]
        u_lin = _dot(jnp.tanh(lora_in).astype(BF16), w2_ref[...])
        a_lin = _dot(lora_in.astype(BF16), a2_ref[...])
        yield
        for ci in range(r_ref.shape[0] // c):
            rows = slice(ci * c, (ci + 1) * c)
            logw = -math.exp(-0.5) * _sigmoid(w0_ref[...] + u_lin[rows, :])
            av = _sigmoid(a0_ref[...] + a_lin[rows, :])
            logp = _dot_exact_lhs(tril_ones, logw)
            e_p = jnp.exp(logp)
            e_n = 1.0 / e_p
            el_ref[0, ci:ci + 1, :] = e_p[c - 1:c, :]
            yield
            r = r_ref[rows, :]
            k = k_ref[rows, :]
            v = v_ref[rows, :]
            kmod = k * (1.0 + (av - 1.0) * ka_ref[...])
            rt_ref[rows, :] = (r * e_p).astype(rt_ref.dtype)
            kt_ref[rows, :] = (kmod * e_n).astype(kt_ref.dtype)
            bonus_ref[rows, :] = seg_sum_pairs(r * kmod * rk_ref[...]) * v
            yield
            kk = k * kk_ref[...]
            kkn = kk * lax.rsqrt(jnp.maximum(seg_sum_pairs(kk * kk), 1e-24))
            at_ref[rows, :] = (-kkn * jnp.exp(logp - logw)).astype(at_ref.dtype)
            bt_ref[rows, :] = (kkn * av * e_n).astype(bt_ref.dtype)
            yield

    proj = project()
    n_slices = 1 + 3 * (r_ref.shape[0] // c)
    every = max(1, n_slices // (-(-tn // MXU_WIDTH)))
    for n, _ in enumerate(prepare()):
        if n % every == every - 1:
            next(proj, None)
    for _ in proj:
        pass


def _att_prep(x2, norm_w, w_in, z_rwkv, w0, w2p, a0, a2p, k_k, k_a, r_k, tm, tn, tp):
    s, d = x2.shape
    n = ATT_COLS
    nj, ni = n // tn, s // tm
    wp = tn // 3
    assert nj * ni * tp == s and tp % CHUNK == 0 and RWKV_COLS % wp == 0 and wp % LANES == 0
    piece = lambda p: pl.BlockSpec((d, wp), lambda j, i: (0, RWKV_COLS // wp + 3 * j + p))
    blk = lambda j, i: j * ni + i
    row = pl.BlockSpec((1, D_RWKV), lambda j, i: (0, 0))
    lora_w = pl.BlockSpec((2 * LORA, D_RWKV), lambda j, i: (0, 0))
    zcol = lambda cb: pl.BlockSpec((tp, D_RWKV), lambda j, i: (blk(j, i), cb))
    tok = pl.BlockSpec((tp, D_RWKV), lambda j, i: (blk(j, i), 0))
    sds = lambda dt: jax.ShapeDtypeStruct((s, D_RWKV), dt)
    return pl.pallas_call(
        _att_prep_kernel,
        grid=(nj, ni),
        in_specs=[
            pl.BlockSpec((tm, d), lambda j, i: (i, 0)),
            pl.BlockSpec((1, d), lambda j, i: (0, 0)),
            piece(0), piece(1), piece(2),
            zcol(0), zcol(1), zcol(2),
            pl.BlockSpec((tp, 2 * LORA), lambda j, i: (blk(j, i), 4 * D_RWKV // (2 * LORA))),
            row, lora_w, row, lora_w, row, row, row,
        ],
        out_specs=[
            pl.BlockSpec((tm, tn), lambda j, i: (i, j)),
            tok, tok, tok, tok, tok,
            pl.BlockSpec((1, tp // CHUNK, D_RWKV), lambda j, i: (blk(j, i), 0, 0)),
        ],
        out_shape=[
            jax.ShapeDtypeStruct((s, n), F32),
            sds(BF16), sds(BF16), sds(BF16), sds(BF16), sds(F32),
            jax.ShapeDtypeStruct((s // tp, tp // CHUNK, D_RWKV), F32),
        ],
        scratch_shapes=[pltpu.VMEM((d, tn), BF16)],
        compiler_params=pltpu.CompilerParams(
            dimension_semantics=("arbitrary", "arbitrary"), vmem_limit_bytes=VMEM_LIMIT),
        name="proj_att_rwkv_prep",
    )(x2, norm_w, w_in, w_in, w_in, z_rwkv, z_rwkv, z_rwkv, z_rwkv, w0, w2p, a0, a2p, k_k, k_a, r_k)


def _rwkv_kernel(rt_ref, kt_ref, at_ref, bt_ref, bonus_ref, el_ref, v_ref, g_ref,
                 lw_ref, lb_ref, o_ref, gt_ref):
    c = CHUNK
    nch = RWKV_CHUNKS_PER_STEP
    i = pl.program_id(0)

    @pl.when(i == 0)
    def _():
        gt_ref[...] = jnp.zeros_like(gt_ref)

    lane = lax.broadcasted_iota(jnp.int32, (c, LANES), 1)
    head0 = lane < HEAD_DIM
    trow = lax.broadcasted_iota(jnp.int32, (c, LANES), 0)
    eye_cat = jnp.where((lane % HEAD_DIM) == trow, 1.0, 0.0)
    t2 = lax.broadcasted_iota(jnp.int32, (c, 2 * LANES), 0)
    s2 = lax.broadcasted_iota(jnp.int32, (c, 2 * LANES), 1) % HEAD_DIM
    strict2 = s2 < t2
    incl2 = s2 <= t2
    r128 = lax.broadcasted_iota(jnp.int32, (LANES, LANES), 0)
    c128 = lax.broadcasted_iota(jnp.int32, (LANES, LANES), 1)
    same_head = (r128 // HEAD_DIM) == (c128 // HEAD_DIM)
    zeros_c = jnp.zeros((c, LANES), F32)

    pairs = range(N_PAIRS)
    psl = [slice(p * LANES, (p + 1) * LANES) for p in pairs]

    def stack2(y):
        return jnp.concatenate([jnp.where(head0, y, 0.0), jnp.where(head0, 0.0, y)], axis=0)

    def catmul(a_cat, y):
        return _dot(a_cat.astype(BF16), stack2(y).astype(BF16))

    seg_mean = jnp.where(same_head, 1.0 / HEAD_DIM, 0.0).astype(BF16)

    def seg_sum_pairs(x, seg):
        y = _seg_sum(jnp.concatenate([x[:, s] for s in psl], axis=0), seg)
        return jnp.concatenate([y[p * c:(p + 1) * c] for p in pairs], axis=1)

    el_chunks = el_ref.shape[1]

    def intra(chunks, out):
        per_item = lambda ref: [ref[ci * c:(ci + 1) * c, s].astype(F32) for ci in chunks for s in psl]
        a_t, b_t, r_t, k_t, v_i = (per_item(ref) for ref in (at_ref, bt_ref, rt_ref, kt_ref, v_ref))
        e_last = [el_ref[ci // el_chunks, ci % el_chunks:ci % el_chunks + 1, s]
                  for ci in chunks for s in psl]
        sc = [_dot_nt(jnp.concatenate([a, r], axis=0).astype(BF16),
                      jnp.concatenate([stack2(b), stack2(k)], axis=0).astype(BF16))
              for a, r, b, k in zip(a_t, r_t, b_t, k_t)]
        yield
        l_all = [jnp.where(strict2, x[0:c, :], 0.0) for x in sc]
        a_rbk = [jnp.where(incl2, x[c:, :], 0.0) for x in sc]
        x = [l[:, 0:LANES] for l in l_all]
        tinv = [eye_cat + xi for xi in x]
        x = [catmul(xi, xi) for xi in x]
        yield
        for _ in range(4):
            tx = [catmul(jnp.concatenate([ti_, xi], axis=0), xi) for ti_, xi in zip(tinv, x)]
            tinv = [ti_ + y[0:c] for ti_, y in zip(tinv, tx)]
            x = [y[c:] for y in tx]
            yield
        tinv = [ti_ + catmul(ti_, xi) for ti_, xi in zip(tinv, x)]
        yield
        lav = [catmul(jnp.concatenate([l[:, LANES:], a[:, LANES:]], axis=0), v)
               for l, a, v in zip(l_all, a_rbk, v_i)]
        lakv = [y[0:c] for y in lav]
        arkv = [y[c:] for y in lav]
        yield
        aw = [_dot(ti_.astype(BF16), jnp.concatenate([stack2(a), stack2(y)], axis=1).astype(BF16))
              for ti_, a, y in zip(tinv, a_t, lakv)]
        a_eff = [y[:, 0:LANES] for y in aw]
        w_loc = [y[:, LANES:] for y in aw]
        yield
        qo = [_dot(a[:, 0:LANES].astype(BF16), jnp.concatenate([stack2(ae), stack2(w)], axis=1).astype(BF16))
              for a, ae, w in zip(a_rbk, a_eff, w_loc)]
        q_eff = [r + y[:, 0:LANES] for r, y in zip(r_t, qo)]
        o_loc = [y[:, LANES:] + av for y, av in zip(qo, arkv)]
        yield
        bh = [b * el for b, el in zip(b_t, e_last)]
        kh = [k * el for k, el in zip(k_t, e_last)]
        m_mat = [_dot(jnp.concatenate([ae, zeros_c], axis=0).T.astype(BF16),
                      jnp.concatenate([b, zeros_c], axis=0).astype(BF16)) for ae, b in zip(a_eff, bh)]
        yield
        n_mat = [_dot(jnp.concatenate([w, v], axis=0).T.astype(BF16),
                      jnp.concatenate([b, k], axis=0).astype(BF16)) for w, v, b, k in zip(w_loc, v_i, bh, kh)]
        out.update(q_eff=q_eff, o_loc=o_loc, m_mat=m_mat, n_mat=n_mat, e_last=e_last)

    def advance(gts, ic, j):
        sel = lambda name: ic[name][j * N_PAIRS:(j + 1) * N_PAIRS]
        gtb = [g.astype(BF16) for g in gts]
        o = [_dot_nt(q.astype(BF16), g) + ol for q, g, ol in zip(sel("q_eff"), gtb, sel("o_loc"))]
        gts = [jnp.where(same_head, g * el + _dot(gb, m.astype(BF16)) + n, 0.0)
               for g, gb, el, m, n in zip(gts, gtb, sel("e_last"), sel("m_mat"), sel("n_mat"))]
        return gts, jnp.concatenate(o, axis=1)

    def finish(ci, o):
        rs = slice(ci * c, (ci + 1) * c)
        d = o - seg_sum_pairs(o, seg_mean)
        var = seg_sum_pairs(d * d, seg_mean)
        on = d * lax.rsqrt(var + LNX_EPS) * lw_ref[...] + lb_ref[...]
        g = g_ref[rs, :]
        o_ref[rs, :] = ((on + bonus_ref[rs, :]) * (g * _sigmoid(g))).astype(o_ref.dtype)

    gsz = RWKV_CHUNKS_PER_GROUP
    ngroups = nch // gsz
    state = dict(gts=[gt_ref[p] for p in pairs])

    def tail_group(g, ic):
        for j in range(gsz):
            state["gts"], o = advance(state["gts"], ic, j)
            yield
            finish(g * gsz + j, o)
            yield

    def drive(main, sides):
        for _ in main:
            for s_ in sides:
                next(s_, None)
        for s_ in sides:
            for _ in s_:
                pass

    results = {}
    for g in range(ngroups):
        results[g] = {}
        sides = [tail_group(g - 1, results[g - 1])] if g >= 1 else []
        drive(intra(range(g * gsz, (g + 1) * gsz), results[g]), sides)
    drive(tail_group(ngroups - 1, results[ngroups - 1]), [])
    for p in pairs:
        gt_ref[p] = state["gts"][p]


def _rwkv(prep, z_rwkv, lnx_w, lnx_b):
    s = z_rwkv.shape[0]
    t = CHUNK * RWKV_CHUNKS_PER_STEP
    el = prep[-1]
    blocks_per_step = RWKV_CHUNKS_PER_STEP // el.shape[1]
    tok = pl.BlockSpec((t, D_RWKV), lambda i: (i, 0))
    row = pl.BlockSpec((1, D_RWKV), lambda i: (0, 0))
    return pl.pallas_call(
        _rwkv_kernel,
        grid=(s // t,),
        in_specs=[
            tok, tok, tok, tok, tok,
            pl.BlockSpec((blocks_per_step, el.shape[1], D_RWKV), lambda i: (i, 0, 0)),
            pl.BlockSpec((t, D_RWKV), lambda i: (i, 2)),
            pl.BlockSpec((t, D_RWKV), lambda i: (i, 3)),
            row, row,
        ],
        out_specs=tok,
        out_shape=jax.ShapeDtypeStruct((s, D_RWKV), BF16),
        scratch_shapes=[
            pltpu.VMEM((N_PAIRS, LANES, LANES), F32),
        ],
        compiler_params=pltpu.CompilerParams(
            dimension_semantics=("arbitrary",), vmem_limit_bytes=VMEM_LIMIT),
        name="rwkv7",
    )(*prep, z_rwkv, z_rwkv, lnx_w, lnx_b)


def _swa_out_kernel(sinks_ref, relb_ref, bucket_ref, q_ref, k_ref, v_ref, g0_ref, g1_ref, g2_ref, g3_ref,
                    qnw_ref, knw_ref, x_ref, yr_ref, wr_ref, wa_ref,
                    o_ref, bias_ref, kprev_ref, vprev_ref, yatt_ref):
    g_refs = (g0_ref, g1_ref, g2_ref, g3_ref)
    i = pl.program_id(0)
    bq = BLOCK
    tm = q_ref.shape[0]

    @pl.when(i == 0)
    def _():
        kprev_ref[...] = jnp.zeros_like(kprev_ref)
        vprev_ref[...] = jnp.zeros_like(vprev_ref)
        yatt_ref[...] = jnp.zeros_like(yatt_ref)
        bucket = bucket_ref[...]
        qi = lax.broadcasted_iota(jnp.int32, (bq, 2 * bq), 0)
        kj = lax.broadcasted_iota(jnp.int32, (bq, 2 * bq), 1)
        dist = bq + qi - kj
        inwin = (dist >= 0) & (dist < WINDOW)
        for h in range(N_Q_HEADS):
            acc = jnp.zeros((bq, 2 * bq), F32)
            for b in range(N_BUCKETS):
                acc = jnp.where(bucket == b, relb_ref[b, h], acc)
            acc = jnp.where(inwin, acc * LOG2E, NEG_BIG)
            bias_ref[0, h] = acc
            bias_ref[1, h] = jnp.where(kj >= bq, acc, NEG_BIG)

    first_step = jnp.where(i == 0, 1, 0)
    lane = lax.broadcasted_iota(jnp.int32, (bq, LANES), 1)
    head0 = lane < HEAD_DIM
    lane2 = lax.broadcasted_iota(jnp.int32, (2 * bq, LANES), 1)
    head0_2 = lane2 < HEAD_DIM
    r128 = lax.broadcasted_iota(jnp.int32, (LANES, LANES), 0)
    c128 = lax.broadcasted_iota(jnp.int32, (LANES, LANES), 1)
    seg_mean = jnp.where((r128 // HEAD_DIM) == (c128 // HEAD_DIM), 1.0 / HEAD_DIM, 0.0).astype(BF16)
    sink2 = [sinks_ref[h] * LOG2E for h in range(N_Q_HEADS)]
    pairs = range(N_PAIRS)
    heads = range(N_Q_HEADS)
    psl = [slice(p * LANES, (p + 1) * LANES) for p in pairs]
    kvh = [(2 * p) // (N_Q_HEADS // N_KV_HEADS) for p in pairs]

    def rms_heads(t, w):
        return t * lax.rsqrt(_seg_sum(t * t, seg_mean) + NORM_EPS) * w

    ya_prev = yatt_ref[...]
    yr_prev = yr_ref[...]

    def project_previous():
        for c0 in range(0, D_MODEL, MXU_WIDTH):
            cs = slice(c0, c0 + MXU_WIDTH)
            o_ref[:, cs] = x_ref[:, cs] + _dot(yr_prev, wr_ref[:, cs]) + _dot(ya_prev, wa_ref[:, cs])
            yield

    def attend():
        kp, vp = kprev_ref[...], vprev_ref[...]
        for b in range(tm // bq):
            rb = slice(b * bq, (b + 1) * bq)
            first = first_step if b == 0 else 0
            kn = rms_heads(k_ref[rb, :], knw_ref[...])
            vc = v_ref[rb, :]
            kcat = jnp.concatenate([kp, kn], axis=0)
            vcat = jnp.concatenate([vp, vc], axis=0)
            kp, vp = kn, vc
            krol = pltpu.roll(kcat, HEAD_DIM, axis=1)
            vrol = pltpu.roll(vcat, HEAD_DIM, axis=1)
            kdup = [jnp.where(head0_2, kcat, krol).astype(BF16), jnp.where(head0_2, krol, kcat).astype(BF16)]
            vdup = [jnp.where(head0_2, vcat, vrol).astype(BF16), jnp.where(head0_2, vrol, vcat).astype(BF16)]
            yield
            qn = [rms_heads(q_ref[rb, s], qnw_ref[:, s]) for s in psl]
            yield
            q2 = [jnp.concatenate([jnp.where(head0, q, 0.0), jnp.where(head0, 0.0, q)], axis=0).astype(BF16)
                  for q in qn]
            lg2 = [_dot_nt(q, kdup[kv]) for q, kv in zip(q2, kvh)]
            yield
            lg = [lg2[h // 2][(h % 2) * bq:(h % 2 + 1) * bq, :] + bias_ref[first, h] for h in heads]
            yield
            m = [jnp.maximum(jnp.max(l, axis=-1, keepdims=True), sink2[h]) for h, l in zip(heads, lg)]
            yield
            e = [jnp.exp2(l - mm) for l, mm in zip(lg, m)]
            yield
            denom = [jnp.sum(ee, axis=-1, keepdims=True) + jnp.exp2(sink2[h] - mm)
                     for h, ee, mm in zip(heads, e, m)]
            yield
            pv = [_dot(ee.astype(BF16), vdup[kvh[h // 2]]) for h, ee in zip(heads, e)]
            yield
            outs = [x / d for x, d in zip(pv, denom)]
            for p in pairs:
                o = jnp.where(head0, outs[2 * p], outs[2 * p + 1])
                g = g_refs[p // 2][rb, (p % 2) * LANES:(p % 2 + 1) * LANES]
                yatt_ref[rb, psl[p]] = (o * (g * _sigmoid(g))).astype(yatt_ref.dtype)
            yield
        kprev_ref[...] = kp
        vprev_ref[...] = vp

    proj = project_previous()
    n_levels = 9 * (tm // bq)
    every = n_levels // (D_MODEL // MXU_WIDTH)
    for n, _ in enumerate(attend()):
        if n % every == every - 1:
            next(proj, None)
    for _ in proj:
        pass


def _swa_out(z_att, bucket, sinks, rel_bias, qnw, knw, x2, y_rwkv, w_out, tm):
    s, d = x2.shape
    bq = BLOCK
    nb = s // tm
    smem = pl.BlockSpec(memory_space=pltpu.SMEM)
    cur = lambda i: jnp.minimum(i, nb - 1)
    prev = lambda i: jnp.maximum(i - 1, 0)
    kv_col = D_ATT // D_KV
    gw = 2 * D_KV
    gate = lambda n: pl.BlockSpec((tm, gw), lambda i: (cur(i), (D_ATT + 2 * D_KV) // gw + n))
    return pl.pallas_call(
        _swa_out_kernel,
        grid=(nb + 1,),
        in_specs=[
            smem, smem,
            pl.BlockSpec((bq, 2 * bq), lambda i: (0, 0)),
            pl.BlockSpec((tm, D_ATT), lambda i: (cur(i), 0)),
            pl.BlockSpec((tm, D_KV), lambda i: (cur(i), kv_col)),
            pl.BlockSpec((tm, D_KV), lambda i: (cur(i), kv_col + 1)),
            gate(0), gate(1), gate(2), gate(3),
            pl.BlockSpec((1, D_ATT), lambda i: (0, 0)),
            pl.BlockSpec((1, D_KV), lambda i: (0, 0)),
            pl.BlockSpec((tm, d), lambda i: (prev(i), 0)),
            pl.BlockSpec((tm, D_RWKV), lambda i: (prev(i), 0)),
            pl.BlockSpec((D_RWKV, d), lambda i: (0, 0)),
            pl.BlockSpec((D_ATT, d), lambda i: (D_RWKV // D_ATT, 0)),
        ],
        out_specs=pl.BlockSpec((tm, d), lambda i: (prev(i), 0)),
        out_shape=jax.ShapeDtypeStruct((s, d), F32),
        scratch_shapes=[
            pltpu.VMEM((2, N_Q_HEADS, bq, 2 * bq), F32),
            pltpu.VMEM((bq, D_KV), F32),
            pltpu.VMEM((bq, D_KV), F32),
            pltpu.VMEM((tm, D_ATT), BF16),
        ],
        compiler_params=pltpu.CompilerParams(
            dimension_semantics=("arbitrary",), vmem_limit_bytes=VMEM_LIMIT),
        name="swa_out",
    )(sinks, rel_bias, bucket, z_att, z_att, z_att, z_att, z_att, z_att, z_att, qnw, knw, x2, y_rwkv, w_out, w_out)


def _t5_bucket_table():
    dist = BLOCK + np.arange(BLOCK)[:, None] - np.arange(2 * BLOCK)[None, :]
    n = np.maximum(dist, 0)
    nf = np.maximum(n, 1).astype(np.float64)
    large = MAX_EXACT + (np.log(nf / MAX_EXACT) / math.log(MAX_DISTANCE / MAX_EXACT)
                         * (N_BUCKETS - MAX_EXACT)).astype(np.int32)
    large = np.minimum(large, N_BUCKETS - 1)
    return np.where(n < MAX_EXACT, n, large).astype(np.int32)


def kernel(x, norm_w, w_in, w_out, mu_rwkv, w0, w2, a0, a2, k_k, k_a, r_k, lnx_w, lnx_b,
           q_norm_w, k_norm_w, sinks, rel_bias):
    b, s, d = x.shape
    assert (b, s, d) == (1, SEQ, D_MODEL) and norm_w.shape[0] == 1
    x2 = x.reshape(s, d)
    l = 0
    row = lambda t: t.reshape(1, -1).astype(F32)

    w_in_l = w_in[l]
    zeros_l = jnp.zeros((LORA, D_RWKV), F32)
    w2p = jnp.concatenate([w2[l], zeros_l], axis=0).astype(BF16)
    a2p = jnp.concatenate([zeros_l, a2[l]], axis=0).astype(BF16)
    w_out_b = w_out[l].astype(BF16)
    qnw = jnp.tile(q_norm_w[l] * (HEAD_DIM ** -0.5 * LOG2E), N_Q_HEADS).reshape(1, D_ATT)
    knw = jnp.tile(k_norm_w[l], N_KV_HEADS).reshape(1, D_KV)
    bucket = jnp.asarray(_t5_bucket_table())

    nw = row(norm_w[l])
    z_rwkv = _proj_in(x2, nw, w_in_l, RWKV_COLS, row(mu_rwkv[l]), ROW_BLOCK, RWKV_COL_BLOCK)
    z_att, *prep = _att_prep(x2, nw, w_in_l, z_rwkv, row(w0[l]), w2p, row(a0[l]), a2p, row(k_k[l]), row(k_a[l]),
                             row(r_k[l]), ROW_BLOCK, ATT_COL_BLOCK, PREP_TOKENS)
    y_rwkv = _rwkv(prep, z_rwkv, row(lnx_w[l]), row(lnx_b[l]))
    out = _swa_out(z_att, bucket, sinks[l].astype(F32), rel_bias.astype(F32), qnw, knw,
                   x2, y_rwkv, w_out_b, ROW_BLOCK)
    return out.reshape(b, s, d)
```

```python
import math

import jax
import jax.numpy as jnp
import numpy as np
from jax import lax
from jax.experimental import pallas as pl
from jax.experimental.pallas import tpu as pltpu

F32 = jnp.float32
BF16 = jnp.bfloat16

D_MODEL = 2048
SEQ = 8192
HEAD_DIM = 64
D_RWKV = 1024
D_ATT = 1024
LORA = 64
N_Q_HEADS = 16
N_KV_HEADS = 2
D_KV = N_KV_HEADS * HEAD_DIM
WINDOW = 128
BLOCK = 128
N_BUCKETS = 32
MAX_EXACT = N_BUCKETS // 2
MAX_DISTANCE = 128
NORM_EPS = 1e-6
LNX_EPS = 64e-5
RWKV_COLS = 4 * D_RWKV + 2 * LORA
ATT_COLS = 2 * D_ATT + 2 * D_KV

LANES = 128
MXU_WIDTH = 256
N_PAIRS = D_RWKV // LANES
CHUNK = 64
RWKV_CHUNKS_PER_GROUP = 4
RWKV_CHUNKS_PER_STEP = 8
NEG_BIG = -1e30
LOG2E = math.log2(math.e)

ROW_BLOCK = 512
RWKV_COL_BLOCK = RWKV_COLS // 3
ATT_COL_BLOCK = ATT_COLS // 2
PREP_TOKENS = SEQ // ((ATT_COLS // ATT_COL_BLOCK) * (SEQ // ROW_BLOCK))
VMEM_LIMIT = 56 * 1024 * 1024


def _dot(a, b):
    return jnp.dot(a, b, preferred_element_type=F32)


def _dot_nt(a, b):
    return lax.dot_general(a, b, (((1,), (1,)), ((), ())), preferred_element_type=F32)


def _split3(x):
    hi = x.astype(BF16)
    r1 = x - hi.astype(F32)
    mid = r1.astype(BF16)
    lo = (r1 - mid.astype(F32)).astype(BF16)
    return hi, mid, lo


def _seg_sum(x, seg_ones):
    hi = x.astype(BF16)
    lo = (x - hi.astype(F32)).astype(BF16)
    return _dot(hi, seg_ones) + _dot(lo, seg_ones)


def _dot_exact_lhs(m, x):
    hi, mid, lo = _split3(x)
    return _dot(m, hi) + _dot(m, mid) + _dot(m, lo)


def _sigmoid(x):
    return 1.0 / (1.0 + jnp.exp(-x))


def _proj_in_kernel(x_ref, nw_ref, w_ref, mu_ref, o_ref, wb_ref, prev_ref):
    @pl.when(pl.program_id(1) == 0)
    def _():
        wb_ref[...] = w_ref[...].astype(BF16)
        prev_ref[...] = jnp.zeros_like(prev_ref)

    tm, tn = o_ref.shape
    x = x_ref[...]
    h = (x * nw_ref[...]).astype(BF16)
    rs = lax.rsqrt(jnp.mean(x * x, axis=-1, keepdims=True) + NORM_EPS)
    for c0 in range(0, tn, MXU_WIDTH):
        cs = slice(c0, min(c0 + MXU_WIDTH, tn))
        z = _dot(h, wb_ref[:, cs]) * rs
        row = lax.broadcasted_iota(jnp.int32, z.shape, 0)
        zprev = jnp.where(row == 0, prev_ref[:, cs], pltpu.roll(z, 1, axis=0))
        prev_ref[:, cs] = z[tm - 1:tm, :]
        o_ref[:, cs] = z + (zprev - z) * mu_ref[:, cs]


def _proj_in(x2, norm_w, w, n, mu, tm, tn):
    s, d = x2.shape
    return pl.pallas_call(
        _proj_in_kernel,
        grid=(n // tn, s // tm),
        in_specs=[
            pl.BlockSpec((tm, d), lambda j, i: (i, 0)),
            pl.BlockSpec((1, d), lambda j, i: (0, 0)),
            pl.BlockSpec((d, tn), lambda j, i: (0, j)),
            pl.BlockSpec((1, tn), lambda j, i: (0, j)),
        ],
        out_specs=pl.BlockSpec((tm, tn), lambda j, i: (i, j)),
        out_shape=jax.ShapeDtypeStruct((s, n), F32),
        scratch_shapes=[pltpu.VMEM((d, tn), BF16), pltpu.VMEM((1, tn), F32)],
        compiler_params=pltpu.CompilerParams(
            dimension_semantics=("arbitrary", "arbitrary"), vmem_limit_bytes=VMEM_LIMIT),
        name="proj_in_rwkv",
    )(x2, norm_w, w, mu)


def _att_prep_kernel(x_ref, nw_ref, wa_ref, wb_in_ref, wc_ref, r_ref, k_ref, v_ref, lora_ref,
                     w0_ref, w2_ref, a0_ref, a2_ref, kk_ref, ka_ref, rk_ref,
                     o_ref, rt_ref, kt_ref, at_ref, bt_ref, bonus_ref, el_ref, wb_ref):
    c = CHUNK

    @pl.when(pl.program_id(1) == 0)
    def _():
        wp = wa_ref.shape[1]
        for n, piece in enumerate((wa_ref, wb_in_ref, wc_ref)):
            wb_ref[:, n * wp:(n + 1) * wp] = piece[...].astype(BF16)

    tm, tn = o_ref.shape
    x = x_ref[...]
    h = (x * nw_ref[...]).astype(BF16)
    rs = lax.rsqrt(jnp.mean(x * x, axis=-1, keepdims=True) + NORM_EPS)

    def project():
        for c0 in range(0, tn, MXU_WIDTH):
            cs = slice(c0, min(c0 + MXU_WIDTH, tn))
            o_ref[:, cs] = _dot(h, wb_ref[:, cs]) * rs
            yield

    r128 = lax.broadcasted_iota(jnp.int32, (LANES, LANES), 0)
    c128 = lax.broadcasted_iota(jnp.int32, (LANES, LANES), 1)
    seg_ones = jnp.where((r128 // HEAD_DIM) == (c128 // HEAD_DIM), 1.0, 0.0).astype(BF16)
    ti = lax.broadcasted_iota(jnp.int32, (c, c), 0)
    si = lax.broadcasted_iota(jnp.int32, (c, c), 1)
    tril_ones = jnp.where(si <= ti, 1.0, 0.0).astype(BF16)
    pairs = range(N_PAIRS)
    psl = [slice(p * LANES, (p + 1) * LANES) for p in pairs]

    def seg_sum_pairs(y):
        z = _seg_sum(jnp.concatenate([y[:, s] for s in psl], axis=0), seg_ones)
        return jnp.concatenate([z[p * c:(p + 1) * c] for p in pairs], axis=1)

    def prepare():
        lora_in = lora_ref[...]
        u_lin = _dot(jnp.tanh(lora_in).astype(BF16), w2_ref[...])
        a_lin = _dot(lora_in.astype(BF16), a2_ref[...])
        yield
        for ci in range(r_ref.shape[0] // c):
            rows = slice(ci * c, (ci + 1) * c)
            logw = -math.exp(-0.5) * _sigmoid(w0_ref[...] + u_lin[rows, :])
            av = _sigmoid(a0_ref[...] + a_lin[rows, :])
            logp = _dot_exact_lhs(tril_ones, logw)
            e_p = jnp.exp(logp)
            e_n = 1.0 / e_p
            el_ref[0, ci:ci + 1, :] = e_p[c - 1:c, :]
            yield
            r = r_ref[rows, :]
            k = k_ref[rows, :]
            v = v_ref[rows, :]
            kmod = k * (1.0 + (av - 1.0) * ka_ref[...])
            rt_ref[rows, :] = (r * e_p).astype(rt_ref.dtype)
            kt_ref[rows, :] = (kmod * e_n).astype(kt_ref.dtype)
            bonus_ref[rows, :] = seg_sum_pairs(r * kmod * rk_ref[...]) * v
            yield
            kk = k * kk_ref[...]
            kkn = kk * lax.rsqrt(jnp.maximum(seg_sum_pairs(kk * kk), 1e-24))
            at_ref[rows, :] = (-kkn * jnp.exp(logp - logw)).astype(at_ref.dtype)
            bt_ref[rows, :] = (kkn * av * e_n).astype(bt_ref.dtype)
            yield

    proj = project()
    n_slices = 1 + 3 * (r_ref.shape[0] // c)
    every = max(1, n_slices // (-(-tn // MXU_WIDTH)))
    for n, _ in enumerate(prepare()):
        if n % every == every - 1:
            next(proj, None)
    for _ in proj:
        pass


def _att_prep(x2, norm_w, w_in, z_rwkv, w0, w2p, a0, a2p, k_k, k_a, r_k, tm, tn, tp):
    s, d = x2.shape
    n = ATT_COLS
    nj, ni = n // tn, s // tm
    wp = tn // 3
    assert nj * ni * tp == s and tp % CHUNK == 0 and RWKV_COLS % wp == 0 and wp % LANES == 0
    piece = lambda p: pl.BlockSpec((d, wp), lambda j, i: (0, RWKV_COLS // wp + 3 * j + p))
    blk = lambda j, i: j * ni + i
    row = pl.BlockSpec((1, D_RWKV), lambda j, i: (0, 0))
    lora_w = pl.BlockSpec((2 * LORA, D_RWKV), lambda j, i: (0, 0))
    zcol = lambda cb: pl.BlockSpec((tp, D_RWKV), lambda j, i: (blk(j, i), cb))
    tok = pl.BlockSpec((tp, D_RWKV), lambda j, i: (blk(j, i), 0))
    sds = lambda dt: jax.ShapeDtypeStruct((s, D_RWKV), dt)
    return pl.pallas_call(
        _att_prep_kernel,
        grid=(nj, ni),
        in_specs=[
            pl.BlockSpec((tm, d), lambda j, i: (i, 0)),
            pl.BlockSpec((1, d), lambda j, i: (0, 0)),
            piece(0), piece(1), piece(2),
            zcol(0), zcol(1), zcol(2),
            pl.BlockSpec((tp, 2 * LORA), lambda j, i: (blk(j, i), 4 * D_RWKV // (2 * LORA))),
            row, lora_w, row, lora_w, row, row, row,
        ],
        out_specs=[
            pl.BlockSpec((tm, tn), lambda j, i: (i, j)),
            tok, tok, tok, tok, tok,
            pl.BlockSpec((1, tp // CHUNK, D_RWKV), lambda j, i: (blk(j, i), 0, 0)),
        ],
        out_shape=[
            jax.ShapeDtypeStruct((s, n), F32),
            sds(BF16), sds(BF16), sds(BF16), sds(BF16), sds(F32),
            jax.ShapeDtypeStruct((s // tp, tp // CHUNK, D_RWKV), F32),
        ],
        scratch_shapes=[pltpu.VMEM((d, tn), BF16)],
        compiler_params=pltpu.CompilerParams(
            dimension_semantics=("arbitrary", "arbitrary"), vmem_limit_bytes=VMEM_LIMIT),
        name="proj_att_rwkv_prep",
    )(x2, norm_w, w_in, w_in, w_in, z_rwkv, z_rwkv, z_rwkv, z_rwkv, w0, w2p, a0, a2p, k_k, k_a, r_k)


def _rwkv_kernel(rt_ref, kt_ref, at_ref, bt_ref, bonus_ref, el_ref, v_ref, g_ref,
                 lw_ref, lb_ref, o_ref, gt_ref):
    c = CHUNK
    nch = RWKV_CHUNKS_PER_STEP
    i = pl.program_id(0)

    @pl.when(i == 0)
    def _():
        gt_ref[...] = jnp.zeros_like(gt_ref)

    lane = lax.broadcasted_iota(jnp.int32, (c, LANES), 1)
    head0 = lane < HEAD_DIM
    trow = lax.broadcasted_iota(jnp.int32, (c, LANES), 0)
    eye_cat = jnp.where((lane % HEAD_DIM) == trow, 1.0, 0.0)
    t2 = lax.broadcasted_iota(jnp.int32, (c, 2 * LANES), 0)
    s2 = lax.broadcasted_iota(jnp.int32, (c, 2 * LANES), 1) % HEAD_DIM
    strict2 = s2 < t2
    incl2 = s2 <= t2
    r128 = lax.broadcasted_iota(jnp.int32, (LANES, LANES), 0)
    c128 = lax.broadcasted_iota(jnp.int32, (LANES, LANES), 1)
    same_head = (r128 // HEAD_DIM) == (c128 // HEAD_DIM)
    zeros_c = jnp.zeros((c, LANES), F32)

    pairs = range(N_PAIRS)
    psl = [slice(p * LANES, (p + 1) * LANES) for p in pairs]

    def stack2(y):
        return jnp.concatenate([jnp.where(head0, y, 0.0), jnp.where(head0, 0.0, y)], axis=0)

    def catmul(a_cat, y):
        return _dot(a_cat.astype(BF16), stack2(y).astype(BF16))

    seg_mean = jnp.where(same_head, 1.0 / HEAD_DIM, 0.0).astype(BF16)

    def seg_sum_pairs(x, seg):
        y = _seg_sum(jnp.concatenate([x[:, s] for s in psl], axis=0), seg)
        return jnp.concatenate([y[p * c:(p + 1) * c] for p in pairs], axis=1)

    el_chunks = el_ref.shape[1]

    def intra(chunks, out):
        per_item = lambda ref: [ref[ci * c:(ci + 1) * c, s].astype(F32) for ci in chunks for s in psl]
        a_t, b_t, r_t, k_t, v_i = (per_item(ref) for ref in (at_ref, bt_ref, rt_ref, kt_ref, v_ref))
        e_last = [el_ref[ci // el_chunks, ci % el_chunks:ci % el_chunks + 1, s]
                  for ci in chunks for s in psl]
        sc = [_dot_nt(jnp.concatenate([a, r], axis=0).astype(BF16),
                      jnp.concatenate([stack2(b), stack2(k)], axis=0).astype(BF16))
              for a, r, b, k in zip(a_t, r_t, b_t, k_t)]
        yield
        l_all = [jnp.where(strict2, x[0:c, :], 0.0) for x in sc]
        a_rbk = [jnp.where(incl2, x[c:, :], 0.0) for x in sc]
        x = [l[:, 0:LANES] for l in l_all]
        tinv = [eye_cat + xi for xi in x]
        x = [catmul(xi, xi) for xi in x]
        yield
        for _ in range(4):
            tx = [catmul(jnp.concatenate([ti_, xi], axis=0), xi) for ti_, xi in zip(tinv, x)]
            tinv = [ti_ + y[0:c] for ti_, y in zip(tinv, tx)]
            x = [y[c:] for y in tx]
            yield
        tinv = [ti_ + catmul(ti_, xi) for ti_, xi in zip(tinv, x)]
        yield
        lav = [catmul(jnp.concatenate([l[:, LANES:], a[:, LANES:]], axis=0), v)
               for l, a, v in zip(l_all, a_rbk, v_i)]
        lakv = [y[0:c] for y in lav]
        arkv = [y[c:] for y in lav]
        yield
        aw = [_dot(ti_.astype(BF16), jnp.concatenate([stack2(a), stack2(y)], axis=1).astype(BF16))
              for ti_, a, y in zip(tinv, a_t, lakv)]
        a_eff = [y[:, 0:LANES] for y in aw]
        w_loc = [y[:, LANES:] for y in aw]
        yield
        qo = [_dot(a[:, 0:LANES].astype(BF16), jnp.concatenate([stack2(ae), stack2(w)], axis=1).astype(BF16))
              for a, ae, w in zip(a_rbk, a_eff, w_loc)]
        q_eff = [r + y[:, 0:LANES] for r, y in zip(r_t, qo)]
        o_loc = [y[:, LANES:] + av for y, av in zip(qo, arkv)]
        yield
        bh = [b * el for b, el in zip(b_t, e_last)]
        kh = [k * el for k, el in zip(k_t, e_last)]
        m_mat = [_dot(jnp.concatenate([ae, zeros_c], axis=0).T.astype(BF16),
                      jnp.concatenate([b, zeros_c], axis=0).astype(BF16)) for ae, b in zip(a_eff, bh)]
        yield
        n_mat = [_dot(jnp.concatenate([w, v], axis=0).T.astype(BF16),
                      jnp.concatenate([b, k], axis=0).astype(BF16)) for w, v, b, k in zip(w_loc, v_i, bh, kh)]
        out.update(q_eff=q_eff, o_loc=o_loc, m_mat=m_mat, n_mat=n_mat, e_last=e_last)

    def advance(gts, ic, j):
        sel = lambda name: ic[name][j * N_PAIRS:(j + 1) * N_PAIRS]
        gtb = [g.astype(BF16) for g in gts]
        o = [_dot_nt(q.astype(BF16), g) + ol for q, g, ol in zip(sel("q_eff"), gtb, sel("o_loc"))]
        gts = [jnp.where(same_head, g * el + _dot(gb, m.astype(BF16)) + n, 0.0)
               for g, gb, el, m, n in zip(gts, gtb, sel("e_last"), sel("m_mat"), sel("n_mat"))]
        return gts, jnp.concatenate(o, axis=1)

    def finish(ci, o):
        rs = slice(ci * c, (ci + 1) * c)
        d = o - seg_sum_pairs(o, seg_mean)
        var = seg_sum_pairs(d * d, seg_mean)
        on = d * lax.rsqrt(var + LNX_EPS) * lw_ref[...] + lb_ref[...]
        g = g_ref[rs, :]
        o_ref[rs, :] = ((on + bonus_ref[rs, :]) * (g * _sigmoid(g))).astype(o_ref.dtype)

    gsz = RWKV_CHUNKS_PER_GROUP
    ngroups = nch // gsz
    state = dict(gts=[gt_ref[p] for p in pairs])

    def tail_group(g, ic):
        for j in range(gsz):
            state["gts"], o = advance(state["gts"], ic, j)
            yield
            finish(g * gsz + j, o)
            yield

    def drive(main, sides):
        for _ in main:
            for s_ in sides:
                next(s_, None)
        for s_ in sides:
            for _ in s_:
                pass

    results = {}
    for g in range(ngroups):
        results[g] = {}
        sides = [tail_group(g - 1, results[g - 1])] if g >= 1 else []
        drive(intra(range(g * gsz, (g + 1) * gsz), results[g]), sides)
    drive(tail_group(ngroups - 1, results[ngroups - 1]), [])
    for p in pairs:
        gt_ref[p] = state["gts"][p]


def _rwkv(prep, z_rwkv, lnx_w, lnx_b):
    s = z_rwkv.shape[0]
    t = CHUNK * RWKV_CHUNKS_PER_STEP
    el = prep[-1]
    blocks_per_step = RWKV_CHUNKS_PER_STEP // el.shape[1]
    tok = pl.BlockSpec((t, D_RWKV), lambda i: (i, 0))
    row = pl.BlockSpec((1, D_RWKV), lambda i: (0, 0))
    return pl.pallas_call(
        _rwkv_kernel,
        grid=(s // t,),
        in_specs=[
            tok, tok, tok, tok, tok,
            pl.BlockSpec((blocks_per_step, el.shape[1], D_RWKV), lambda i: (i, 0, 0)),
            pl.BlockSpec((t, D_RWKV), lambda i: (i, 2)),
            pl.BlockSpec((t, D_RWKV), lambda i: (i, 3)),
            row, row,
        ],
        out_specs=tok,
        out_shape=jax.ShapeDtypeStruct((s, D_RWKV), BF16),
        scratch_shapes=[
            pltpu.VMEM((N_PAIRS, LANES, LANES), F32),
        ],
        compiler_params=pltpu.CompilerParams(
            dimension_semantics=("arbitrary",), vmem_limit_bytes=VMEM_LIMIT),
        name="rwkv7",
    )(*prep, z_rwkv, z_rwkv, lnx_w, lnx_b)


def _swa_out_kernel(sinks_ref, relb_ref, bucket_ref, q_ref, k_ref, v_ref, g0_ref, g1_ref, g2_ref, g3_ref,
                    qnw_ref, knw_ref, x_ref, yr_ref, wr_ref, wa_ref,
                    o_ref, bias_ref, kprev_ref, vprev_ref, yatt_ref):
    g_refs = (g0_ref, g1_ref, g2_ref, g3_ref)
    i = pl.program_id(0)
    bq = BLOCK
    tm = q_ref.shape[0]

    @pl.when(i == 0)
    def _():
        kprev_ref[...] = jnp.zeros_like(kprev_ref)
        vprev_ref[...] = jnp.zeros_like(vprev_ref)
        yatt_ref[...] = jnp.zeros_like(yatt_ref)
        bucket = bucket_ref[...]
        qi = lax.broadcasted_iota(jnp.int32, (bq, 2 * bq), 0)
        kj = lax.broadcasted_iota(jnp.int32, (bq, 2 * bq), 1)
        dist = bq + qi - kj
        inwin = (dist >= 0) & (dist < WINDOW)
        for h in range(N_Q_HEADS):
            acc = jnp.zeros((bq, 2 * bq), F32)
            for b in range(N_BUCKETS):
                acc = jnp.where(bucket == b, relb_ref[b, h], acc)
            acc = jnp.where(inwin, acc * LOG2E, NEG_BIG)
            bias_ref[0, h] = acc
            bias_ref[1, h] = jnp.where(kj >= bq, acc, NEG_BIG)

    first_step = jnp.where(i == 0, 1, 0)
    lane = lax.broadcasted_iota(jnp.int32, (bq, LANES), 1)
    head0 = lane < HEAD_DIM
    lane2 = lax.broadcasted_iota(jnp.int32, (2 * bq, LANES), 1)
    head0_2 = lane2 < HEAD_DIM
    r128 = lax.broadcasted_iota(jnp.int32, (LANES, LANES), 0)
    c128 = lax.broadcasted_iota(jnp.int32, (LANES, LANES), 1)
    seg_mean = jnp.where((r128 // HEAD_DIM) == (c128 // HEAD_DIM), 1.0 / HEAD_DIM, 0.0).astype(BF16)
    sink2 = [sinks_ref[h] * LOG2E for h in range(N_Q_HEADS)]
    pairs = range(N_PAIRS)
    heads = range(N_Q_HEADS)
    psl = [slice(p * LANES, (p + 1) * LANES) for p in pairs]
    kvh = [(2 * p) // (N_Q_HEADS // N_KV_HEADS) for p in pairs]

    def rms_heads(t, w):
        return t * lax.rsqrt(_seg_sum(t * t, seg_mean) + NORM_EPS) * w

    ya_prev = yatt_ref[...]
    yr_prev = yr_ref[...]

    def project_previous():
        for c0 in range(0, D_MODEL, MXU_WIDTH):
            cs = slice(c0, c0 + MXU_WIDTH)
            o_ref[:, cs] = x_ref[:, cs] + _dot(yr_prev, wr_ref[:, cs]) + _dot(ya_prev, wa_ref[:, cs])
            yield

    def attend():
        kp, vp = kprev_ref[...], vprev_ref[...]
        for b in range(tm // bq):
            rb = slice(b * bq, (b + 1) * bq)
            first = first_step if b == 0 else 0
            kn = rms_heads(k_ref[rb, :], knw_ref[...])
            vc = v_ref[rb, :]
            kcat = jnp.concatenate([kp, kn], axis=0)
            vcat = jnp.concatenate([vp, vc], axis=0)
            kp, vp = kn, vc
            krol = pltpu.roll(kcat, HEAD_DIM, axis=1)
            vrol = pltpu.roll(vcat, HEAD_DIM, axis=1)
            kdup = [jnp.where(head0_2, kcat, krol).astype(BF16), jnp.where(head0_2, krol, kcat).astype(BF16)]
            vdup = [jnp.where(head0_2, vcat, vrol).astype(BF16), jnp.where(head0_2, vrol, vcat).astype(BF16)]
            yield
            qn = [rms_heads(q_ref[rb, s], qnw_ref[:, s]) for s in psl]
            yield
            q2 = [jnp.concatenate([jnp.where(head0, q, 0.0), jnp.where(head0, 0.0, q)], axis=0).astype(BF16)
                  for q in qn]
            gh = N_Q_HEADS // N_KV_HEADS
            lgk = [_dot_nt(jnp.concatenate([q for q, kv in zip(q2, kvh) if kv == g], axis=0), kdup[g])
                   for g in range(N_KV_HEADS)]
            yield
            lg = [lgk[h // gh][(h % gh) * bq:(h % gh + 1) * bq, :] + bias_ref[first, h] for h in heads]
            yield
            m = [jnp.maximum(jnp.max(l, axis=-1, keepdims=True), sink2[h]) for h, l in zip(heads, lg)]
            yield
            e = [jnp.exp2(l - mm) for l, mm in zip(lg, m)]
            yield
            denom = [jnp.sum(ee, axis=-1, keepdims=True) + jnp.exp2(sink2[h] - mm)
                     for h, ee, mm in zip(heads, e, m)]
            yield
            pvk = [_dot(jnp.concatenate([ee.astype(BF16) for ee in e[g * gh:(g + 1) * gh]], axis=0), vdup[g])
                   for g in range(N_KV_HEADS)]
            pv = [pvk[h // gh][(h % gh) * bq:(h % gh + 1) * bq, :] for h in heads]
            yield
            outs = [x / d for x, d in zip(pv, denom)]
            for p in pairs:
                o = jnp.where(head0, outs[2 * p], outs[2 * p + 1])
                g = g_refs[p // 2][rb, (p % 2) * LANES:(p % 2 + 1) * LANES]
                yatt_ref[rb, psl[p]] = (o * (g * _sigmoid(g))).astype(yatt_ref.dtype)
            yield
        kprev_ref[...] = kp
        vprev_ref[...] = vp

    proj = project_previous()
    n_levels = 9 * (tm // bq)
    every = n_levels // (D_MODEL // MXU_WIDTH)
    for n, _ in enumerate(attend()):
        if n % every == every - 1:
            next(proj, None)
    for _ in proj:
        pass


def _swa_out(z_att, bucket, sinks, rel_bias, qnw, knw, x2, y_rwkv, w_out, tm):
    s, d = x2.shape
    bq = BLOCK
    nb = s // tm
    smem = pl.BlockSpec(memory_space=pltpu.SMEM)
    cur = lambda i: jnp.minimum(i, nb - 1)
    prev = lambda i: jnp.maximum(i - 1, 0)
    kv_col = D_ATT // D_KV
    gw = 2 * D_KV
    gate = lambda n: pl.BlockSpec((tm, gw), lambda i: (cur(i), (D_ATT + 2 * D_KV) // gw + n))
    return pl.pallas_call(
        _swa_out_kernel,
        grid=(nb + 1,),
        in_specs=[
            smem, smem,
            pl.BlockSpec((bq, 2 * bq), lambda i: (0, 0)),
            pl.BlockSpec((tm, D_ATT), lambda i: (cur(i), 0)),
            pl.BlockSpec((tm, D_KV), lambda i: (cur(i), kv_col)),
            pl.BlockSpec((tm, D_KV), lambda i: (cur(i), kv_col + 1)),
            gate(0), gate(1), gate(2), gate(3),
            pl.BlockSpec((1, D_ATT), lambda i: (0, 0)),
            pl.BlockSpec((1, D_KV), lambda i: (0, 0)),
            pl.BlockSpec((tm, d), lambda i: (prev(i), 0)),
            pl.BlockSpec((tm, D_RWKV), lambda i: (prev(i), 0)),
            pl.BlockSpec((D_RWKV, d), lambda i: (0, 0)),
            pl.BlockSpec((D_ATT, d), lambda i: (D_RWKV // D_ATT, 0)),
        ],
        out_specs=pl.BlockSpec((tm, d), lambda i: (prev(i), 0)),
        out_shape=jax.ShapeDtypeStruct((s, d), F32),
        scratch_shapes=[
            pltpu.VMEM((2, N_Q_HEADS, bq, 2 * bq), F32),
            pltpu.VMEM((bq, D_KV), F32),
            pltpu.VMEM((bq, D_KV), F32),
            pltpu.VMEM((tm, D_ATT), BF16),
        ],
        compiler_params=pltpu.CompilerParams(
            dimension_semantics=("arbitrary",), vmem_limit_bytes=VMEM_LIMIT),
        name="swa_out",
    )(sinks, rel_bias, bucket, z_att, z_att, z_att, z_att, z_att, z_att, z_att, qnw, knw, x2, y_rwkv, w_out, w_out)


def _t5_bucket_table():
    dist = BLOCK + np.arange(BLOCK)[:, None] - np.arange(2 * BLOCK)[None, :]
    n = np.maximum(dist, 0)
    nf = np.maximum(n, 1).astype(np.float64)
    large = MAX_EXACT + (np.log(nf / MAX_EXACT) / math.log(MAX_DISTANCE / MAX_EXACT)
                         * (N_BUCKETS - MAX_EXACT)).astype(np.int32)
    large = np.minimum(large, N_BUCKETS - 1)
    return np.where(n < MAX_EXACT, n, large).astype(np.int32)


def kernel(x, norm_w, w_in, w_out, mu_rwkv, w0, w2, a0, a2, k_k, k_a, r_k, lnx_w, lnx_b,
           q_norm_w, k_norm_w, sinks, rel_bias):
    b, s, d = x.shape
    assert (b, s, d) == (1, SEQ, D_MODEL) and norm_w.shape[0] == 1
    x2 = x.reshape(s, d)
    l = 0
    row = lambda t: t.reshape(1, -1).astype(F32)

    w_in_l = w_in[l]
    zeros_l = jnp.zeros((LORA, D_RWKV), F32)
    w2p = jnp.concatenate([w2[l], zeros_l], axis=0).astype(BF16)
    a2p = jnp.concatenate([zeros_l, a2[l]], axis=0).astype(BF16)
    w_out_b = w_out[l].astype(BF16)
    qnw = jnp.tile(q_norm_w[l] * (HEAD_DIM ** -0.5 * LOG2E), N_Q_HEADS).reshape(1, D_ATT)
    knw = jnp.tile(k_norm_w[l], N_KV_HEADS).reshape(1, D_KV)
    bucket = jnp.asarray(_t5_bucket_table())

    nw = row(norm_w[l])
    z_rwkv = _proj_in(x2, nw, w_in_l, RWKV_COLS, row(mu_rwkv[l]), ROW_BLOCK, RWKV_COL_BLOCK)
    z_att, *prep = _att_prep(x2, nw, w_in_l, z_rwkv, row(w0[l]), w2p, row(a0[l]), a2p, row(k_k[l]), row(k_a[l]),
                             row(r_k[l]), ROW_BLOCK, ATT_COL_BLOCK, PREP_TOKENS)
    y_rwkv = _rwkv(prep, z_rwkv, row(lnx_w[l]), row(lnx_b[l]))
    out = _swa_out(z_att, bucket, sinks[l].astype(F32), rel_bias.astype(F32), qnw, knw,
                   x2, y_rwkv, w_out_b, ROW_BLOCK)
    return out.reshape(b, s, d)
```

```python
import math

import jax
import jax.numpy as jnp
import numpy as np
from jax import lax
from jax.experimental import pallas as pl
from jax.experimental.pallas import tpu as pltpu

F32 = jnp.float32
BF16 = jnp.bfloat16

D_MODEL = 2048
SEQ = 8192
HEAD_DIM = 64
D_RWKV = 1024
D_ATT = 1024
LORA = 64
N_Q_HEADS = 16
N_KV_HEADS = 2
D_KV = N_KV_HEADS * HEAD_DIM
WINDOW = 128
BLOCK = 128
N_BUCKETS = 32
MAX_EXACT = N_BUCKETS // 2
MAX_DISTANCE = 128
NORM_EPS = 1e-6
LNX_EPS = 64e-5
RWKV_COLS = 4 * D_RWKV + 2 * LORA
ATT_COLS = 2 * D_ATT + 2 * D_KV

LANES = 128
MXU_WIDTH = 256
N_PAIRS = D_RWKV // LANES
CHUNK = 64
RWKV_GROUP_CHUNKS = (2, 2, 2, 2)
RWKV_CHUNKS_PER_STEP = sum(RWKV_GROUP_CHUNKS)
NEG_BIG = -1e30
LOG2E = math.log2(math.e)

ROW_BLOCK = 512
RWKV_COL_BLOCK = RWKV_COLS // 3
ATT_COL_BLOCK = ATT_COLS // 2
PREP_TOKENS = SEQ // ((ATT_COLS // ATT_COL_BLOCK) * (SEQ // ROW_BLOCK))
VMEM_LIMIT = 56 * 1024 * 1024


def _dot(a, b):
    return jnp.dot(a, b, preferred_element_type=F32)


def _dot_nt(a, b):
    return lax.dot_general(a, b, (((1,), (1,)), ((), ())), preferred_element_type=F32)


def _split3(x):
    hi = x.astype(BF16)
    r1 = x - hi.astype(F32)
    mid = r1.astype(BF16)
    lo = (r1 - mid.astype(F32)).astype(BF16)
    return hi, mid, lo


def _seg_sum(x, seg_ones):
    hi = x.astype(BF16)
    lo = (x - hi.astype(F32)).astype(BF16)
    return _dot(hi, seg_ones) + _dot(lo, seg_ones)


def _dot_exact_lhs(m, x):
    hi, mid, lo = _split3(x)
    return _dot(m, hi) + _dot(m, mid) + _dot(m, lo)


def _sigmoid(x):
    return 1.0 / (1.0 + jnp.exp(-x))


def _proj_in_kernel(x_ref, nw_ref, w_ref, mu_ref, o_ref, wb_ref, prev_ref):
    @pl.when(pl.program_id(1) == 0)
    def _():
        wb_ref[...] = w_ref[...].astype(BF16)
        prev_ref[...] = jnp.zeros_like(prev_ref)

    tm, tn = o_ref.shape
    x = x_ref[...]
    h = (x * nw_ref[...]).astype(BF16)
    rs = lax.rsqrt(jnp.mean(x * x, axis=-1, keepdims=True) + NORM_EPS)
    for c0 in range(0, tn, MXU_WIDTH):
        cs = slice(c0, min(c0 + MXU_WIDTH, tn))
        z = _dot(h, wb_ref[:, cs]) * rs
        row = lax.broadcasted_iota(jnp.int32, z.shape, 0)
        zprev = jnp.where(row == 0, prev_ref[:, cs], pltpu.roll(z, 1, axis=0))
        prev_ref[:, cs] = z[tm - 1:tm, :]
        o_ref[:, cs] = z + (zprev - z) * mu_ref[:, cs]


def _proj_in(x2, norm_w, w, n, mu, tm, tn):
    s, d = x2.shape
    return pl.pallas_call(
        _proj_in_kernel,
        grid=(n // tn, s // tm),
        in_specs=[
            pl.BlockSpec((tm, d), lambda j, i: (i, 0)),
            pl.BlockSpec((1, d), lambda j, i: (0, 0)),
            pl.BlockSpec((d, tn), lambda j, i: (0, j)),
            pl.BlockSpec((1, tn), lambda j, i: (0, j)),
        ],
        out_specs=pl.BlockSpec((tm, tn), lambda j, i: (i, j)),
        out_shape=jax.ShapeDtypeStruct((s, n), F32),
        scratch_shapes=[pltpu.VMEM((d, tn), BF16), pltpu.VMEM((1, tn), F32)],
        compiler_params=pltpu.CompilerParams(
            dimension_semantics=("arbitrary", "arbitrary"), vmem_limit_bytes=VMEM_LIMIT),
        name="proj_in_rwkv",
    )(x2, norm_w, w, mu)


def _att_prep_kernel(x_ref, nw_ref, wa_ref, wb_in_ref, wc_ref, r_ref, k_ref, v_ref, lora_ref,
                     w0_ref, w2_ref, a0_ref, a2_ref, kk_ref, ka_ref, rk_ref,
                     o_ref, rt_ref, kt_ref, at_ref, bt_ref, bonus_ref, el_ref, wb_ref):
    c = CHUNK

    @pl.when(pl.program_id(1) == 0)
    def _():
        wp = wa_ref.shape[1]
        for n, piece in enumerate((wa_ref, wb_in_ref, wc_ref)):
            wb_ref[:, n * wp:(n + 1) * wp] = piece[...].astype(BF16)

    tm, tn = o_ref.shape
    x = x_ref[...]
    h = (x * nw_ref[...]).astype(BF16)
    rs = lax.rsqrt(jnp.mean(x * x, axis=-1, keepdims=True) + NORM_EPS)

    def project():
        for c0 in range(0, tn, MXU_WIDTH):
            cs = slice(c0, min(c0 + MXU_WIDTH, tn))
            o_ref[:, cs] = _dot(h, wb_ref[:, cs]) * rs
            yield

    r128 = lax.broadcasted_iota(jnp.int32, (LANES, LANES), 0)
    c128 = lax.broadcasted_iota(jnp.int32, (LANES, LANES), 1)
    seg_ones = jnp.where((r128 // HEAD_DIM) == (c128 // HEAD_DIM), 1.0, 0.0).astype(BF16)
    ti = lax.broadcasted_iota(jnp.int32, (c, c), 0)
    si = lax.broadcasted_iota(jnp.int32, (c, c), 1)
    tril_ones = jnp.where(si <= ti, 1.0, 0.0).astype(BF16)
    pairs = range(N_PAIRS)
    psl = [slice(p * LANES, (p + 1) * LANES) for p in pairs]

    def seg_sum_pairs(y):
        z = _seg_sum(jnp.concatenate([y[:, s] for s in psl], axis=0), seg_ones)
        return jnp.concatenate([z[p * c:(p + 1) * c] for p in pairs], axis=1)

    def prepare():
        lora_in = lora_ref[...]
        u_lin = _dot(jnp.tanh(lora_in).astype(BF16), w2_ref[...])
        a_lin = _dot(lora_in.astype(BF16), a2_ref[...])
        yield
        for ci in range(r_ref.shape[0] // c):
            rows = slice(ci * c, (ci + 1) * c)
            logw = -math.exp(-0.5) * _sigmoid(w0_ref[...] + u_lin[rows, :])
            av = _sigmoid(a0_ref[...] + a_lin[rows, :])
            logp = _dot_exact_lhs(tril_ones, logw)
            e_p = jnp.exp(logp)
            e_n = 1.0 / e_p
            el_ref[0, ci:ci + 1, :] = e_p[c - 1:c, :]
            yield
            r = r_ref[rows, :]
            k = k_ref[rows, :]
            v = v_ref[rows, :]
            kmod = k * (1.0 + (av - 1.0) * ka_ref[...])
            rt_ref[rows, :] = (r * e_p).astype(rt_ref.dtype)
            kt_ref[rows, :] = (kmod * e_n).astype(kt_ref.dtype)
            bonus_ref[rows, :] = seg_sum_pairs(r * kmod * rk_ref[...]) * v
            yield
            kk = k * kk_ref[...]
            kkn = kk * lax.rsqrt(jnp.maximum(seg_sum_pairs(kk * kk), 1e-24))
            at_ref[rows, :] = (-kkn * jnp.exp(logp - logw)).astype(at_ref.dtype)
            bt_ref[rows, :] = (kkn * av * e_n).astype(bt_ref.dtype)
            yield

    proj = project()
    n_slices = 1 + 3 * (r_ref.shape[0] // c)
    every = max(1, n_slices // (-(-tn // MXU_WIDTH)))
    for n, _ in enumerate(prepare()):
        if n % every == every - 1:
            next(proj, None)
    for _ in proj:
        pass


def _att_prep(x2, norm_w, w_in, z_rwkv, w0, w2p, a0, a2p, k_k, k_a, r_k, tm, tn, tp):
    s, d = x2.shape
    n = ATT_COLS
    nj, ni = n // tn, s // tm
    wp = tn // 3
    assert nj * ni * tp == s and tp % CHUNK == 0 and RWKV_COLS % wp == 0 and wp % LANES == 0
    piece = lambda p: pl.BlockSpec((d, wp), lambda j, i: (0, RWKV_COLS // wp + 3 * j + p))
    blk = lambda j, i: j * ni + i
    row = pl.BlockSpec((1, D_RWKV), lambda j, i: (0, 0))
    lora_w = pl.BlockSpec((2 * LORA, D_RWKV), lambda j, i: (0, 0))
    zcol = lambda cb: pl.BlockSpec((tp, D_RWKV), lambda j, i: (blk(j, i), cb))
    tok = pl.BlockSpec((tp, D_RWKV), lambda j, i: (blk(j, i), 0))
    sds = lambda dt: jax.ShapeDtypeStruct((s, D_RWKV), dt)
    return pl.pallas_call(
        _att_prep_kernel,
        grid=(nj, ni),
        in_specs=[
            pl.BlockSpec((tm, d), lambda j, i: (i, 0)),
            pl.BlockSpec((1, d), lambda j, i: (0, 0)),
            piece(0), piece(1), piece(2),
            zcol(0), zcol(1), zcol(2),
            pl.BlockSpec((tp, 2 * LORA), lambda j, i: (blk(j, i), 4 * D_RWKV // (2 * LORA))),
            row, lora_w, row, lora_w, row, row, row,
        ],
        out_specs=[
            pl.BlockSpec((tm, tn), lambda j, i: (i, j)),
            tok, tok, tok, tok, tok,
            pl.BlockSpec((1, tp // CHUNK, D_RWKV), lambda j, i: (blk(j, i), 0, 0)),
        ],
        out_shape=[
            jax.ShapeDtypeStruct((s, n), F32),
            sds(BF16), sds(BF16), sds(BF16), sds(BF16), sds(F32),
            jax.ShapeDtypeStruct((s // tp, tp // CHUNK, D_RWKV), F32),
        ],
        scratch_shapes=[pltpu.VMEM((d, tn), BF16)],
        compiler_params=pltpu.CompilerParams(
            dimension_semantics=("arbitrary", "arbitrary"), vmem_limit_bytes=VMEM_LIMIT),
        name="proj_att_rwkv_prep",
    )(x2, norm_w, w_in, w_in, w_in, z_rwkv, z_rwkv, z_rwkv, z_rwkv, w0, w2p, a0, a2p, k_k, k_a, r_k)


def _rwkv_kernel(rt_ref, kt_ref, at_ref, bt_ref, bonus_ref, el_ref, v_ref, g_ref,
                 lw_ref, lb_ref, o_ref, gt_ref):
    c = CHUNK
    nch = RWKV_CHUNKS_PER_STEP
    i = pl.program_id(0)

    @pl.when(i == 0)
    def _():
        gt_ref[...] = jnp.zeros_like(gt_ref)

    lane = lax.broadcasted_iota(jnp.int32, (c, LANES), 1)
    head0 = lane < HEAD_DIM
    trow = lax.broadcasted_iota(jnp.int32, (c, LANES), 0)
    eye_cat = jnp.where((lane % HEAD_DIM) == trow, 1.0, 0.0)
    t2 = lax.broadcasted_iota(jnp.int32, (c, 2 * LANES), 0)
    s2 = lax.broadcasted_iota(jnp.int32, (c, 2 * LANES), 1) % HEAD_DIM
    strict2 = s2 < t2
    incl2 = s2 <= t2
    r128 = lax.broadcasted_iota(jnp.int32, (LANES, LANES), 0)
    c128 = lax.broadcasted_iota(jnp.int32, (LANES, LANES), 1)
    same_head = (r128 // HEAD_DIM) == (c128 // HEAD_DIM)
    zeros_c = jnp.zeros((c, LANES), F32)

    pairs = range(N_PAIRS)
    psl = [slice(p * LANES, (p + 1) * LANES) for p in pairs]

    def stack2(y):
        return jnp.concatenate([jnp.where(head0, y, 0.0), jnp.where(head0, 0.0, y)], axis=0)

    def catmul(a_cat, y):
        return _dot(a_cat.astype(BF16), stack2(y).astype(BF16))

    seg_mean = jnp.where(same_head, 1.0 / HEAD_DIM, 0.0).astype(BF16)

    def seg_sum_pairs(x, seg):
        y = _seg_sum(jnp.concatenate([x[:, s] for s in psl], axis=0), seg)
        return jnp.concatenate([y[p * c:(p + 1) * c] for p in pairs], axis=1)

    el_chunks = el_ref.shape[1]

    def intra(chunks, out):
        per_item = lambda ref: [ref[ci * c:(ci + 1) * c, s].astype(F32) for ci in chunks for s in psl]
        a_t, b_t, r_t, k_t, v_i = (per_item(ref) for ref in (at_ref, bt_ref, rt_ref, kt_ref, v_ref))
        e_last = [el_ref[ci // el_chunks, ci % el_chunks:ci % el_chunks + 1, s]
                  for ci in chunks for s in psl]
        sc = [_dot_nt(jnp.concatenate([a, r], axis=0).astype(BF16),
                      jnp.concatenate([stack2(b), stack2(k)], axis=0).astype(BF16))
              for a, r, b, k in zip(a_t, r_t, b_t, k_t)]
        yield
        l_all = [jnp.where(strict2, x[0:c, :], 0.0) for x in sc]
        a_rbk = [jnp.where(incl2, x[c:, :], 0.0) for x in sc]
        x = [l[:, 0:LANES] for l in l_all]
        tinv = [eye_cat + xi for xi in x]
        x = [catmul(xi, xi) for xi in x]
        yield
        for _ in range(4):
            tx = [catmul(jnp.concatenate([ti_, xi], axis=0), xi) for ti_, xi in zip(tinv, x)]
            tinv = [ti_ + y[0:c] for ti_, y in zip(tinv, tx)]
            x = [y[c:] for y in tx]
            yield
        tinv = [ti_ + catmul(ti_, xi) for ti_, xi in zip(tinv, x)]
        yield
        lav = [catmul(jnp.concatenate([l[:, LANES:], a[:, LANES:]], axis=0), v)
               for l, a, v in zip(l_all, a_rbk, v_i)]
        lakv = [y[0:c] for y in lav]
        arkv = [y[c:] for y in lav]
        yield
        aw = [_dot(ti_.astype(BF16), jnp.concatenate([stack2(a), stack2(y)], axis=1).astype(BF16))
              for ti_, a, y in zip(tinv, a_t, lakv)]
        a_eff = [y[:, 0:LANES] for y in aw]
        w_loc = [y[:, LANES:] for y in aw]
        yield
        qo = [_dot(a[:, 0:LANES].astype(BF16), jnp.concatenate([stack2(ae), stack2(w)], axis=1).astype(BF16))
              for a, ae, w in zip(a_rbk, a_eff, w_loc)]
        q_eff = [r + y[:, 0:LANES] for r, y in zip(r_t, qo)]
        o_loc = [y[:, LANES:] + av for y, av in zip(qo, arkv)]
        yield
        bh = [b * el for b, el in zip(b_t, e_last)]
        kh = [k * el for k, el in zip(k_t, e_last)]
        m_mat = [_dot(jnp.concatenate([ae, zeros_c], axis=0).T.astype(BF16),
                      jnp.concatenate([b, zeros_c], axis=0).astype(BF16)) for ae, b in zip(a_eff, bh)]
        yield
        n_mat = [_dot(jnp.concatenate([w, v], axis=0).T.astype(BF16),
                      jnp.concatenate([b, k], axis=0).astype(BF16)) for w, v, b, k in zip(w_loc, v_i, bh, kh)]
        out.update(q_eff=q_eff, o_loc=o_loc, m_mat=m_mat, n_mat=n_mat, e_last=e_last)

    def advance(gts, ic, j):
        sel = lambda name: ic[name][j * N_PAIRS:(j + 1) * N_PAIRS]
        gtb = [g.astype(BF16) for g in gts]
        o = [_dot_nt(q.astype(BF16), g) + ol for q, g, ol in zip(sel("q_eff"), gtb, sel("o_loc"))]
        gts = [jnp.where(same_head, g * el + _dot(gb, m.astype(BF16)) + n, 0.0)
               for g, gb, el, m, n in zip(gts, gtb, sel("e_last"), sel("m_mat"), sel("n_mat"))]
        return gts, jnp.concatenate(o, axis=1)

    def finish(ci, o):
        rs = slice(ci * c, (ci + 1) * c)
        d = o - seg_sum_pairs(o, seg_mean)
        var = seg_sum_pairs(d * d, seg_mean)
        on = d * lax.rsqrt(var + LNX_EPS) * lw_ref[...] + lb_ref[...]
        g = g_ref[rs, :]
        o_ref[rs, :] = ((on + bonus_ref[rs, :]) * (g * _sigmoid(g))).astype(o_ref.dtype)

    bounds = np.cumsum((0,) + RWKV_GROUP_CHUNKS)
    assert bounds[-1] == nch
    groups = [range(bounds[g], bounds[g + 1]) for g in range(len(RWKV_GROUP_CHUNKS))]
    ngroups = len(groups)
    state = dict(gts=[gt_ref[p] for p in pairs])

    def tail_group(g, ic):
        for j, ci in enumerate(groups[g]):
            state["gts"], o = advance(state["gts"], ic, j)
            yield
            finish(ci, o)
            yield

    def drive(main, sides):
        for _ in main:
            for s_ in sides:
                next(s_, None)
        for s_ in sides:
            for _ in s_:
                pass

    results = {}
    for g in range(ngroups):
        results[g] = {}
        sides = [tail_group(g - 1, results[g - 1])] if g >= 1 else []
        drive(intra(groups[g], results[g]), sides)
    drive(tail_group(ngroups - 1, results[ngroups - 1]), [])
    for p in pairs:
        gt_ref[p] = state["gts"][p]


def _rwkv(prep, z_rwkv, lnx_w, lnx_b):
    s = z_rwkv.shape[0]
    t = CHUNK * RWKV_CHUNKS_PER_STEP
    el = prep[-1]
    blocks_per_step = RWKV_CHUNKS_PER_STEP // el.shape[1]
    tok = pl.BlockSpec((t, D_RWKV), lambda i: (i, 0))
    row = pl.BlockSpec((1, D_RWKV), lambda i: (0, 0))
    return pl.pallas_call(
        _rwkv_kernel,
        grid=(s // t,),
        in_specs=[
            tok, tok, tok, tok, tok,
            pl.BlockSpec((blocks_per_step, el.shape[1], D_RWKV), lambda i: (i, 0, 0)),
            pl.BlockSpec((t, D_RWKV), lambda i: (i, 2)),
            pl.BlockSpec((t, D_RWKV), lambda i: (i, 3)),
            row, row,
        ],
        out_specs=tok,
        out_shape=jax.ShapeDtypeStruct((s, D_RWKV), BF16),
        scratch_shapes=[
            pltpu.VMEM((N_PAIRS, LANES, LANES), F32),
        ],
        compiler_params=pltpu.CompilerParams(
            dimension_semantics=("arbitrary",), vmem_limit_bytes=VMEM_LIMIT),
        name="rwkv7",
    )(*prep, z_rwkv, z_rwkv, lnx_w, lnx_b)


def _swa_out_kernel(sinks_ref, relb_ref, bucket_ref, q_ref, k_ref, v_ref, g0_ref, g1_ref, g2_ref, g3_ref,
                    qnw_ref, knw_ref, x_ref, yr_ref, wr_ref, wa_ref,
                    o_ref, bias_ref, kprev_ref, vprev_ref, yatt_ref):
    g_refs = (g0_ref, g1_ref, g2_ref, g3_ref)
    i = pl.program_id(0)
    bq = BLOCK
    tm = q_ref.shape[0]

    @pl.when(i == 0)
    def _():
        kprev_ref[...] = jnp.zeros_like(kprev_ref)
        vprev_ref[...] = jnp.zeros_like(vprev_ref)
        yatt_ref[...] = jnp.zeros_like(yatt_ref)
        bucket = bucket_ref[...]
        qi = lax.broadcasted_iota(jnp.int32, (bq, 2 * bq), 0)
        kj = lax.broadcasted_iota(jnp.int32, (bq, 2 * bq), 1)
        dist = bq + qi - kj
        inwin = (dist >= 0) & (dist < WINDOW)
        for h in range(N_Q_HEADS):
            acc = jnp.zeros((bq, 2 * bq), F32)
            for b in range(N_BUCKETS):
                acc = jnp.where(bucket == b, relb_ref[b, h], acc)
            acc = jnp.where(inwin, acc * LOG2E, NEG_BIG)
            bias_ref[0, h] = acc
            bias_ref[1, h] = jnp.where(kj >= bq, acc, NEG_BIG)

    first_step = jnp.where(i == 0, 1, 0)
    lane = lax.broadcasted_iota(jnp.int32, (bq, LANES), 1)
    head0 = lane < HEAD_DIM
    lane2 = lax.broadcasted_iota(jnp.int32, (2 * bq, LANES), 1)
    head0_2 = lane2 < HEAD_DIM
    r128 = lax.broadcasted_iota(jnp.int32, (LANES, LANES), 0)
    c128 = lax.broadcasted_iota(jnp.int32, (LANES, LANES), 1)
    seg_mean = jnp.where((r128 // HEAD_DIM) == (c128 // HEAD_DIM), 1.0 / HEAD_DIM, 0.0).astype(BF16)
    sink2 = [sinks_ref[h] * LOG2E for h in range(N_Q_HEADS)]
    pairs = range(N_PAIRS)
    heads = range(N_Q_HEADS)
    psl = [slice(p * LANES, (p + 1) * LANES) for p in pairs]
    kvh = [(2 * p) // (N_Q_HEADS // N_KV_HEADS) for p in pairs]

    def rms_heads(t, w):
        return t * lax.rsqrt(_seg_sum(t * t, seg_mean) + NORM_EPS) * w

    ya_prev = yatt_ref[...]
    yr_prev = yr_ref[...]

    def project_previous():
        for c0 in range(0, D_MODEL, MXU_WIDTH):
            cs = slice(c0, c0 + MXU_WIDTH)
            o_ref[:, cs] = x_ref[:, cs] + _dot(yr_prev, wr_ref[:, cs]) + _dot(ya_prev, wa_ref[:, cs])
            yield

    def attend():
        kp, vp = kprev_ref[...], vprev_ref[...]
        for b in range(tm // bq):
            rb = slice(b * bq, (b + 1) * bq)
            first = first_step if b == 0 else 0
            kn = rms_heads(k_ref[rb, :], knw_ref[...])
            vc = v_ref[rb, :]
            kcat = jnp.concatenate([kp, kn], axis=0)
            vcat = jnp.concatenate([vp, vc], axis=0)
            kp, vp = kn, vc
            krol = pltpu.roll(kcat, HEAD_DIM, axis=1)
            vrol = pltpu.roll(vcat, HEAD_DIM, axis=1)
            kdup = [jnp.where(head0_2, kcat, krol).astype(BF16), jnp.where(head0_2, krol, kcat).astype(BF16)]
            vdup = [jnp.where(head0_2, vcat, vrol).astype(BF16), jnp.where(head0_2, vrol, vcat).astype(BF16)]
            yield
            qn = [rms_heads(q_ref[rb, s], qnw_ref[:, s]) for s in psl]
            yield
            q2 = [jnp.concatenate([jnp.where(head0, q, 0.0), jnp.where(head0, 0.0, q)], axis=0).astype(BF16)
                  for q in qn]
            gh = N_Q_HEADS // N_KV_HEADS
            lgk = [_dot_nt(jnp.concatenate([q for q, kv in zip(q2, kvh) if kv == g], axis=0), kdup[g])
                   for g in range(N_KV_HEADS)]
            yield
            lg = [lgk[h // gh][(h % gh) * bq:(h % gh + 1) * bq, :] + bias_ref[first, h] for h in heads]
            yield
            m = [jnp.maximum(jnp.max(l, axis=-1, keepdims=True), sink2[h]) for h, l in zip(heads, lg)]
            yield
            e = [jnp.exp2(l - mm) for l, mm in zip(lg, m)]
            yield
            denom = [jnp.sum(ee, axis=-1, keepdims=True) + jnp.exp2(sink2[h] - mm)
                     for h, ee, mm in zip(heads, e, m)]
            yield
            pvk = [_dot(jnp.concatenate([ee.astype(BF16) for ee in e[g * gh:(g + 1) * gh]], axis=0), vdup[g])
                   for g in range(N_KV_HEADS)]
            pv = [pvk[h // gh][(h % gh) * bq:(h % gh + 1) * bq, :] for h in heads]
            yield
            outs = [x / d for x, d in zip(pv, denom)]
            for p in pairs:
                o = jnp.where(head0, outs[2 * p], outs[2 * p + 1])
                g = g_refs[p // 2][rb, (p % 2) * LANES:(p % 2 + 1) * LANES]
                yatt_ref[rb, psl[p]] = (o * (g * _sigmoid(g))).astype(yatt_ref.dtype)
            yield
        kprev_ref[...] = kp
        vprev_ref[...] = vp

    proj = project_previous()
    n_levels = 9 * (tm // bq)
    every = n_levels // (D_MODEL // MXU_WIDTH)
    for n, _ in enumerate(attend()):
        if n % every == every - 1:
            next(proj, None)
    for _ in proj:
        pass


def _swa_out(z_att, bucket, sinks, rel_bias, qnw, knw, x2, y_rwkv, w_out, tm):
    s, d = x2.shape
    bq = BLOCK
    nb = s // tm
    smem = pl.BlockSpec(memory_space=pltpu.SMEM)
    cur = lambda i: jnp.minimum(i, nb - 1)
    prev = lambda i: jnp.maximum(i - 1, 0)
    kv_col = D_ATT // D_KV
    gw = 2 * D_KV
    gate = lambda n: pl.BlockSpec((tm, gw), lambda i: (cur(i), (D_ATT + 2 * D_KV) // gw + n))
    return pl.pallas_call(
        _swa_out_kernel,
        grid=(nb + 1,),
        in_specs=[
            smem, smem,
            pl.BlockSpec((bq, 2 * bq), lambda i: (0, 0)),
            pl.BlockSpec((tm, D_ATT), lambda i: (cur(i), 0)),
            pl.BlockSpec((tm, D_KV), lambda i: (cur(i), kv_col)),
            pl.BlockSpec((tm, D_KV), lambda i: (cur(i), kv_col + 1)),
            gate(0), gate(1), gate(2), gate(3),
            pl.BlockSpec((1, D_ATT), lambda i: (0, 0)),
            pl.BlockSpec((1, D_KV), lambda i: (0, 0)),
            pl.BlockSpec((tm, d), lambda i: (prev(i), 0)),
            pl.BlockSpec((tm, D_RWKV), lambda i: (prev(i), 0)),
            pl.BlockSpec((D_RWKV, d), lambda i: (0, 0)),
            pl.BlockSpec((D_ATT, d), lambda i: (D_RWKV // D_ATT, 0)),
        ],
        out_specs=pl.BlockSpec((tm, d), lambda i: (prev(i), 0)),
        out_shape=jax.ShapeDtypeStruct((s, d), F32),
        scratch_shapes=[
            pltpu.VMEM((2, N_Q_HEADS, bq, 2 * bq), F32),
            pltpu.VMEM((bq, D_KV), F32),
            pltpu.VMEM((bq, D_KV), F32),
            pltpu.VMEM((tm, D_ATT), BF16),
        ],
        compiler_params=pltpu.CompilerParams(
            dimension_semantics=("arbitrary",), vmem_limit_bytes=VMEM_LIMIT),
        name="swa_out",
    )(sinks, rel_bias, bucket, z_att, z_att, z_att, z_att, z_att, z_att, z_att, qnw, knw, x2, y_rwkv, w_out, w_out)


def _t5_bucket_table():
    dist = BLOCK + np.arange(BLOCK)[:, None] - np.arange(2 * BLOCK)[None, :]
    n = np.maximum(dist, 0)
    nf = np.maximum(n, 1).astype(np.float64)
    large = MAX_EXACT + (np.log(nf / MAX_EXACT) / math.log(MAX_DISTANCE / MAX_EXACT)
                         * (N_BUCKETS - MAX_EXACT)).astype(np.int32)
    large = np.minimum(large, N_BUCKETS - 1)
    return np.where(n < MAX_EXACT, n, large).astype(np.int32)


def kernel(x, norm_w, w_in, w_out, mu_rwkv, w0, w2, a0, a2, k_k, k_a, r_k, lnx_w, lnx_b,
           q_norm_w, k_norm_w, sinks, rel_bias):
    b, s, d = x.shape
    assert (b, s, d) == (1, SEQ, D_MODEL) and norm_w.shape[0] == 1
    x2 = x.reshape(s, d)
    l = 0
    row = lambda t: t.reshape(1, -1).astype(F32)

    w_in_l = w_in[l]
    zeros_l = jnp.zeros((LORA, D_RWKV), F32)
    w2p = jnp.concatenate([w2[l], zeros_l], axis=0).astype(BF16)
    a2p = jnp.concatenate([zeros_l, a2[l]], axis=0).astype(BF16)
    w_out_b = w_out[l].astype(BF16)
    qnw = jnp.tile(q_norm_w[l] * (HEAD_DIM ** -0.5 * LOG2E), N_Q_HEADS).reshape(1, D_ATT)
    knw = jnp.tile(k_norm_w[l], N_KV_HEADS).reshape(1, D_KV)
    bucket = jnp.asarray(_t5_bucket_table())

    nw = row(norm_w[l])
    z_rwkv = _proj_in(x2, nw, w_in_l, RWKV_COLS, row(mu_rwkv[l]), ROW_BLOCK, RWKV_COL_BLOCK)
    z_att, *prep = _att_prep(x2, nw, w_in_l, z_rwkv, row(w0[l]), w2p, row(a0[l]), a2p, row(k_k[l]), row(k_a[l]),
                             row(r_k[l]), ROW_BLOCK, ATT_COL_BLOCK, PREP_TOKENS)
    y_rwkv = _rwkv(prep, z_rwkv, row(lnx_w[l]), row(lnx_b[l]))
    out = _swa_out(z_att, bucket, sinks[l].astype(F32), rel_bias.astype(F32), qnw, knw,
                   x2, y_rwkv, w_out_b, ROW_BLOCK)
    return out.reshape(b, s, d)
```

```python
import math

import jax
import jax.numpy as jnp
import numpy as np
from jax import lax
from jax.experimental import pallas as pl
from jax.experimental.pallas import tpu as pltpu

F32 = jnp.float32
BF16 = jnp.bfloat16

D_MODEL = 2048
SEQ = 8192
HEAD_DIM = 64
D_RWKV = 1024
D_ATT = 1024
LORA = 64
N_Q_HEADS = 16
N_KV_HEADS = 2
D_KV = N_KV_HEADS * HEAD_DIM
WINDOW = 128
BLOCK = 128
N_BUCKETS = 32
MAX_EXACT = N_BUCKETS // 2
MAX_DISTANCE = 128
NORM_EPS = 1e-6
LNX_EPS = 64e-5
RWKV_COLS = 4 * D_RWKV + 2 * LORA
ATT_COLS = 2 * D_ATT + 2 * D_KV

LANES = 128
MXU_WIDTH = 256
N_PAIRS = D_RWKV // LANES
CHUNK = 64
RWKV_GROUP_CHUNKS = (2, 2, 2, 2)
RWKV_CHUNKS_PER_STEP = sum(RWKV_GROUP_CHUNKS)
NEG_BIG = -1e30
LOG2E = math.log2(math.e)

ROW_BLOCK = 512
RWKV_COL_BLOCK = RWKV_COLS // 3
ATT_COL_BLOCK = ATT_COLS // 2
PREP_TOKENS = SEQ // ((ATT_COLS // ATT_COL_BLOCK) * (SEQ // ROW_BLOCK))
VMEM_LIMIT = 56 * 1024 * 1024


def _dot(a, b):
    return jnp.dot(a, b, preferred_element_type=F32)


def _dot_nt(a, b):
    return lax.dot_general(a, b, (((1,), (1,)), ((), ())), preferred_element_type=F32)


def _split3(x):
    hi = x.astype(BF16)
    r1 = x - hi.astype(F32)
    mid = r1.astype(BF16)
    lo = (r1 - mid.astype(F32)).astype(BF16)
    return hi, mid, lo


def _seg_sum(x, seg_ones):
    hi = x.astype(BF16)
    lo = (x - hi.astype(F32)).astype(BF16)
    return _dot(hi, seg_ones) + _dot(lo, seg_ones)


def _dot_exact_lhs(m, x):
    hi, mid, lo = _split3(x)
    return _dot(m, hi) + _dot(m, mid) + _dot(m, lo)


def _sigmoid(x):
    return 1.0 / (1.0 + jnp.exp(-x))


def _proj_in_kernel(x_ref, nw_ref, w_ref, mu_ref, o_ref, wb_ref, prev_ref):
    @pl.when(pl.program_id(1) == 0)
    def _():
        wb_ref[...] = w_ref[...].astype(BF16)
        prev_ref[...] = jnp.zeros_like(prev_ref)

    tm, tn = o_ref.shape
    x = x_ref[...]
    h = (x * nw_ref[...]).astype(BF16)
    rs = lax.rsqrt(jnp.mean(x * x, axis=-1, keepdims=True) + NORM_EPS)
    for c0 in range(0, tn, MXU_WIDTH):
        cs = slice(c0, min(c0 + MXU_WIDTH, tn))
        z = _dot(h, wb_ref[:, cs]) * rs
        row = lax.broadcasted_iota(jnp.int32, z.shape, 0)
        zprev = jnp.where(row == 0, prev_ref[:, cs], pltpu.roll(z, 1, axis=0))
        prev_ref[:, cs] = z[tm - 1:tm, :]
        o_ref[:, cs] = z + (zprev - z) * mu_ref[:, cs]


def _proj_in(x2, norm_w, w, n, mu, tm, tn):
    s, d = x2.shape
    return pl.pallas_call(
        _proj_in_kernel,
        grid=(n // tn, s // tm),
        in_specs=[
            pl.BlockSpec((tm, d), lambda j, i: (i, 0)),
            pl.BlockSpec((1, d), lambda j, i: (0, 0)),
            pl.BlockSpec((d, tn), lambda j, i: (0, j)),
            pl.BlockSpec((1, tn), lambda j, i: (0, j)),
        ],
        out_specs=pl.BlockSpec((tm, tn), lambda j, i: (i, j)),
        out_shape=jax.ShapeDtypeStruct((s, n), F32),
        scratch_shapes=[pltpu.VMEM((d, tn), BF16), pltpu.VMEM((1, tn), F32)],
        compiler_params=pltpu.CompilerParams(
            dimension_semantics=("arbitrary", "arbitrary"), vmem_limit_bytes=VMEM_LIMIT),
        name="proj_in_rwkv",
    )(x2, norm_w, w, mu)


def _att_prep_kernel(x_ref, nw_ref, wa_ref, wb_in_ref, wc_ref, r_ref, k_ref, v_ref, lora_ref,
                     w0_ref, w2_ref, a0_ref, a2_ref, kk_ref, ka_ref, rk_ref, wo_in_ref,
                     o_ref, rt_ref, kt_ref, at_ref, bt_ref, bonus_ref, el_ref, wo_ref, wb_ref):
    c = CHUNK
    wo_ref[...] = wo_in_ref[...].astype(BF16)

    @pl.when(pl.program_id(1) == 0)
    def _():
        wp = wa_ref.shape[1]
        for n, piece in enumerate((wa_ref, wb_in_ref, wc_ref)):
            wb_ref[:, n * wp:(n + 1) * wp] = piece[...].astype(BF16)

    tm, tn = o_ref.shape
    x = x_ref[...]
    h = (x * nw_ref[...]).astype(BF16)
    rs = lax.rsqrt(jnp.mean(x * x, axis=-1, keepdims=True) + NORM_EPS)

    def project():
        for c0 in range(0, tn, MXU_WIDTH):
            cs = slice(c0, min(c0 + MXU_WIDTH, tn))
            o_ref[:, cs] = _dot(h, wb_ref[:, cs]) * rs
            yield

    r128 = lax.broadcasted_iota(jnp.int32, (LANES, LANES), 0)
    c128 = lax.broadcasted_iota(jnp.int32, (LANES, LANES), 1)
    seg_ones = jnp.where((r128 // HEAD_DIM) == (c128 // HEAD_DIM), 1.0, 0.0).astype(BF16)
    ti = lax.broadcasted_iota(jnp.int32, (c, c), 0)
    si = lax.broadcasted_iota(jnp.int32, (c, c), 1)
    tril_ones = jnp.where(si <= ti, 1.0, 0.0).astype(BF16)
    pairs = range(N_PAIRS)
    psl = [slice(p * LANES, (p + 1) * LANES) for p in pairs]

    def seg_sum_pairs(y):
        z = _seg_sum(jnp.concatenate([y[:, s] for s in psl], axis=0), seg_ones)
        return jnp.concatenate([z[p * c:(p + 1) * c] for p in pairs], axis=1)

    def prepare():
        lora_in = lora_ref[...]
        u_lin = _dot(jnp.tanh(lora_in).astype(BF16), w2_ref[...])
        a_lin = _dot(lora_in.astype(BF16), a2_ref[...])
        yield
        for ci in range(r_ref.shape[0] // c):
            rows = slice(ci * c, (ci + 1) * c)
            logw = -math.exp(-0.5) * _sigmoid(w0_ref[...] + u_lin[rows, :])
            av = _sigmoid(a0_ref[...] + a_lin[rows, :])
            logp = _dot_exact_lhs(tril_ones, logw)
            e_p = jnp.exp(logp)
            e_n = 1.0 / e_p
            el_ref[0, ci:ci + 1, :] = e_p[c - 1:c, :]
            yield
            r = r_ref[rows, :]
            k = k_ref[rows, :]
            v = v_ref[rows, :]
            kmod = k * (1.0 + (av - 1.0) * ka_ref[...])
            rt_ref[rows, :] = (r * e_p).astype(rt_ref.dtype)
            kt_ref[rows, :] = (kmod * e_n).astype(kt_ref.dtype)
            bonus_ref[rows, :] = seg_sum_pairs(r * kmod * rk_ref[...]) * v
            yield
            kk = k * kk_ref[...]
            kkn = kk * lax.rsqrt(jnp.maximum(seg_sum_pairs(kk * kk), 1e-24))
            at_ref[rows, :] = (-kkn * jnp.exp(logp - logw)).astype(at_ref.dtype)
            bt_ref[rows, :] = (kkn * av * e_n).astype(bt_ref.dtype)
            yield

    proj = project()
    n_slices = 1 + 3 * (r_ref.shape[0] // c)
    every = max(1, n_slices // (-(-tn // MXU_WIDTH)))
    for n, _ in enumerate(prepare()):
        if n % every == every - 1:
            next(proj, None)
    for _ in proj:
        pass


def _att_prep(x2, norm_w, w_in, z_rwkv, w0, w2p, a0, a2p, k_k, k_a, r_k, w_out, tm, tn, tp):
    s, d = x2.shape
    n = ATT_COLS
    nj, ni = n // tn, s // tm
    wp = tn // 3
    assert nj * ni * tp == s and tp % CHUNK == 0 and RWKV_COLS % wp == 0 and wp % LANES == 0
    piece = lambda p: pl.BlockSpec((d, wp), lambda j, i: (0, RWKV_COLS // wp + 3 * j + p))
    blk = lambda j, i: j * ni + i
    row = pl.BlockSpec((1, D_RWKV), lambda j, i: (0, 0))
    lora_w = pl.BlockSpec((2 * LORA, D_RWKV), lambda j, i: (0, 0))
    zcol = lambda cb: pl.BlockSpec((tp, D_RWKV), lambda j, i: (blk(j, i), cb))
    tok = pl.BlockSpec((tp, D_RWKV), lambda j, i: (blk(j, i), 0))
    sds = lambda dt: jax.ShapeDtypeStruct((s, D_RWKV), dt)
    wo_rows = w_out.shape[0] // (nj * ni)
    assert wo_rows * nj * ni == w_out.shape[0] and wo_rows % 16 == 0
    wo_slab = pl.BlockSpec((wo_rows, w_out.shape[1]), lambda j, i: (blk(j, i), 0))
    return pl.pallas_call(
        _att_prep_kernel,
        grid=(nj, ni),
        in_specs=[
            pl.BlockSpec((tm, d), lambda j, i: (i, 0)),
            pl.BlockSpec((1, d), lambda j, i: (0, 0)),
            piece(0), piece(1), piece(2),
            zcol(0), zcol(1), zcol(2),
            pl.BlockSpec((tp, 2 * LORA), lambda j, i: (blk(j, i), 4 * D_RWKV // (2 * LORA))),
            row, lora_w, row, lora_w, row, row, row,
            wo_slab,
        ],
        out_specs=[
            pl.BlockSpec((tm, tn), lambda j, i: (i, j)),
            tok, tok, tok, tok, tok,
            pl.BlockSpec((1, tp // CHUNK, D_RWKV), lambda j, i: (blk(j, i), 0, 0)),
            wo_slab,
        ],
        out_shape=[
            jax.ShapeDtypeStruct((s, n), F32),
            sds(BF16), sds(BF16), sds(BF16), sds(BF16), sds(F32),
            jax.ShapeDtypeStruct((s // tp, tp // CHUNK, D_RWKV), F32),
            jax.ShapeDtypeStruct(w_out.shape, BF16),
        ],
        scratch_shapes=[pltpu.VMEM((d, tn), BF16)],
        compiler_params=pltpu.CompilerParams(
            dimension_semantics=("arbitrary", "arbitrary"), vmem_limit_bytes=VMEM_LIMIT),
        name="proj_att_rwkv_prep",
    )(x2, norm_w, w_in, w_in, w_in, z_rwkv, z_rwkv, z_rwkv, z_rwkv, w0, w2p, a0, a2p, k_k, k_a, r_k, w_out)


def _rwkv_kernel(rt_ref, kt_ref, at_ref, bt_ref, bonus_ref, el_ref, v_ref, g_ref,
                 lw_ref, lb_ref, o_ref, gt_ref):
    c = CHUNK
    nch = RWKV_CHUNKS_PER_STEP
    i = pl.program_id(0)

    @pl.when(i == 0)
    def _():
        gt_ref[...] = jnp.zeros_like(gt_ref)

    lane = lax.broadcasted_iota(jnp.int32, (c, LANES), 1)
    head0 = lane < HEAD_DIM
    trow = lax.broadcasted_iota(jnp.int32, (c, LANES), 0)
    eye_cat = jnp.where((lane % HEAD_DIM) == trow, 1.0, 0.0)
    t2 = lax.broadcasted_iota(jnp.int32, (c, 2 * LANES), 0)
    s2 = lax.broadcasted_iota(jnp.int32, (c, 2 * LANES), 1) % HEAD_DIM
    strict2 = s2 < t2
    incl2 = s2 <= t2
    r128 = lax.broadcasted_iota(jnp.int32, (LANES, LANES), 0)
    c128 = lax.broadcasted_iota(jnp.int32, (LANES, LANES), 1)
    same_head = (r128 // HEAD_DIM) == (c128 // HEAD_DIM)
    zeros_c = jnp.zeros((c, LANES), F32)

    pairs = range(N_PAIRS)
    psl = [slice(p * LANES, (p + 1) * LANES) for p in pairs]

    def stack2(y):
        return jnp.concatenate([jnp.where(head0, y, 0.0), jnp.where(head0, 0.0, y)], axis=0)

    def catmul(a_cat, y):
        return _dot(a_cat.astype(BF16), stack2(y).astype(BF16))

    seg_mean = jnp.where(same_head, 1.0 / HEAD_DIM, 0.0).astype(BF16)

    def seg_sum_pairs(x, seg):
        y = _seg_sum(jnp.concatenate([x[:, s] for s in psl], axis=0), seg)
        return jnp.concatenate([y[p * c:(p + 1) * c] for p in pairs], axis=1)

    el_chunks = el_ref.shape[1]

    def intra(chunks, out):
        per_item = lambda ref: [ref[ci * c:(ci + 1) * c, s].astype(F32) for ci in chunks for s in psl]
        a_t, b_t, r_t, k_t, v_i = (per_item(ref) for ref in (at_ref, bt_ref, rt_ref, kt_ref, v_ref))
        e_last = [el_ref[ci // el_chunks, ci % el_chunks:ci % el_chunks + 1, s]
                  for ci in chunks for s in psl]
        sc = [_dot_nt(jnp.concatenate([a, r], axis=0).astype(BF16),
                      jnp.concatenate([stack2(b), stack2(k)], axis=0).astype(BF16))
              for a, r, b, k in zip(a_t, r_t, b_t, k_t)]
        yield
        l_all = [jnp.where(strict2, x[0:c, :], 0.0) for x in sc]
        a_rbk = [jnp.where(incl2, x[c:, :], 0.0) for x in sc]
        x = [l[:, 0:LANES] for l in l_all]
        tinv = [eye_cat + xi for xi in x]
        x = [catmul(xi, xi) for xi in x]
        yield
        for _ in range(4):
            tx = [catmul(jnp.concatenate([ti_, xi], axis=0), xi) for ti_, xi in zip(tinv, x)]
            tinv = [ti_ + y[0:c] for ti_, y in zip(tinv, tx)]
            x = [y[c:] for y in tx]
            yield
        tinv = [ti_ + catmul(ti_, xi) for ti_, xi in zip(tinv, x)]
        yield
        lav = [catmul(jnp.concatenate([l[:, LANES:], a[:, LANES:]], axis=0), v)
               for l, a, v in zip(l_all, a_rbk, v_i)]
        lakv = [y[0:c] for y in lav]
        arkv = [y[c:] for y in lav]
        yield
        aw = [_dot(ti_.astype(BF16), jnp.concatenate([stack2(a), stack2(y)], axis=1).astype(BF16))
              for ti_, a, y in zip(tinv, a_t, lakv)]
        a_eff = [y[:, 0:LANES] for y in aw]
        w_loc = [y[:, LANES:] for y in aw]
        yield
        qo = [_dot(a[:, 0:LANES].astype(BF16), jnp.concatenate([stack2(ae), stack2(w)], axis=1).astype(BF16))
              for a, ae, w in zip(a_rbk, a_eff, w_loc)]
        q_eff = [r + y[:, 0:LANES] for r, y in zip(r_t, qo)]
        o_loc = [y[:, LANES:] + av for y, av in zip(qo, arkv)]
        yield
        bh = [b * el for b, el in zip(b_t, e_last)]
        kh = [k * el for k, el in zip(k_t, e_last)]
        m_mat = [_dot(jnp.concatenate([ae, zeros_c], axis=0).T.astype(BF16),
                      jnp.concatenate([b, zeros_c], axis=0).astype(BF16)) for ae, b in zip(a_eff, bh)]
        yield
        n_mat = [_dot(jnp.concatenate([w, v], axis=0).T.astype(BF16),
                      jnp.concatenate([b, k], axis=0).astype(BF16)) for w, v, b, k in zip(w_loc, v_i, bh, kh)]
        out.update(q_eff=q_eff, o_loc=o_loc, m_mat=m_mat, n_mat=n_mat, e_last=e_last)

    def advance(gts, ic, j):
        sel = lambda name: ic[name][j * N_PAIRS:(j + 1) * N_PAIRS]
        gtb = [g.astype(BF16) for g in gts]
        o = [_dot_nt(q.astype(BF16), g) + ol for q, g, ol in zip(sel("q_eff"), gtb, sel("o_loc"))]
        gts = [jnp.where(same_head, g * el + _dot(gb, m.astype(BF16)) + n, 0.0)
               for g, gb, el, m, n in zip(gts, gtb, sel("e_last"), sel("m_mat"), sel("n_mat"))]
        return gts, jnp.concatenate(o, axis=1)

    def finish(ci, o):
        rs = slice(ci * c, (ci + 1) * c)
        d = o - seg_sum_pairs(o, seg_mean)
        var = seg_sum_pairs(d * d, seg_mean)
        on = d * lax.rsqrt(var + LNX_EPS) * lw_ref[...] + lb_ref[...]
        g = g_ref[rs, :]
        o_ref[rs, :] = ((on + bonus_ref[rs, :]) * (g * _sigmoid(g))).astype(o_ref.dtype)

    bounds = np.cumsum((0,) + RWKV_GROUP_CHUNKS)
    assert bounds[-1] == nch
    groups = [range(bounds[g], bounds[g + 1]) for g in range(len(RWKV_GROUP_CHUNKS))]
    ngroups = len(groups)
    state = dict(gts=[gt_ref[p] for p in pairs])

    def tail_group(g, ic):
        for j, ci in enumerate(groups[g]):
            state["gts"], o = advance(state["gts"], ic, j)
            yield
            finish(ci, o)
            yield

    def drive(main, sides):
        for _ in main:
            for s_ in sides:
                next(s_, None)
        for s_ in sides:
            for _ in s_:
                pass

    results = {}
    for g in range(ngroups):
        results[g] = {}
        sides = [tail_group(g - 1, results[g - 1])] if g >= 1 else []
        drive(intra(groups[g], results[g]), sides)
    drive(tail_group(ngroups - 1, results[ngroups - 1]), [])
    for p in pairs:
        gt_ref[p] = state["gts"][p]


def _rwkv(prep, z_rwkv, lnx_w, lnx_b):
    s = z_rwkv.shape[0]
    t = CHUNK * RWKV_CHUNKS_PER_STEP
    el = prep[-1]
    blocks_per_step = RWKV_CHUNKS_PER_STEP // el.shape[1]
    tok = pl.BlockSpec((t, D_RWKV), lambda i: (i, 0))
    row = pl.BlockSpec((1, D_RWKV), lambda i: (0, 0))
    return pl.pallas_call(
        _rwkv_kernel,
        grid=(s // t,),
        in_specs=[
            tok, tok, tok, tok, tok,
            pl.BlockSpec((blocks_per_step, el.shape[1], D_RWKV), lambda i: (i, 0, 0)),
            pl.BlockSpec((t, D_RWKV), lambda i: (i, 2)),
            pl.BlockSpec((t, D_RWKV), lambda i: (i, 3)),
            row, row,
        ],
        out_specs=tok,
        out_shape=jax.ShapeDtypeStruct((s, D_RWKV), BF16),
        scratch_shapes=[
            pltpu.VMEM((N_PAIRS, LANES, LANES), F32),
        ],
        compiler_params=pltpu.CompilerParams(
            dimension_semantics=("arbitrary",), vmem_limit_bytes=VMEM_LIMIT),
        name="rwkv7",
    )(*prep, z_rwkv, z_rwkv, lnx_w, lnx_b)


def _swa_out_kernel(sinks_ref, relb_ref, bucket_ref, q_ref, k_ref, v_ref, g0_ref, g1_ref, g2_ref, g3_ref,
                    qnw_ref, knw_ref, x_ref, yr_ref, wr_ref, wa_ref,
                    o_ref, bias_ref, kprev_ref, vprev_ref, yatt_ref):
    g_refs = (g0_ref, g1_ref, g2_ref, g3_ref)
    i = pl.program_id(0)
    bq = BLOCK
    tm = q_ref.shape[0]

    @pl.when(i == 0)
    def _():
        kprev_ref[...] = jnp.zeros_like(kprev_ref)
        vprev_ref[...] = jnp.zeros_like(vprev_ref)
        yatt_ref[...] = jnp.zeros_like(yatt_ref)
        bucket = bucket_ref[...]
        qi = lax.broadcasted_iota(jnp.int32, (bq, 2 * bq), 0)
        kj = lax.broadcasted_iota(jnp.int32, (bq, 2 * bq), 1)
        dist = bq + qi - kj
        inwin = (dist >= 0) & (dist < WINDOW)
        for h in range(N_Q_HEADS):
            acc = jnp.zeros((bq, 2 * bq), F32)
            for b in range(N_BUCKETS):
                acc = jnp.where(bucket == b, relb_ref[b, h], acc)
            acc = jnp.where(inwin, acc * LOG2E, NEG_BIG)
            bias_ref[0, h] = acc
            bias_ref[1, h] = jnp.where(kj >= bq, acc, NEG_BIG)

    first_step = jnp.where(i == 0, 1, 0)
    lane = lax.broadcasted_iota(jnp.int32, (bq, LANES), 1)
    head0 = lane < HEAD_DIM
    lane2 = lax.broadcasted_iota(jnp.int32, (2 * bq, LANES), 1)
    head0_2 = lane2 < HEAD_DIM
    r128 = lax.broadcasted_iota(jnp.int32, (LANES, LANES), 0)
    c128 = lax.broadcasted_iota(jnp.int32, (LANES, LANES), 1)
    seg_mean = jnp.where((r128 // HEAD_DIM) == (c128 // HEAD_DIM), 1.0 / HEAD_DIM, 0.0).astype(BF16)
    sink2 = [sinks_ref[h] * LOG2E for h in range(N_Q_HEADS)]
    pairs = range(N_PAIRS)
    heads = range(N_Q_HEADS)
    psl = [slice(p * LANES, (p + 1) * LANES) for p in pairs]
    kvh = [(2 * p) // (N_Q_HEADS // N_KV_HEADS) for p in pairs]

    def rms_heads(t, w):
        return t * lax.rsqrt(_seg_sum(t * t, seg_mean) + NORM_EPS) * w

    ya_prev = yatt_ref[...]
    yr_prev = yr_ref[...]

    def project_previous():
        for c0 in range(0, D_MODEL, MXU_WIDTH):
            cs = slice(c0, c0 + MXU_WIDTH)
            o_ref[:, cs] = x_ref[:, cs] + _dot(yr_prev, wr_ref[:, cs]) + _dot(ya_prev, wa_ref[:, cs])
            yield

    def attend():
        kp, vp = kprev_ref[...], vprev_ref[...]
        for b in range(tm // bq):
            rb = slice(b * bq, (b + 1) * bq)
            first = first_step if b == 0 else 0
            kn = rms_heads(k_ref[rb, :], knw_ref[...])
            vc = v_ref[rb, :]
            kcat = jnp.concatenate([kp, kn], axis=0)
            vcat = jnp.concatenate([vp, vc], axis=0)
            kp, vp = kn, vc
            krol = pltpu.roll(kcat, HEAD_DIM, axis=1)
            vrol = pltpu.roll(vcat, HEAD_DIM, axis=1)
            kdup = [jnp.where(head0_2, kcat, krol).astype(BF16), jnp.where(head0_2, krol, kcat).astype(BF16)]
            vdup = [jnp.where(head0_2, vcat, vrol).astype(BF16), jnp.where(head0_2, vrol, vcat).astype(BF16)]
            yield
            qn = [rms_heads(q_ref[rb, s], qnw_ref[:, s]) for s in psl]
            yield
            q2 = [jnp.concatenate([jnp.where(head0, q, 0.0), jnp.where(head0, 0.0, q)], axis=0).astype(BF16)
                  for q in qn]
            gh = N_Q_HEADS // N_KV_HEADS
            lgk = [_dot_nt(jnp.concatenate([q for q, kv in zip(q2, kvh) if kv == g], axis=0), kdup[g])
                   for g in range(N_KV_HEADS)]
            yield
            lg = [lgk[h // gh][(h % gh) * bq:(h % gh + 1) * bq, :] + bias_ref[first, h] for h in heads]
            yield
            m = [jnp.maximum(jnp.max(l, axis=-1, keepdims=True), sink2[h]) for h, l in zip(heads, lg)]
            yield
            e = [jnp.exp2(l - mm) for l, mm in zip(lg, m)]
            yield
            denom = [jnp.sum(ee, axis=-1, keepdims=True) + jnp.exp2(sink2[h] - mm)
                     for h, ee, mm in zip(heads, e, m)]
            yield
            pvk = [_dot(jnp.concatenate([ee.astype(BF16) for ee in e[g * gh:(g + 1) * gh]], axis=0), vdup[g])
                   for g in range(N_KV_HEADS)]
            pv = [pvk[h // gh][(h % gh) * bq:(h % gh + 1) * bq, :] for h in heads]
            yield
            outs = [x / d for x, d in zip(pv, denom)]
            for p in pairs:
                o = jnp.where(head0, outs[2 * p], outs[2 * p + 1])
                g = g_refs[p // 2][rb, (p % 2) * LANES:(p % 2 + 1) * LANES]
                yatt_ref[rb, psl[p]] = (o * (g * _sigmoid(g))).astype(yatt_ref.dtype)
            yield
        kprev_ref[...] = kp
        vprev_ref[...] = vp

    proj = project_previous()
    n_levels = 9 * (tm // bq)
    every = n_levels // (D_MODEL // MXU_WIDTH)
    for n, _ in enumerate(attend()):
        if n % every == every - 1:
            next(proj, None)
    for _ in proj:
        pass


def _swa_out(z_att, bucket, sinks, rel_bias, qnw, knw, x2, y_rwkv, w_out, tm):
    s, d = x2.shape
    bq = BLOCK
    nb = s // tm
    smem = pl.BlockSpec(memory_space=pltpu.SMEM)
    cur = lambda i: jnp.minimum(i, nb - 1)
    prev = lambda i: jnp.maximum(i - 1, 0)
    kv_col = D_ATT // D_KV
    gw = 2 * D_KV
    gate = lambda n: pl.BlockSpec((tm, gw), lambda i: (cur(i), (D_ATT + 2 * D_KV) // gw + n))
    return pl.pallas_call(
        _swa_out_kernel,
        grid=(nb + 1,),
        in_specs=[
            smem, smem,
            pl.BlockSpec((bq, 2 * bq), lambda i: (0, 0)),
            pl.BlockSpec((tm, D_ATT), lambda i: (cur(i), 0)),
            pl.BlockSpec((tm, D_KV), lambda i: (cur(i), kv_col)),
            pl.BlockSpec((tm, D_KV), lambda i: (cur(i), kv_col + 1)),
            gate(0), gate(1), gate(2), gate(3),
            pl.BlockSpec((1, D_ATT), lambda i: (0, 0)),
            pl.BlockSpec((1, D_KV), lambda i: (0, 0)),
            pl.BlockSpec((tm, d), lambda i: (prev(i), 0)),
            pl.BlockSpec((tm, D_RWKV), lambda i: (prev(i), 0)),
            pl.BlockSpec((D_RWKV, d), lambda i: (0, 0)),
            pl.BlockSpec((D_ATT, d), lambda i: (D_RWKV // D_ATT, 0)),
        ],
        out_specs=pl.BlockSpec((tm, d), lambda i: (prev(i), 0)),
        out_shape=jax.ShapeDtypeStruct((s, d), F32),
        scratch_shapes=[
            pltpu.VMEM((2, N_Q_HEADS, bq, 2 * bq), F32),
            pltpu.VMEM((bq, D_KV), F32),
            pltpu.VMEM((bq, D_KV), F32),
            pltpu.VMEM((tm, D_ATT), BF16),
        ],
        compiler_params=pltpu.CompilerParams(
            dimension_semantics=("arbitrary",), vmem_limit_bytes=VMEM_LIMIT),
        name="swa_out",
    )(sinks, rel_bias, bucket, z_att, z_att, z_att, z_att, z_att, z_att, z_att, qnw, knw, x2, y_rwkv, w_out, w_out)


def _t5_bucket_table():
    dist = BLOCK + np.arange(BLOCK)[:, None] - np.arange(2 * BLOCK)[None, :]
    n = np.maximum(dist, 0)
    nf = np.maximum(n, 1).astype(np.float64)
    large = MAX_EXACT + (np.log(nf / MAX_EXACT) / math.log(MAX_DISTANCE / MAX_EXACT)
                         * (N_BUCKETS - MAX_EXACT)).astype(np.int32)
    large = np.minimum(large, N_BUCKETS - 1)
    return np.where(n < MAX_EXACT, n, large).astype(np.int32)


def kernel(x, norm_w, w_in, w_out, mu_rwkv, w0, w2, a0, a2, k_k, k_a, r_k, lnx_w, lnx_b,
           q_norm_w, k_norm_w, sinks, rel_bias):
    b, s, d = x.shape
    assert (b, s, d) == (1, SEQ, D_MODEL) and norm_w.shape[0] == 1
    x2 = x.reshape(s, d)
    l = 0
    row = lambda t: t.reshape(1, -1).astype(F32)

    w_in_l = w_in[l]
    zeros_l = jnp.zeros((LORA, D_RWKV), F32)
    w2p = jnp.concatenate([w2[l], zeros_l], axis=0).astype(BF16)
    a2p = jnp.concatenate([zeros_l, a2[l]], axis=0).astype(BF16)
    qnw = jnp.tile(q_norm_w[l] * (HEAD_DIM ** -0.5 * LOG2E), N_Q_HEADS).reshape(1, D_ATT)
    knw = jnp.tile(k_norm_w[l], N_KV_HEADS).reshape(1, D_KV)
    bucket = jnp.asarray(_t5_bucket_table())

    nw = row(norm_w[l])
    z_rwkv = _proj_in(x2, nw, w_in_l, RWKV_COLS, row(mu_rwkv[l]), ROW_BLOCK, RWKV_COL_BLOCK)
    z_att, *prep, w_out_b = _att_prep(x2, nw, w_in_l, z_rwkv, row(w0[l]), w2p, row(a0[l]), a2p, row(k_k[l]),
                                      row(k_a[l]), row(r_k[l]), w_out[l], ROW_BLOCK, ATT_COL_BLOCK, PREP_TOKENS)
    y_rwkv = _rwkv(prep, z_rwkv, row(lnx_w[l]), row(lnx_b[l]))
    out = _swa_out(z_att, bucket, sinks[l].astype(F32), rel_bias.astype(F32), qnw, knw,
                   x2, y_rwkv, w_out_b, ROW_BLOCK)
    return out.reshape(b, s, d)
```

```python
import math

import jax
import jax.numpy as jnp
import numpy as np
from jax import lax
from jax.experimental import pallas as pl
from jax.experimental.pallas import tpu as pltpu

F32 = jnp.float32
BF16 = jnp.bfloat16

D_MODEL = 2048
SEQ = 8192
HEAD_DIM = 64
D_RWKV = 1024
D_ATT = 1024
LORA = 64
N_Q_HEADS = 16
N_KV_HEADS = 2
D_KV = N_KV_HEADS * HEAD_DIM
WINDOW = 128
BLOCK = 128
N_BUCKETS = 32
MAX_EXACT = N_BUCKETS // 2
MAX_DISTANCE = 128
NORM_EPS = 1e-6
LNX_EPS = 64e-5
RWKV_COLS = 4 * D_RWKV + 2 * LORA
ATT_COLS = 2 * D_ATT + 2 * D_KV

LANES = 128
MXU_WIDTH = 256
N_PAIRS = D_RWKV // LANES
CHUNK = 64
RWKV_GROUP_CHUNKS = (2, 2, 2, 2)
RWKV_CHUNKS_PER_STEP = sum(RWKV_GROUP_CHUNKS)
NEG_BIG = -1e30
LOG2E = math.log2(math.e)

ROW_BLOCK = 512
RWKV_COL_BLOCK = RWKV_COLS // 3
ATT_COL_BLOCK = ATT_COLS // 2
PREP_TOKENS = SEQ // ((ATT_COLS // ATT_COL_BLOCK) * (SEQ // ROW_BLOCK))
VMEM_LIMIT = 56 * 1024 * 1024


def _dot(a, b):
    return jnp.dot(a, b, preferred_element_type=F32)


def _dot_nt(a, b):
    return lax.dot_general(a, b, (((1,), (1,)), ((), ())), preferred_element_type=F32)


def _split3(x):
    hi = x.astype(BF16)
    r1 = x - hi.astype(F32)
    mid = r1.astype(BF16)
    lo = (r1 - mid.astype(F32)).astype(BF16)
    return hi, mid, lo


def _seg_sum(x, seg_ones):
    hi = x.astype(BF16)
    lo = (x - hi.astype(F32)).astype(BF16)
    return _dot(hi, seg_ones) + _dot(lo, seg_ones)


def _dot_exact_lhs(m, x):
    hi, mid, lo = _split3(x)
    return _dot(m, hi) + _dot(m, mid) + _dot(m, lo)


def _sigmoid(x):
    return 1.0 / (1.0 + jnp.exp(-x))


def _proj_in_kernel(x_ref, nw_ref, w_ref, mu_ref, o_ref, wb_ref, prev_ref):
    @pl.when(pl.program_id(1) == 0)
    def _():
        wb_ref[...] = w_ref[...].astype(BF16)
        prev_ref[...] = jnp.zeros_like(prev_ref)

    tm, tn = o_ref.shape
    x = x_ref[...]
    h = (x * nw_ref[...]).astype(BF16)
    rs = lax.rsqrt(jnp.mean(x * x, axis=-1, keepdims=True) + NORM_EPS)
    for c0 in range(0, tn, MXU_WIDTH):
        cs = slice(c0, min(c0 + MXU_WIDTH, tn))
        z = _dot(h, wb_ref[:, cs]) * rs
        row = lax.broadcasted_iota(jnp.int32, z.shape, 0)
        zprev = jnp.where(row == 0, prev_ref[:, cs], pltpu.roll(z, 1, axis=0))
        prev_ref[:, cs] = z[tm - 1:tm, :]
        o_ref[:, cs] = z + (zprev - z) * mu_ref[:, cs]


def _proj_in(x2, norm_w, w, n, mu, tm, tn):
    s, d = x2.shape
    return pl.pallas_call(
        _proj_in_kernel,
        grid=(n // tn, s // tm),
        in_specs=[
            pl.BlockSpec((tm, d), lambda j, i: (i, 0)),
            pl.BlockSpec((1, d), lambda j, i: (0, 0)),
            pl.BlockSpec((d, tn), lambda j, i: (0, j)),
            pl.BlockSpec((1, tn), lambda j, i: (0, j)),
        ],
        out_specs=pl.BlockSpec((tm, tn), lambda j, i: (i, j)),
        out_shape=jax.ShapeDtypeStruct((s, n), F32),
        scratch_shapes=[pltpu.VMEM((d, tn), BF16), pltpu.VMEM((1, tn), F32)],
        compiler_params=pltpu.CompilerParams(
            dimension_semantics=("arbitrary", "arbitrary"), vmem_limit_bytes=VMEM_LIMIT),
        name="proj_in_rwkv",
    )(x2, norm_w, w, mu)


def _att_prep_kernel(x_ref, nw_ref, wa_ref, wb_in_ref, wc_ref, r_ref, k_ref, v_ref, lora_ref,
                     w0_ref, w2_ref, a0_ref, a2_ref, kk_ref, ka_ref, rk_ref, wo_in_ref,
                     o_ref, rt_ref, kt_ref, at_ref, bt_ref, bonus_ref, el_ref, wo_ref, wb_ref):
    c = CHUNK
    wo_ref[...] = wo_in_ref[...].astype(BF16)

    @pl.when(pl.program_id(1) == 0)
    def _():
        wp = wa_ref.shape[1]
        for n, piece in enumerate((wa_ref, wb_in_ref, wc_ref)):
            wb_ref[:, n * wp:(n + 1) * wp] = piece[...].astype(BF16)

    tm, tn = o_ref.shape
    x = x_ref[...]
    h = (x * nw_ref[...]).astype(BF16)
    rs = lax.rsqrt(jnp.mean(x * x, axis=-1, keepdims=True) + NORM_EPS)

    def project():
        for c0 in range(0, tn, MXU_WIDTH):
            cs = slice(c0, min(c0 + MXU_WIDTH, tn))
            o_ref[:, cs] = _dot(h, wb_ref[:, cs]) * rs
            yield

    r128 = lax.broadcasted_iota(jnp.int32, (LANES, LANES), 0)
    c128 = lax.broadcasted_iota(jnp.int32, (LANES, LANES), 1)
    seg_ones = jnp.where((r128 // HEAD_DIM) == (c128 // HEAD_DIM), 1.0, 0.0).astype(BF16)
    ti = lax.broadcasted_iota(jnp.int32, (c, c), 0)
    si = lax.broadcasted_iota(jnp.int32, (c, c), 1)
    tril_ones = jnp.where(si <= ti, 1.0, 0.0).astype(BF16)
    pairs = range(N_PAIRS)
    psl = [slice(p * LANES, (p + 1) * LANES) for p in pairs]

    def seg_sum_pairs(y):
        z = _seg_sum(jnp.concatenate([y[:, s] for s in psl], axis=0), seg_ones)
        return jnp.concatenate([z[p * c:(p + 1) * c] for p in pairs], axis=1)

    def prepare():
        lora_in = lora_ref[...]
        u_lin = _dot(jnp.tanh(lora_in).astype(BF16), w2_ref[...])
        a_lin = _dot(lora_in.astype(BF16), a2_ref[...])
        yield
        for ci in range(r_ref.shape[0] // c):
            rows = slice(ci * c, (ci + 1) * c)
            logw = -math.exp(-0.5) * _sigmoid(w0_ref[...] + u_lin[rows, :])
            av = _sigmoid(a0_ref[...] + a_lin[rows, :])
            logp = _dot_exact_lhs(tril_ones, logw)
            e_p = jnp.exp(logp)
            e_n = 1.0 / e_p
            el_ref[0, ci:ci + 1, :] = e_p[c - 1:c, :]
            yield
            r = r_ref[rows, :]
            k = k_ref[rows, :]
            v = v_ref[rows, :]
            kmod = k * (1.0 + (av - 1.0) * ka_ref[...])
            rt_ref[rows, :] = (r * e_p).astype(rt_ref.dtype)
            kt_ref[rows, :] = (kmod * e_n).astype(kt_ref.dtype)
            bonus_ref[rows, :] = seg_sum_pairs(r * kmod * rk_ref[...]) * v
            yield
            kk = k * kk_ref[...]
            kkn = kk * lax.rsqrt(jnp.maximum(seg_sum_pairs(kk * kk), 1e-24))
            at_ref[rows, :] = (-kkn * jnp.exp(logp - logw)).astype(at_ref.dtype)
            bt_ref[rows, :] = (kkn * av * e_n).astype(bt_ref.dtype)
            yield

    proj = project()
    n_slices = 1 + 3 * (r_ref.shape[0] // c)
    every = max(1, n_slices // (-(-tn // MXU_WIDTH)))
    for n, _ in enumerate(prepare()):
        if n % every == every - 1:
            next(proj, None)
    for _ in proj:
        pass


def _att_prep(x2, norm_w, w_in, z_rwkv, w0, w2p, a0, a2p, k_k, k_a, r_k, w_out, tm, tn, tp):
    s, d = x2.shape
    n = ATT_COLS
    nj, ni = n // tn, s // tm
    wp = tn // 3
    assert nj * ni * tp == s and tp % CHUNK == 0 and RWKV_COLS % wp == 0 and wp % LANES == 0
    piece = lambda p: pl.BlockSpec((d, wp), lambda j, i: (0, RWKV_COLS // wp + 3 * j + p))
    blk = lambda j, i: j * ni + i
    row = pl.BlockSpec((1, D_RWKV), lambda j, i: (0, 0))
    lora_w = pl.BlockSpec((2 * LORA, D_RWKV), lambda j, i: (0, 0))
    zcol = lambda cb: pl.BlockSpec((tp, D_RWKV), lambda j, i: (blk(j, i), cb))
    tok = pl.BlockSpec((tp, D_RWKV), lambda j, i: (blk(j, i), 0))
    sds = lambda dt: jax.ShapeDtypeStruct((s, D_RWKV), dt)
    wo_rows = w_out.shape[0] // (nj * ni)
    assert wo_rows * nj * ni == w_out.shape[0] and wo_rows % 16 == 0
    wo_slab = pl.BlockSpec((wo_rows, w_out.shape[1]), lambda j, i: (blk(j, i), 0))
    return pl.pallas_call(
        _att_prep_kernel,
        grid=(nj, ni),
        in_specs=[
            pl.BlockSpec((tm, d), lambda j, i: (i, 0)),
            pl.BlockSpec((1, d), lambda j, i: (0, 0)),
            piece(0), piece(1), piece(2),
            zcol(0), zcol(1), zcol(2),
            pl.BlockSpec((tp, 2 * LORA), lambda j, i: (blk(j, i), 4 * D_RWKV // (2 * LORA))),
            row, lora_w, row, lora_w, row, row, row,
            wo_slab,
        ],
        out_specs=[
            pl.BlockSpec((tm, tn), lambda j, i: (i, j)),
            tok, tok, tok, tok, tok,
            pl.BlockSpec((1, tp // CHUNK, D_RWKV), lambda j, i: (blk(j, i), 0, 0)),
            wo_slab,
        ],
        out_shape=[
            jax.ShapeDtypeStruct((s, n), F32),
            sds(BF16), sds(BF16), sds(BF16), sds(BF16), sds(F32),
            jax.ShapeDtypeStruct((s // tp, tp // CHUNK, D_RWKV), F32),
            jax.ShapeDtypeStruct(w_out.shape, BF16),
        ],
        scratch_shapes=[pltpu.VMEM((d, tn), BF16)],
        compiler_params=pltpu.CompilerParams(
            dimension_semantics=("arbitrary", "arbitrary"), vmem_limit_bytes=VMEM_LIMIT),
        name="proj_att_rwkv_prep",
    )(x2, norm_w, w_in, w_in, w_in, z_rwkv, z_rwkv, z_rwkv, z_rwkv, w0, w2p, a0, a2p, k_k, k_a, r_k, w_out)


def _rwkv_kernel(rt_ref, kt_ref, at_ref, bt_ref, bonus_ref, el_ref, v_ref, g_ref,
                 lw_ref, lb_ref, o_ref, gt_ref):
    c = CHUNK
    nch = RWKV_CHUNKS_PER_STEP
    i = pl.program_id(0)

    @pl.when(i == 0)
    def _():
        gt_ref[...] = jnp.zeros_like(gt_ref)

    lane = lax.broadcasted_iota(jnp.int32, (c, LANES), 1)
    head0 = lane < HEAD_DIM
    trow = lax.broadcasted_iota(jnp.int32, (c, LANES), 0)
    eye_cat = jnp.where((lane % HEAD_DIM) == trow, 1.0, 0.0)
    t2 = lax.broadcasted_iota(jnp.int32, (c, 2 * LANES), 0)
    s2 = lax.broadcasted_iota(jnp.int32, (c, 2 * LANES), 1) % HEAD_DIM
    strict2 = s2 < t2
    incl2 = s2 <= t2
    r128 = lax.broadcasted_iota(jnp.int32, (LANES, LANES), 0)
    c128 = lax.broadcasted_iota(jnp.int32, (LANES, LANES), 1)
    same_head = (r128 // HEAD_DIM) == (c128 // HEAD_DIM)
    zeros_c = jnp.zeros((c, LANES), F32)

    pairs = range(N_PAIRS)
    psl = [slice(p * LANES, (p + 1) * LANES) for p in pairs]

    def stack2(y):
        return jnp.concatenate([jnp.where(head0, y, 0.0), jnp.where(head0, 0.0, y)], axis=0)

    def catmul(a_cat, y):
        return _dot(a_cat.astype(BF16), stack2(y).astype(BF16))

    seg_mean = jnp.where(same_head, 1.0 / HEAD_DIM, 0.0).astype(BF16)

    def seg_sum_pairs(x, seg):
        y = _seg_sum(jnp.concatenate([x[:, s] for s in psl], axis=0), seg)
        return jnp.concatenate([y[p * c:(p + 1) * c] for p in pairs], axis=1)

    el_chunks = el_ref.shape[1]

    def intra(chunks, out):
        per_item = lambda ref: [ref[ci * c:(ci + 1) * c, s].astype(F32) for ci in chunks for s in psl]
        a_t, b_t, r_t, k_t, v_i = (per_item(ref) for ref in (at_ref, bt_ref, rt_ref, kt_ref, v_ref))
        e_last = [el_ref[ci // el_chunks, ci % el_chunks:ci % el_chunks + 1, s]
                  for ci in chunks for s in psl]
        sc = [_dot_nt(jnp.concatenate([a, r], axis=0).astype(BF16),
                      jnp.concatenate([stack2(b), stack2(k)], axis=0).astype(BF16))
              for a, r, b, k in zip(a_t, r_t, b_t, k_t)]
        yield
        l_all = [jnp.where(strict2, x[0:c, :], 0.0) for x in sc]
        a_rbk = [jnp.where(incl2, x[c:, :], 0.0) for x in sc]
        x = [l[:, 0:LANES] for l in l_all]
        tinv = [eye_cat + xi for xi in x]
        x = [catmul(xi, xi) for xi in x]
        yield
        for _ in range(4):
            tx = [catmul(jnp.concatenate([ti_, xi], axis=0), xi) for ti_, xi in zip(tinv, x)]
            tinv = [ti_ + y[0:c] for ti_, y in zip(tinv, tx)]
            x = [y[c:] for y in tx]
            yield
        tinv = [ti_ + catmul(ti_, xi) for ti_, xi in zip(tinv, x)]
        yield
        lav = [catmul(jnp.concatenate([l[:, LANES:], a[:, LANES:]], axis=0), v)
               for l, a, v in zip(l_all, a_rbk, v_i)]
        lakv = [y[0:c] for y in lav]
        arkv = [y[c:] for y in lav]
        yield
        aw = [_dot(ti_.astype(BF16), jnp.concatenate([stack2(a), stack2(y)], axis=1).astype(BF16))
              for ti_, a, y in zip(tinv, a_t, lakv)]
        a_eff = [y[:, 0:LANES] for y in aw]
        w_loc = [y[:, LANES:] for y in aw]
        yield
        qo = [_dot(a[:, 0:LANES].astype(BF16), jnp.concatenate([stack2(ae), stack2(w)], axis=1).astype(BF16))
              for a, ae, w in zip(a_rbk, a_eff, w_loc)]
        q_eff = [r + y[:, 0:LANES] for r, y in zip(r_t, qo)]
        o_loc = [y[:, LANES:] + av for y, av in zip(qo, arkv)]
        yield
        bh = [b * el for b, el in zip(b_t, e_last)]
        kh = [k * el for k, el in zip(k_t, e_last)]
        m_mat = [_dot(jnp.concatenate([ae, zeros_c], axis=0).T.astype(BF16),
                      jnp.concatenate([b, zeros_c], axis=0).astype(BF16)) for ae, b in zip(a_eff, bh)]
        yield
        n_mat = [_dot(jnp.concatenate([w, v], axis=0).T.astype(BF16),
                      jnp.concatenate([b, k], axis=0).astype(BF16)) for w, v, b, k in zip(w_loc, v_i, bh, kh)]
        out.update(q_eff=q_eff, o_loc=o_loc, m_mat=m_mat, n_mat=n_mat, e_last=e_last)

    def advance(gts, ic, j):
        sel = lambda name: ic[name][j * N_PAIRS:(j + 1) * N_PAIRS]
        gtb = [g.astype(BF16) for g in gts]
        o = [_dot_nt(q.astype(BF16), g) + ol for q, g, ol in zip(sel("q_eff"), gtb, sel("o_loc"))]
        gts = [jnp.where(same_head, g * el + _dot(gb, m.astype(BF16)) + n, 0.0)
               for g, gb, el, m, n in zip(gts, gtb, sel("e_last"), sel("m_mat"), sel("n_mat"))]
        return gts, jnp.concatenate(o, axis=1)

    def finish(ci, o):
        rs = slice(ci * c, (ci + 1) * c)
        d = o - seg_sum_pairs(o, seg_mean)
        var = seg_sum_pairs(d * d, seg_mean)
        on = d * lax.rsqrt(var + LNX_EPS) * lw_ref[...] + lb_ref[...]
        g = g_ref[rs, :]
        o_ref[rs, :] = ((on + bonus_ref[rs, :]) * (g * _sigmoid(g))).astype(o_ref.dtype)

    bounds = np.cumsum((0,) + RWKV_GROUP_CHUNKS)
    assert bounds[-1] == nch
    groups = [range(bounds[g], bounds[g + 1]) for g in range(len(RWKV_GROUP_CHUNKS))]
    ngroups = len(groups)
    state = dict(gts=[gt_ref[p] for p in pairs])

    def tail_group(g, ic):
        for j, ci in enumerate(groups[g]):
            state["gts"], o = advance(state["gts"], ic, j)
            yield
            finish(ci, o)
            yield

    def drive(main, sides):
        for _ in main:
            for s_ in sides:
                next(s_, None)
        for s_ in sides:
            for _ in s_:
                pass

    results = {}
    for g in range(ngroups):
        results[g] = {}
        sides = [tail_group(g - 1, results[g - 1])] if g >= 1 else []
        drive(intra(groups[g], results[g]), sides)
    drive(tail_group(ngroups - 1, results[ngroups - 1]), [])
    for p in pairs:
        gt_ref[p] = state["gts"][p]


def _rwkv(prep, z_rwkv, lnx_w, lnx_b):
    s = z_rwkv.shape[0]
    t = CHUNK * RWKV_CHUNKS_PER_STEP
    el = prep[-1]
    blocks_per_step = RWKV_CHUNKS_PER_STEP // el.shape[1]
    tok = pl.BlockSpec((t, D_RWKV), lambda i: (i, 0))
    row = pl.BlockSpec((1, D_RWKV), lambda i: (0, 0))
    return pl.pallas_call(
        _rwkv_kernel,
        grid=(s // t,),
        in_specs=[
            tok, tok, tok, tok, tok,
            pl.BlockSpec((blocks_per_step, el.shape[1], D_RWKV), lambda i: (i, 0, 0)),
            pl.BlockSpec((t, D_RWKV), lambda i: (i, 2)),
            pl.BlockSpec((t, D_RWKV), lambda i: (i, 3)),
            row, row,
        ],
        out_specs=tok,
        out_shape=jax.ShapeDtypeStruct((s, D_RWKV), BF16),
        scratch_shapes=[
            pltpu.VMEM((N_PAIRS, LANES, LANES), F32),
        ],
        compiler_params=pltpu.CompilerParams(
            dimension_semantics=("arbitrary",), vmem_limit_bytes=VMEM_LIMIT),
        name="rwkv7",
    )(*prep, z_rwkv, z_rwkv, lnx_w, lnx_b)


def _swa_out_kernel(sinks_ref, relb_ref, bucket_ref, q_ref, k_ref, v_ref, g0_ref, g1_ref, g2_ref, g3_ref,
                    qnw_ref, knw_ref, x_ref, yr_ref, wr_ref, wa_ref,
                    o_ref, bias_ref, kprev_ref, vprev_ref, yatt_ref):
    g_refs = (g0_ref, g1_ref, g2_ref, g3_ref)
    i = pl.program_id(0)
    bq = BLOCK
    tm = q_ref.shape[0]

    @pl.when(i == 0)
    def _():
        kprev_ref[...] = jnp.zeros_like(kprev_ref)
        vprev_ref[...] = jnp.zeros_like(vprev_ref)
        yatt_ref[...] = jnp.zeros_like(yatt_ref)
        bucket = bucket_ref[...]
        qi = lax.broadcasted_iota(jnp.int32, (bq, 2 * bq), 0)
        kj = lax.broadcasted_iota(jnp.int32, (bq, 2 * bq), 1)
        dist = bq + qi - kj
        inwin = (dist >= 0) & (dist < WINDOW)
        for h in range(N_Q_HEADS):
            acc = jnp.zeros((bq, 2 * bq), F32)
            for b in range(N_BUCKETS):
                acc = jnp.where(bucket == b, relb_ref[b, h], acc)
            acc = jnp.where(inwin, acc * LOG2E, NEG_BIG)
            bias_ref[0, h] = acc
            bias_ref[1, h] = jnp.where(kj >= bq, acc, NEG_BIG)

    first_step = jnp.where(i == 0, 1, 0)
    lane = lax.broadcasted_iota(jnp.int32, (bq, LANES), 1)
    head0 = lane < HEAD_DIM
    lane2 = lax.broadcasted_iota(jnp.int32, (2 * bq, LANES), 1)
    head0_2 = lane2 < HEAD_DIM
    r128 = lax.broadcasted_iota(jnp.int32, (LANES, LANES), 0)
    c128 = lax.broadcasted_iota(jnp.int32, (LANES, LANES), 1)
    seg_mean = jnp.where((r128 // HEAD_DIM) == (c128 // HEAD_DIM), 1.0 / HEAD_DIM, 0.0).astype(BF16)
    sink2 = [sinks_ref[h] * LOG2E for h in range(N_Q_HEADS)]
    pairs = range(N_PAIRS)
    heads = range(N_Q_HEADS)
    psl = [slice(p * LANES, (p + 1) * LANES) for p in pairs]
    kvh = [(2 * p) // (N_Q_HEADS // N_KV_HEADS) for p in pairs]

    def rms_heads(t, w):
        return t * lax.rsqrt(_seg_sum(t * t, seg_mean) + NORM_EPS) * w

    def project_previous():
        ya_prev = yatt_ref[...]
        yr_prev = yr_ref[...]
        yield
        for c0 in range(0, D_MODEL, MXU_WIDTH):
            cs = slice(c0, c0 + MXU_WIDTH)
            o_ref[:, cs] = x_ref[:, cs] + _dot(yr_prev, wr_ref[:, cs]) + _dot(ya_prev, wa_ref[:, cs])
            yield

    def attend():
        kp, vp = kprev_ref[...], vprev_ref[...]
        for b in range(tm // bq):
            rb = slice(b * bq, (b + 1) * bq)
            first = first_step if b == 0 else 0
            kn = rms_heads(k_ref[rb, :], knw_ref[...])
            vc = v_ref[rb, :]
            kcat = jnp.concatenate([kp, kn], axis=0)
            vcat = jnp.concatenate([vp, vc], axis=0)
            kp, vp = kn, vc
            krol = pltpu.roll(kcat, HEAD_DIM, axis=1)
            vrol = pltpu.roll(vcat, HEAD_DIM, axis=1)
            kdup = [jnp.where(head0_2, kcat, krol).astype(BF16), jnp.where(head0_2, krol, kcat).astype(BF16)]
            vdup = [jnp.where(head0_2, vcat, vrol).astype(BF16), jnp.where(head0_2, vrol, vcat).astype(BF16)]
            yield
            qn = [rms_heads(q_ref[rb, s], qnw_ref[:, s]) for s in psl]
            yield
            q2 = [jnp.concatenate([jnp.where(head0, q, 0.0), jnp.where(head0, 0.0, q)], axis=0).astype(BF16)
                  for q in qn]
            gh = N_Q_HEADS // N_KV_HEADS
            lgk = [_dot_nt(jnp.concatenate([q for q, kv in zip(q2, kvh) if kv == g], axis=0), kdup[g])
                   for g in range(N_KV_HEADS)]
            yield
            lg = [lgk[h // gh][(h % gh) * bq:(h % gh + 1) * bq, :] + bias_ref[first, h] for h in heads]
            yield
            m = [jnp.maximum(jnp.max(l, axis=-1, keepdims=True), sink2[h]) for h, l in zip(heads, lg)]
            yield
            e = [jnp.exp2(l - mm) for l, mm in zip(lg, m)]
            yield
            denom = [jnp.sum(ee, axis=-1, keepdims=True) + jnp.exp2(sink2[h] - mm)
                     for h, ee, mm in zip(heads, e, m)]
            yield
            pvk = [_dot(jnp.concatenate([ee.astype(BF16) for ee in e[g * gh:(g + 1) * gh]], axis=0), vdup[g])
                   for g in range(N_KV_HEADS)]
            pv = [pvk[h // gh][(h % gh) * bq:(h % gh + 1) * bq, :] for h in heads]
            yield
            outs = [x / d for x, d in zip(pv, denom)]
            for p in pairs:
                o = jnp.where(head0, outs[2 * p], outs[2 * p + 1])
                g = g_refs[p // 2][rb, (p % 2) * LANES:(p % 2 + 1) * LANES]
                yatt_ref[rb, psl[p]] = (o * (g * _sigmoid(g))).astype(yatt_ref.dtype)
            yield
        kprev_ref[...] = kp
        vprev_ref[...] = vp

    last = pl.num_programs(0) - 1

    @pl.when(i < last)
    def _():
        proj = project_previous()
        next(proj)
        n_levels = 9 * (tm // bq)
        every = n_levels // (D_MODEL // MXU_WIDTH)
        for n, _ in enumerate(attend()):
            if n % every == every - 1:
                next(proj, None)
        for _ in proj:
            pass

    @pl.when(i == last)
    def _():
        for _ in project_previous():
            pass


def _swa_out(z_att, bucket, sinks, rel_bias, qnw, knw, x2, y_rwkv, w_out, tm):
    s, d = x2.shape
    bq = BLOCK
    nb = s // tm
    smem = pl.BlockSpec(memory_space=pltpu.SMEM)
    cur = lambda i: jnp.minimum(i, nb - 1)
    prev = lambda i: jnp.maximum(i - 1, 0)
    kv_col = D_ATT // D_KV
    gw = 2 * D_KV
    gate = lambda n: pl.BlockSpec((tm, gw), lambda i: (cur(i), (D_ATT + 2 * D_KV) // gw + n))
    return pl.pallas_call(
        _swa_out_kernel,
        grid=(nb + 1,),
        in_specs=[
            smem, smem,
            pl.BlockSpec((bq, 2 * bq), lambda i: (0, 0)),
            pl.BlockSpec((tm, D_ATT), lambda i: (cur(i), 0)),
            pl.BlockSpec((tm, D_KV), lambda i: (cur(i), kv_col)),
            pl.BlockSpec((tm, D_KV), lambda i: (cur(i), kv_col + 1)),
            gate(0), gate(1), gate(2), gate(3),
            pl.BlockSpec((1, D_ATT), lambda i: (0, 0)),
            pl.BlockSpec((1, D_KV), lambda i: (0, 0)),
            pl.BlockSpec((tm, d), lambda i: (prev(i), 0)),
            pl.BlockSpec((tm, D_RWKV), lambda i: (prev(i), 0)),
            pl.BlockSpec((D_RWKV, d), lambda i: (0, 0)),
            pl.BlockSpec((D_ATT, d), lambda i: (D_RWKV // D_ATT, 0)),
        ],
        out_specs=pl.BlockSpec((tm, d), lambda i: (prev(i), 0)),
        out_shape=jax.ShapeDtypeStruct((s, d), F32),
        scratch_shapes=[
            pltpu.VMEM((2, N_Q_HEADS, bq, 2 * bq), F32),
            pltpu.VMEM((bq, D_KV), F32),
            pltpu.VMEM((bq, D_KV), F32),
            pltpu.VMEM((tm, D_ATT), BF16),
        ],
        compiler_params=pltpu.CompilerParams(
            dimension_semantics=("arbitrary",), vmem_limit_bytes=VMEM_LIMIT),
        name="swa_out",
    )(sinks, rel_bias, bucket, z_att, z_att, z_att, z_att, z_att, z_att, z_att, qnw, knw, x2, y_rwkv, w_out, w_out)


def _t5_bucket_table():
    dist = BLOCK + np.arange(BLOCK)[:, None] - np.arange(2 * BLOCK)[None, :]
    n = np.maximum(dist, 0)
    nf = np.maximum(n, 1).astype(np.float64)
    large = MAX_EXACT + (np.log(nf / MAX_EXACT) / math.log(MAX_DISTANCE / MAX_EXACT)
                         * (N_BUCKETS - MAX_EXACT)).astype(np.int32)
    large = np.minimum(large, N_BUCKETS - 1)
    return np.where(n < MAX_EXACT, n, large).astype(np.int32)


def kernel(x, norm_w, w_in, w_out, mu_rwkv, w0, w2, a0, a2, k_k, k_a, r_k, lnx_w, lnx_b,
           q_norm_w, k_norm_w, sinks, rel_bias):
    b, s, d = x.shape
    assert (b, s, d) == (1, SEQ, D_MODEL) and norm_w.shape[0] == 1
    x2 = x.reshape(s, d)
    l = 0
    row = lambda t: t.reshape(1, -1).astype(F32)

    w_in_l = w_in[l]
    zeros_l = jnp.zeros((LORA, D_RWKV), F32)
    w2p = jnp.concatenate([w2[l], zeros_l], axis=0).astype(BF16)
    a2p = jnp.concatenate([zeros_l, a2[l]], axis=0).astype(BF16)
    qnw = jnp.tile(q_norm_w[l] * (HEAD_DIM ** -0.5 * LOG2E), N_Q_HEADS).reshape(1, D_ATT)
    knw = jnp.tile(k_norm_w[l], N_KV_HEADS).reshape(1, D_KV)
    bucket = jnp.asarray(_t5_bucket_table())

    nw = row(norm_w[l])
    z_rwkv = _proj_in(x2, nw, w_in_l, RWKV_COLS, row(mu_rwkv[l]), ROW_BLOCK, RWKV_COL_BLOCK)
    z_att, *prep, w_out_b = _att_prep(x2, nw, w_in_l, z_rwkv, row(w0[l]), w2p, row(a0[l]), a2p, row(k_k[l]),
                                      row(k_a[l]), row(r_k[l]), w_out[l], ROW_BLOCK, ATT_COL_BLOCK, PREP_TOKENS)
    y_rwkv = _rwkv(prep, z_rwkv, row(lnx_w[l]), row(lnx_b[l]))
    out = _swa_out(z_att, bucket, sinks[l].astype(F32), rel_bias.astype(F32), qnw, knw,
                   x2, y_rwkv, w_out_b, ROW_BLOCK)
    return out.reshape(b, s, d)
```
